```python
import math
import jax, jax.numpy as jnp
from jax import lax
import numpy as np

D_MODEL = 2048
BATCH = 4
SEQ = 8192
DEPTH = 1

MIX_WIDTH = D_MODEL
HEAD_DIM = 128
A_WIDTH = MIX_WIDTH // 2
A_HEADS = A_WIDTH // HEAD_DIM
A_KV_HEADS = 2
WINDOW = 128
BLOCK = 128
B_WIDTH = MIX_WIDTH - A_WIDTH
B_QK_DIM = 64
B_V_DIM = 2 * B_QK_DIM
B_HEADS = B_WIDTH // B_V_DIM
N_ATTN_HEADS = A_HEADS + B_HEADS
D_FF = -(-8 * D_MODEL // (3 * 256)) * 256
N_MOD = 6
EPS = 1e-6
NEG_INF = -1e30

A_Q_COLS = A_HEADS * HEAD_DIM
A_KV_COLS = A_KV_HEADS * HEAD_DIM
B_Q_COLS = B_HEADS * 2 * B_QK_DIM
B_K_COLS = B_HEADS * 2 * B_QK_DIM
B_V_COLS = B_HEADS * B_V_DIM
IN_COLS = A_Q_COLS + 2 * A_KV_COLS + B_Q_COLS + B_K_COLS + B_V_COLS

kernel_name = "hymba_style_window_gqa_diff_attn_swiglu_block"


def rms_norm(x, gain):
    xf = x.astype(jnp.float32)
    y = xf * lax.rsqrt(jnp.mean(xf * xf, axis=-1, keepdims=True) + EPS)
    return (y * gain.astype(jnp.float32)).astype(x.dtype)


def alibi_slopes(n):
    return 2.0 ** (-8.0 * jnp.arange(1, n + 1, dtype=jnp.float32) / n)


def lambda_init_fn(layer_idx):
    return 0.8 - 0.6 * math.exp(-0.3 * layer_idx)


def windowed_gqa(q, k, v, sink, slopes):
    b_, s_, _, dh = q.shape
    nb = s_ // BLOCK
    g_ = A_HEADS // A_KV_HEADS
    qb = q.reshape(b_, nb, BLOCK, A_KV_HEADS, g_, dh)
    pad = ((0, 0), (BLOCK, BLOCK), (0, 0), (0, 0))
    kp = jnp.pad(k, pad).reshape(b_, nb + 2, BLOCK, A_KV_HEADS, dh)
    vp = jnp.pad(v, pad).reshape(b_, nb + 2, BLOCK, A_KV_HEADS, dh)
    kb = jnp.concatenate([kp[:, :-2], kp[:, 1:-1], kp[:, 2:]], axis=2)
    vb = jnp.concatenate([vp[:, :-2], vp[:, 1:-1], vp[:, 2:]], axis=2)
    scores = jnp.einsum('bnqhgd,bnshd->bnhgqs', qb, kb).astype(jnp.float32) * (dh ** -0.5)
    qpos = jnp.arange(BLOCK)[:, None] + BLOCK
    kpos = jnp.arange(3 * BLOCK)[None, :]
    dist = jnp.abs(qpos - kpos)
    kabs = jnp.arange(nb)[:, None] * BLOCK - BLOCK + kpos
    valid = (dist <= WINDOW)[None] & ((kabs >= 0) & (kabs < s_))[:, None, :]
    bias = -slopes.reshape(A_KV_HEADS, g_)[:, :, None, None] * dist.astype(jnp.float32)
    scores = jnp.where(valid[None, :, None, None], scores + bias[None, None], NEG_INF)
    sink_b = sink.astype(jnp.float32).reshape(1, 1, A_KV_HEADS, g_, 1, 1)
    m = jnp.maximum(jnp.max(scores, axis=-1, keepdims=True), sink_b)
    p = jnp.exp(scores - m)
    probs = p / (jnp.sum(p, axis=-1, keepdims=True) + jnp.exp(sink_b - m))
    out = jnp.einsum('bnhgqs,bnshd->bnqhgd', probs.astype(v.dtype), vb)
    return out.reshape(b_, s_, A_HEADS * dh)


def diff_attention(q, k, v, lam, slopes):
    b_, s_, h_, _, dq = q.shape
    nb = s_ // BLOCK
    qblocks = jnp.moveaxis(q.reshape(b_, nb, BLOCK, h_, 2, dq), 1, 0)
    kpos = jnp.arange(s_)

    def one_block(args):
        qb, i = args
        sc = jnp.einsum('bqhcd,bshcd->bhcqs', qb, k).astype(jnp.float32) * (dq ** -0.5)
        qpos = i * BLOCK + jnp.arange(BLOCK)
        dist = jnp.abs(qpos[:, None] - kpos[None, :]).astype(jnp.float32)
        sc = sc - slopes[None, :, None, None, None] * dist[None, None, None]
        p = jax.nn.softmax(sc, axis=-1)
        w = p[:, :, 0] - lam * p[:, :, 1]
        return jnp.einsum('bhqs,bshd->bqhd', w.astype(v.dtype), v)

    out = lax.map(one_block, (qblocks, jnp.arange(nb)))
    return jnp.moveaxis(out, 0, 1).reshape(b_, s_, h_, v.shape[-1])


def setup_inputs(seed: int = 0) -> dict:
    key = jax.random.key(seed)
    ks = jax.random.split(key, 20)
    f32 = jnp.float32
    nrm = lambda k, shape, scale: jax.random.normal(k, shape, f32) * scale
    gain = lambda k, shape: 1.0 + 0.02 * jax.random.normal(k, shape, f32)
    return {
        "x": jax.random.normal(ks[0], (BATCH, SEQ, D_MODEL), f32),
        "c": jax.random.normal(ks[1], (BATCH, D_MODEL), f32),
        "w_ada": nrm(ks[2], (DEPTH, D_MODEL, N_MOD * D_MODEL), 0.5 * D_MODEL ** -0.5),
        "b_ada": nrm(ks[3], (DEPTH, N_MOD * D_MODEL), 0.01),
        "norm1_gain": gain(ks[4], (DEPTH, D_MODEL)),
        "w_in": nrm(ks[5], (DEPTH, D_MODEL, IN_COLS), D_MODEL ** -0.5),
        "a_sink": nrm(ks[6], (DEPTH, A_HEADS), 0.5),
        "a_out_gain": gain(ks[7], (DEPTH, A_WIDTH)),
        "diff_lq1": nrm(ks[8], (DEPTH, B_QK_DIM), 0.1),
        "diff_lk1": nrm(ks[9], (DEPTH, B_QK_DIM), 0.1),
        "diff_lq2": nrm(ks[10], (DEPTH, B_QK_DIM), 0.1),
        "diff_lk2": nrm(ks[11], (DEPTH, B_QK_DIM), 0.1),
        "diff_subln_gain": gain(ks[12], (DEPTH, B_V_DIM)),
        "w_o": nrm(ks[13], (DEPTH, MIX_WIDTH, D_MODEL), MIX_WIDTH ** -0.5),
        "norm2_gain": gain(ks[14], (DEPTH, D_MODEL)),
        "w_gate": nrm(ks[15], (DEPTH, D_MODEL, D_FF), D_MODEL ** -0.5),
        "w_up": nrm(ks[16], (DEPTH, D_MODEL, D_FF), D_MODEL ** -0.5),
        "w_down": nrm(ks[17], (DEPTH, D_FF, D_MODEL), D_FF ** -0.5),
        "final_gain": gain(ks[18], (D_MODEL,)),
    }


def reference(x, c, w_ada, b_ada, norm1_gain, w_in, a_sink, a_out_gain,
              diff_lq1, diff_lk1, diff_lq2, diff_lk2, diff_subln_gain, w_o,
              norm2_gain, w_gate, w_up, w_down, final_gain):
    b_, s_, _ = x.shape
    slopes = alibi_slopes(N_ATTN_HEADS)
    slopes_a, slopes_b = slopes[:A_HEADS], slopes[A_HEADS:]
    o1 = A_Q_COLS
    o2 = o1 + A_KV_COLS
    o3 = o2 + A_KV_COLS
    o4 = o3 + B_Q_COLS
    o5 = o4 + B_K_COLS
    for l in range(DEPTH):
        mod = jnp.einsum('bd,de->be', jax.nn.silu(c), w_ada[l]) + b_ada[l]
        sh1, sc1, g1, sh2, sc2, g2 = jnp.split(mod[:, None, :], N_MOD, axis=-1)

        h = rms_norm(x, norm1_gain[l]) * (1.0 + sc1) + sh1
        proj = jnp.einsum('bsd,de->bse', h, w_in[l])
        qa = proj[..., :o1].reshape(b_, s_, A_HEADS, HEAD_DIM)
        ka = proj[..., o1:o2].reshape(b_, s_, A_KV_HEADS, HEAD_DIM)
        va = proj[..., o2:o3].reshape(b_, s_, A_KV_HEADS, HEAD_DIM)
        qd = proj[..., o3:o4].reshape(b_, s_, B_HEADS, 2, B_QK_DIM)
        kd = proj[..., o4:o5].reshape(b_, s_, B_HEADS, 2, B_QK_DIM)
        vd = proj[..., o5:].reshape(b_, s_, B_HEADS, B_V_DIM)

        out_a = rms_norm(windowed_gqa(qa, ka, va, a_sink[l], slopes_a), a_out_gain[l])

        lam_init = lambda_init_fn(l)
        lam = (jnp.exp(jnp.sum(diff_lq1[l].astype(jnp.float32) * diff_lk1[l].astype(jnp.float32)))
               - jnp.exp(jnp.sum(diff_lq2[l].astype(jnp.float32) * diff_lk2[l].astype(jnp.float32)))
               + lam_init)
        od = diff_attention(qd, kd, vd, lam, slopes_b)
        out_b = (rms_norm(od, diff_subln_gain[l]) * (1.0 - lam_init)).reshape(b_, s_, B_WIDTH)

        mix = jnp.einsum('bse,ed->bsd', jnp.concatenate([out_a, out_b], axis=-1), w_o[l])
        x = x + g1 * mix

        h2 = rms_norm(x, norm2_gain[l]) * (1.0 + sc2) + sh2
        ff = jax.nn.silu(jnp.einsum('bsd,df->bsf', h2, w_gate[l])) * jnp.einsum('bsd,df->bsf', h2, w_up[l])
        x = x + g2 * jnp.einsum('bsf,fd->bsd', ff, w_down[l])
    return rms_norm(x, final_gain)
```

```python
import functools
import math

import jax
import jax.numpy as jnp
from jax import lax
from jax.experimental import pallas as pl
from jax.experimental.pallas import tpu as pltpu

HEAD_DIM = 128
A_KV_HEADS = 2
WINDOW = 128
B_QK_DIM = 64
B_V_DIM = 2 * B_QK_DIM
N_MOD = 6
EPS = 1e-6
NEG_INF = -1e30
LAM_INIT = 0.8 - 0.6 * math.exp(-0.3 * 0)

V7X_VMEM_LIMIT_BYTES = 56 * 1024 * 1024

BF16 = jnp.bfloat16
F32 = jnp.float32


def _cparams(semantics):
    return pltpu.CompilerParams(dimension_semantics=semantics,
                                vmem_limit_bytes=V7X_VMEM_LIMIT_BYTES)


def _rms(x, gain):
    return x * lax.rsqrt(jnp.mean(x * x, axis=-1, keepdims=True) + EPS) * gain


def _resident(shape):
    return pl.BlockSpec(shape, lambda *_: (0,) * len(shape), pipeline_mode=pl.Buffered(1))


def _ada_kernel(c_ref, w_ref, b_ref, o_ref):
    c = c_ref[...]
    sc = (c * jax.nn.sigmoid(c)).astype(BF16)
    o_ref[...] = jnp.dot(sc, w_ref[...].astype(BF16), preferred_element_type=F32) + b_ref[...]


def _ada(c_pad, w, b, tn=1024):
    rows, d = c_pad.shape
    n = w.shape[1]
    return pl.pallas_call(
        _ada_kernel,
        grid=(n // tn,),
        in_specs=[pl.BlockSpec((rows, d), lambda j: (0, 0)),
                  pl.BlockSpec((d, tn), lambda j: (0, j)),
                  pl.BlockSpec((1, tn), lambda j: (0, j))],
        out_specs=pl.BlockSpec((rows, tn), lambda j: (0, j)),
        out_shape=jax.ShapeDtypeStruct((rows, n), F32),
        compiler_params=_cparams(("arbitrary",)),
        name="ada_mod",
    )(c_pad, w, b)


def _inproj_kernel(x_ref, mod_ref, g_ref, w_ref,
                   qa_ref, ka_ref, va_ref, qdt_ref, kd_ref, vdt_ref, h_ref,
                   *, a_q, a_kv, b_w, chunk):
    x = x_ref[...]
    sh1 = mod_ref[0, 0:1, :]
    sc1 = mod_ref[0, 1:2, :]
    h_ref[...] = (_rms(x, g_ref[...]) * (1.0 + sc1) + sh1).astype(BF16)

    def proj(c0, width):
        return jnp.dot(h_ref[...], w_ref[:, c0:c0 + width], preferred_element_type=F32)

    o1 = a_q
    o2 = o1 + a_kv
    o3 = o2 + a_kv
    o4 = o3 + b_w
    o5 = o4 + b_w
    for c in range(0, a_q, chunk):
        qa_ref[:, c:c + chunk] = proj(c, chunk).astype(BF16)
    ka_ref[...] = proj(o1, a_kv).astype(BF16)
    va_ref[...] = proj(o2, a_kv).astype(BF16)
    for c in range(0, b_w, chunk):
        kd_ref[:, c:c + chunk] = proj(o4 + c, chunk).astype(BF16)
    qscale = B_QK_DIM ** -0.5
    for c in range(0, b_w, B_V_DIM):
        qdt_ref[c:c + B_V_DIM, :] = (proj(o3 + c, B_V_DIM) * qscale).T.astype(BF16)
        vdt_ref[c:c + B_V_DIM, :] = proj(o5 + c, B_V_DIM).T.astype(BF16)


def _inproj(x2, mod3, gain, w_bf, *, seq, a_q, a_kv, b_w, tm=512):
    t, d = x2.shape
    n = w_bf.shape[1]
    tiles_per_batch = seq // tm
    chunk = min(512, a_q, b_w)
    kern = functools.partial(_inproj_kernel, a_q=a_q, a_kv=a_kv, b_w=b_w, chunk=chunk)
    row = lambda width: pl.BlockSpec((tm, width), lambda i: (i, 0))
    col = lambda height: pl.BlockSpec((height, tm), lambda i: (0, i))
    return pl.pallas_call(
        kern,
        grid=(t // tm,),
        in_specs=[row(d),
                  pl.BlockSpec((1, N_MOD, d), lambda i: (i // tiles_per_batch, 0, 0)),
                  pl.BlockSpec((1, d), lambda i: (0, 0)),
                  _resident((d, n))],
        out_specs=[row(a_q), row(a_kv), row(a_kv), col(b_w), row(b_w), col(b_w)],
        out_shape=[jax.ShapeDtypeStruct((t, a_q), BF16),
                   jax.ShapeDtypeStruct((t, a_kv), BF16),
                   jax.ShapeDtypeStruct((t, a_kv), BF16),
                   jax.ShapeDtypeStruct((b_w, t), BF16),
                   jax.ShapeDtypeStruct((t, b_w), BF16),
                   jax.ShapeDtypeStruct((b_w, t), BF16)],
        scratch_shapes=[pltpu.VMEM((tm, d), BF16)],
        compiler_params=_cparams(("arbitrary",)),
        name="norm1_inproj",
    )(x2, mod3, gain, w_bf)


def _attn_a_kernel(slopes_ref, sink_ref, q_ref, k_ref, v_ref, g_ref, o_ref, acc_ref,
                   *, tq, kw, seq, heads):
    i = pl.program_id(1)
    q0 = i * tq
    kstart = jnp.clip(q0 - WINDOW, 0, seq - kw)
    kstart = pl.multiple_of(kstart, WINDOW)
    group = heads // A_KV_HEADS
    r = lax.broadcasted_iota(jnp.int32, (tq, kw), 0)
    c = lax.broadcasted_iota(jnp.int32, (tq, kw), 1)
    dist_i = jnp.abs((r - c) + (q0 - kstart))
    valid = dist_i <= WINDOW
    dist = dist_i.astype(F32)
    scale = HEAD_DIM ** -0.5
    for kvh in range(A_KV_HEADS):
        kwin = k_ref[pl.ds(kstart, kw), kvh * HEAD_DIM:(kvh + 1) * HEAD_DIM]
        vwin = v_ref[pl.ds(kstart, kw), kvh * HEAD_DIM:(kvh + 1) * HEAD_DIM]
        for gi in range(group):
            h = kvh * group + gi
            q = q_ref[:, h * HEAD_DIM:(h + 1) * HEAD_DIM]
            s = lax.dot_general(q, kwin, (((1,), (1,)), ((), ())),
                                preferred_element_type=F32) * scale
            s = jnp.where(valid, s - slopes_ref[h] * dist, NEG_INF)
            sink = sink_ref[h]
            m = jnp.maximum(jnp.max(s, axis=-1, keepdims=True), sink)
            p = jnp.exp(s - m)
            denom = jnp.sum(p, axis=-1, keepdims=True) + jnp.exp(sink - m)
            probs = (p / denom).astype(BF16)
            acc_ref[:, h * HEAD_DIM:(h + 1) * HEAD_DIM] = jnp.dot(
                probs, vwin, preferred_element_type=F32)
    o_ref[...] = _rms(acc_ref[...], g_ref[...]).astype(BF16)


def _attn_a(slopes_a, sink, qa, ka, va, gain, *, batch, seq, tq=128):
    t, a_q = qa.shape
    a_kv = ka.shape[1]
    heads = a_q // HEAD_DIM
    kw = tq + 2 * WINDOW
    nq = seq // tq
    kern = functools.partial(_attn_a_kernel, tq=tq, kw=kw, seq=seq, heads=heads)
    smem = pl.BlockSpec(memory_space=pltpu.SMEM)
    return pl.pallas_call(
        kern,
        grid=(batch, nq),
        in_specs=[smem, smem,
                  pl.BlockSpec((tq, a_q), lambda b, i: (b * nq + i, 0)),
                  pl.BlockSpec((seq, a_kv), lambda b, i: (b, 0)),
                  pl.BlockSpec((seq, a_kv), lambda b, i: (b, 0)),
                  pl.BlockSpec((1, a_q), lambda b, i: (0, 0))],
        out_specs=pl.BlockSpec((tq, a_q), lambda b, i: (b * nq + i, 0)),
        out_shape=jax.ShapeDtypeStruct((t, a_q), BF16),
        scratch_shapes=[pltpu.VMEM((tq, a_q), F32)],
        compiler_params=_cparams(("arbitrary", "arbitrary")),
        name="attn_window_gqa",
    )(slopes_a, sink, qa, ka, va, gain)


def _attn_b_kernel(slopes_ref, lq1_ref, lk1_ref, lq2_ref, lk2_ref, g_ref, d0_ref,
                   qt_ref, k_ref, vt_ref, o_ref,
                   qbd_ref, m_ref, l_ref, acc_ref, *, tq, tk, seq):
    h = pl.program_id(1)
    slope = slopes_ref[h]
    lam = (jnp.exp(jnp.sum(lq1_ref[...] * lk1_ref[...], keepdims=True))
           - jnp.exp(jnp.sum(lq2_ref[...] * lk2_ref[...], keepdims=True)) + LAM_INIT)
    qbd_ref[...] = jnp.zeros_like(qbd_ref)
    nq = seq // tq
    nk = seq // tk

    def q_body(i, carry):
        q0 = pl.multiple_of(i * tq, tq)
        qbd_ref[0:B_QK_DIM, 0:tq] = qt_ref[0:B_QK_DIM, pl.ds(q0, tq)]
        qbd_ref[B_QK_DIM:, tq:] = qt_ref[B_QK_DIM:, pl.ds(q0, tq)]
        m_ref[...] = jnp.full_like(m_ref, NEG_INF)
        l_ref[...] = jnp.zeros_like(l_ref)
        acc_ref[...] = jnp.zeros_like(acc_ref)

        def kv_body(j, c2):
            k0 = pl.multiple_of(j * tk, tk)
            s = jnp.dot(k_ref[pl.ds(k0, tk), :], qbd_ref[...], preferred_element_type=F32)
            off = (k0 - q0).astype(F32)
            s = s - slope * jnp.abs(d0_ref[...] + off)
            m_old = m_ref[...]
            m_new = jnp.maximum(m_old, jnp.max(s, axis=0, keepdims=True))
            alpha = jnp.exp(m_old - m_new)
            p = jnp.exp(s - m_new)
            l_ref[...] = alpha * l_ref[...] + jnp.sum(p, axis=0, keepdims=True)
            acc_ref[...] = alpha * acc_ref[...] + jnp.dot(
                vt_ref[:, pl.ds(k0, tk)], p.astype(BF16), preferred_element_type=F32)
            m_ref[...] = m_new
            return c2

        lax.fori_loop(0, nk, kv_body, 0)
        o = acc_ref[...] / l_ref[...]
        od = (o[:, :tq] - lam * o[:, tq:]).T
        o_ref[pl.ds(q0, tq), :] = (_rms(od, g_ref[...]) * (1.0 - LAM_INIT)).astype(BF16)
        return carry

    lax.fori_loop(0, nq, q_body, 0)


def _attn_b(slopes_b, lq1, lk1, lq2, lk2, gain, qdt, kd, vdt, *, batch, seq, tq=256, tk=256):
    b_w, t = qdt.shape
    heads = b_w // B_V_DIM
    tq = min(tq, seq)
    tk = min(tk, seq)
    r = lax.broadcasted_iota(jnp.int32, (tk, 2 * tq), 0)
    c = lax.broadcasted_iota(jnp.int32, (tk, 2 * tq), 1) % tq
    d0 = (r - c).astype(F32)
    kern = functools.partial(_attn_b_kernel, tq=tq, tk=tk, seq=seq)
    smem = pl.BlockSpec(memory_space=pltpu.SMEM)
    vec = lambda n: pl.BlockSpec((1, n), lambda b, h: (0, 0))
    tposed = pl.BlockSpec((B_V_DIM, seq), lambda b, h: (h, b))
    natural = pl.BlockSpec((seq, B_V_DIM), lambda b, h: (b, h))
    return pl.pallas_call(
        kern,
        grid=(batch, heads),
        in_specs=[smem, vec(B_QK_DIM), vec(B_QK_DIM), vec(B_QK_DIM), vec(B_QK_DIM),
                  vec(B_V_DIM),
                  pl.BlockSpec((tk, 2 * tq), lambda b, h: (0, 0)),
                  tposed, natural, tposed],
        out_specs=natural,
        out_shape=jax.ShapeDtypeStruct((t, b_w), BF16),
        scratch_shapes=[pltpu.VMEM((B_V_DIM, 2 * tq), BF16),
                        pltpu.VMEM((1, 2 * tq), F32),
                        pltpu.VMEM((1, 2 * tq), F32),
                        pltpu.VMEM((B_V_DIM, 2 * tq), F32)],
        compiler_params=_cparams(("arbitrary", "arbitrary")),
        name="attn_diff",
    )(slopes_b, lq1, lk1, lq2, lk2, gain, d0, qdt, kd, vdt)


def _outproj_kernel(a_ref, b_ref, w_ref, x_ref, mod_ref, g_ref, x1_ref, h2_ref, *, a_w):
    mix = (jnp.dot(a_ref[...], w_ref[0:a_w, :], preferred_element_type=F32)
           + jnp.dot(b_ref[...], w_ref[a_w:, :], preferred_element_type=F32))
    g1 = mod_ref[0, 2:3, :]
    sh2 = mod_ref[0, 3:4, :]
    sc2 = mod_ref[0, 4:5, :]
    x1 = x_ref[...] + g1 * mix
    x1_ref[...] = x1
    h2_ref[...] = (_rms(x1, g_ref[...]) * (1.0 + sc2) + sh2).astype(BF16)


def _outproj(out_a, out_b, w_bf, x2, mod3, gain, *, seq, tm=512):
    t, d = x2.shape
    a_w = out_a.shape[1]
    b_w = out_b.shape[1]
    tiles_per_batch = seq // tm
    row = lambda width: pl.BlockSpec((tm, width), lambda i: (i, 0))
    return pl.pallas_call(
        functools.partial(_outproj_kernel, a_w=a_w),
        grid=(t // tm,),
        in_specs=[row(a_w), row(b_w), _resident((a_w + b_w, d)), row(d),
                  pl.BlockSpec((1, N_MOD, d), lambda i: (i // tiles_per_batch, 0, 0)),
                  pl.BlockSpec((1, d), lambda i: (0, 0))],
        out_specs=[row(d), row(d)],
        out_shape=[jax.ShapeDtypeStruct((t, d), F32), jax.ShapeDtypeStruct((t, d), BF16)],
        compiler_params=_cparams(("arbitrary",)),
        name="outproj_norm2",
    )(out_a, out_b, w_bf, x2, mod3, gain)


def _ffn_kernel(h_ref, wg_ref, wu_ref, wd_ref, x1_ref, mod_ref, fg_ref, o_ref, acc_ref):
    f = pl.program_id(1)

    @pl.when(f == 0)
    def _():
        acc_ref[...] = jnp.zeros_like(acc_ref)

    h = h_ref[...]
    g = jnp.dot(h, wg_ref[...], preferred_element_type=F32)
    u = jnp.dot(h, wu_ref[...], preferred_element_type=F32)
    a = (g * jax.nn.sigmoid(g) * u).astype(BF16)
    acc_ref[...] += jnp.dot(a, wd_ref[...], preferred_element_type=F32)

    @pl.when(f == pl.num_programs(1) - 1)
    def _():
        g2 = mod_ref[0, 5:6, :]
        x2 = x1_ref[...] + g2 * acc_ref[...]
        o_ref[...] = _rms(x2, fg_ref[...])


def _ffn(h2, wg, wu, wd, x1, mod3, final_gain, *, seq, tm=512, tf=512):
    t, d = h2.shape
    ff = wg.shape[1]
    if ff % tf:
        tf = 256
    assert ff % tf == 0 and t % tm == 0 and seq % tm == 0
    tiles_per_batch = seq // tm
    return pl.pallas_call(
        _ffn_kernel,
        grid=(t // tm, ff // tf),
        in_specs=[pl.BlockSpec((tm, d), lambda i, f: (i, 0)),
                  pl.BlockSpec((d, tf), lambda i, f: (0, f)),
                  pl.BlockSpec((d, tf), lambda i, f: (0, f)),
                  pl.BlockSpec((tf, d), lambda i, f: (f, 0)),
                  pl.BlockSpec((tm, d), lambda i, f: (i, 0)),
                  pl.BlockSpec((1, N_MOD, d), lambda i, f: (i // tiles_per_batch, 0, 0)),
                  pl.BlockSpec((1, d), lambda i, f: (0, 0))],
        out_specs=pl.BlockSpec((tm, d), lambda i, f: (i, 0)),
        out_shape=jax.ShapeDtypeStruct((t, d), F32),
        scratch_shapes=[pltpu.VMEM((tm, d), F32)],
        compiler_params=_cparams(("arbitrary", "arbitrary")),
        name="swiglu_ffn_final_norm",
    )(h2, wg, wu, wd, x1, mod3, final_gain)


def kernel(x, c, w_ada, b_ada, norm1_gain, w_in, a_sink, a_out_gain, diff_lq1, diff_lk1,
           diff_lq2, diff_lk2, diff_subln_gain, w_o, norm2_gain, w_gate, w_up, w_down,
           final_gain):
    batch, seq, d = x.shape
    assert w_ada.shape[0] == 1, "single-layer block"
    a_w = d // 2
    b_w = d - a_w
    a_heads = a_w // HEAD_DIM
    b_heads = b_w // B_V_DIM
    a_kv = A_KV_HEADS * HEAD_DIM
    n_heads = a_heads + b_heads
    slopes = 2.0 ** (-8.0 * jnp.arange(1, n_heads + 1, dtype=F32) / n_heads)

    rows = 8
    c_pad = jnp.zeros((rows, d), F32).at[:batch].set(c)
    mod = _ada(c_pad, w_ada[0], b_ada[0][None, :])[:batch]
    mod3 = mod.reshape(batch, N_MOD, d)

    x2 = x.reshape(batch * seq, d)
    qa, ka, va, qdt, kd, vdt = _inproj(
        x2, mod3, norm1_gain[0][None, :], w_in[0].astype(BF16),
        seq=seq, a_q=a_w, a_kv=a_kv, b_w=b_w)

    out_a = _attn_a(slopes[:a_heads], a_sink[0].astype(F32), qa, ka, va,
                    a_out_gain[0][None, :], batch=batch, seq=seq)
    out_b = _attn_b(slopes[a_heads:], diff_lq1[0][None, :], diff_lk1[0][None, :],
                    diff_lq2[0][None, :], diff_lk2[0][None, :], diff_subln_gain[0][None, :],
                    qdt, kd, vdt, batch=batch, seq=seq)

    x1, h2 = _outproj(out_a, out_b, w_o[0].astype(BF16), x2, mod3, norm2_gain[0][None, :],
                      seq=seq)
    out = _ffn(h2, w_gate[0].astype(BF16), w_up[0].astype(BF16), w_down[0].astype(BF16),
               x1, mod3, final_gain[None, :], seq=seq)
    return out.reshape(batch, seq, d)
```

```python
import functools
import math

import jax
import jax.numpy as jnp
from jax import lax
from jax.experimental import pallas as pl
from jax.experimental.pallas import tpu as pltpu

HEAD_DIM = 128
A_KV_HEADS = 2
WINDOW = 128
B_QK_DIM = 64
B_V_DIM = 2 * B_QK_DIM
N_MOD = 6
EPS = 1e-6
NEG_INF = -1e30
LAM_INIT = 0.8 - 0.6 * math.exp(-0.3 * 0)
LOG2E = math.log2(math.e)

V7X_VMEM_LIMIT_BYTES = 56 * 1024 * 1024

BF16 = jnp.bfloat16
F32 = jnp.float32


def _cparams(semantics):
    return pltpu.CompilerParams(dimension_semantics=semantics,
                                vmem_limit_bytes=V7X_VMEM_LIMIT_BYTES)


def _rms(x, gain):
    return x * lax.rsqrt(jnp.mean(x * x, axis=-1, keepdims=True) + EPS) * gain


def _resident(shape):
    return pl.BlockSpec(shape, lambda *_: (0,) * len(shape), pipeline_mode=pl.Buffered(1))


def _ada_kernel(c_ref, w_ref, b_ref, o_ref):
    c = c_ref[...]
    sc = (c * jax.nn.sigmoid(c)).astype(BF16)
    o_ref[...] = jnp.dot(sc, w_ref[...].astype(BF16), preferred_element_type=F32) + b_ref[...]


def _ada(c_pad, w, b, tn=1024):
    rows, d = c_pad.shape
    n = w.shape[1]
    return pl.pallas_call(
        _ada_kernel,
        grid=(n // tn,),
        in_specs=[pl.BlockSpec((rows, d), lambda j: (0, 0)),
                  pl.BlockSpec((d, tn), lambda j: (0, j)),
                  pl.BlockSpec((1, tn), lambda j: (0, j))],
        out_specs=pl.BlockSpec((rows, tn), lambda j: (0, j)),
        out_shape=jax.ShapeDtypeStruct((rows, n), F32),
        compiler_params=_cparams(("arbitrary",)),
        name="ada_mod",
    )(c_pad, w, b)


def _inproj_kernel(x_ref, mod_ref, g_ref, w_ref,
                   qa_ref, ka_ref, va_ref, qdt_ref, kd_ref, vdt_ref, h_ref,
                   *, a_q, a_kv, b_w, chunk):
    x = x_ref[...]
    sh1 = mod_ref[0, 0:1, :]
    sc1 = mod_ref[0, 1:2, :]
    h_ref[...] = (_rms(x, g_ref[...]) * (1.0 + sc1) + sh1).astype(BF16)

    def proj(c0, width):
        return jnp.dot(h_ref[...], w_ref[:, c0:c0 + width], preferred_element_type=F32)

    o1 = a_q
    o2 = o1 + a_kv
    o3 = o2 + a_kv
    o4 = o3 + b_w
    o5 = o4 + b_w
    for c in range(0, a_q, chunk):
        qa_ref[:, c:c + chunk] = proj(c, chunk).astype(BF16)
    ka_ref[...] = proj(o1, a_kv).astype(BF16)
    va_ref[...] = proj(o2, a_kv).astype(BF16)
    for c in range(0, b_w, chunk):
        kd_ref[:, c:c + chunk] = proj(o4 + c, chunk).astype(BF16)
    qscale = B_QK_DIM ** -0.5 * LOG2E
    for c in range(0, b_w, B_V_DIM):
        qdt_ref[c:c + B_V_DIM, :] = (proj(o3 + c, B_V_DIM) * qscale).T.astype(BF16)
        vdt_ref[c:c + B_V_DIM, :] = proj(o5 + c, B_V_DIM).T.astype(BF16)


def _inproj(x2, mod3, gain, w_bf, *, seq, a_q, a_kv, b_w, tm=512):
    t, d = x2.shape
    n = w_bf.shape[1]
    tiles_per_batch = seq // tm
    chunk = min(512, a_q, b_w)
    kern = functools.partial(_inproj_kernel, a_q=a_q, a_kv=a_kv, b_w=b_w, chunk=chunk)
    row = lambda width: pl.BlockSpec((tm, width), lambda i: (i, 0))
    col = lambda height: pl.BlockSpec((height, tm), lambda i: (0, i))
    return pl.pallas_call(
        kern,
        grid=(t // tm,),
        in_specs=[row(d),
                  pl.BlockSpec((1, N_MOD, d), lambda i: (i // tiles_per_batch, 0, 0)),
                  pl.BlockSpec((1, d), lambda i: (0, 0)),
                  _resident((d, n))],
        out_specs=[row(a_q), row(a_kv), row(a_kv), col(b_w), row(b_w), col(b_w)],
        out_shape=[jax.ShapeDtypeStruct((t, a_q), BF16),
                   jax.ShapeDtypeStruct((t, a_kv), BF16),
                   jax.ShapeDtypeStruct((t, a_kv), BF16),
                   jax.ShapeDtypeStruct((b_w, t), BF16),
                   jax.ShapeDtypeStruct((t, b_w), BF16),
                   jax.ShapeDtypeStruct((b_w, t), BF16)],
        scratch_shapes=[pltpu.VMEM((tm, d), BF16)],
        compiler_params=_cparams(("arbitrary",)),
        name="norm1_inproj",
    )(x2, mod3, gain, w_bf)


def _attn_a_kernel(slopes_ref, sink_ref, q_ref, k_ref, v_ref, g_ref, o_ref, acc_ref,
                   *, tq, kw, seq, heads):
    i = pl.program_id(1)
    q0 = i * tq
    kstart = jnp.clip(q0 - WINDOW, 0, seq - kw)
    kstart = pl.multiple_of(kstart, WINDOW)
    group = heads // A_KV_HEADS
    r = lax.broadcasted_iota(jnp.int32, (tq, kw), 0)
    c = lax.broadcasted_iota(jnp.int32, (tq, kw), 1)
    dist_i = jnp.abs((r - c) + (q0 - kstart))
    valid = dist_i <= WINDOW
    dist = dist_i.astype(F32)
    scale = HEAD_DIM ** -0.5
    for kvh in range(A_KV_HEADS):
        kwin = k_ref[pl.ds(kstart, kw), kvh * HEAD_DIM:(kvh + 1) * HEAD_DIM]
        vwin = v_ref[pl.ds(kstart, kw), kvh * HEAD_DIM:(kvh + 1) * HEAD_DIM]
        for gi in range(group):
            h = kvh * group + gi
            q = q_ref[:, h * HEAD_DIM:(h + 1) * HEAD_DIM]
            s = lax.dot_general(q, kwin, (((1,), (1,)), ((), ())),
                                preferred_element_type=F32) * scale
            s = jnp.where(valid, s - slopes_ref[h] * dist, NEG_INF)
            sink = sink_ref[h]
            m = jnp.maximum(jnp.max(s, axis=-1, keepdims=True), sink)
            p = jnp.exp(s - m)
            denom = jnp.sum(p, axis=-1, keepdims=True) + jnp.exp(sink - m)
            probs = (p / denom).astype(BF16)
            acc_ref[:, h * HEAD_DIM:(h + 1) * HEAD_DIM] = jnp.dot(
                probs, vwin, preferred_element_type=F32)
    o_ref[...] = _rms(acc_ref[...], g_ref[...]).astype(BF16)


def _attn_a(slopes_a, sink, qa, ka, va, gain, *, batch, seq, tq=128):
    t, a_q = qa.shape
    a_kv = ka.shape[1]
    heads = a_q // HEAD_DIM
    kw = tq + 2 * WINDOW
    nq = seq // tq
    kern = functools.partial(_attn_a_kernel, tq=tq, kw=kw, seq=seq, heads=heads)
    smem = pl.BlockSpec(memory_space=pltpu.SMEM)
    return pl.pallas_call(
        kern,
        grid=(batch, nq),
        in_specs=[smem, smem,
                  pl.BlockSpec((tq, a_q), lambda b, i: (b * nq + i, 0)),
                  pl.BlockSpec((seq, a_kv), lambda b, i: (b, 0)),
                  pl.BlockSpec((seq, a_kv), lambda b, i: (b, 0)),
                  pl.BlockSpec((1, a_q), lambda b, i: (0, 0))],
        out_specs=pl.BlockSpec((tq, a_q), lambda b, i: (b * nq + i, 0)),
        out_shape=jax.ShapeDtypeStruct((t, a_q), BF16),
        scratch_shapes=[pltpu.VMEM((tq, a_q), F32)],
        compiler_params=_cparams(("arbitrary", "arbitrary")),
        name="attn_window_gqa",
    )(slopes_a, sink, qa, ka, va, gain)


def _attn_b_kernel(slopes_ref, lq1_ref, lk1_ref, lq2_ref, lk2_ref, g_ref, d0_ref,
                   qt_ref, k_ref, vt_ref, o_ref,
                   w_ref, bias_ref, s0_ref, s1_ref, m_ref, l_ref, acc_ref, *, tile, sw, seq):
    h = pl.program_id(1)
    slope2 = slopes_ref[h] * LOG2E
    lam = (jnp.exp(jnp.sum(lq1_ref[...] * lk1_ref[...], keepdims=True))
           - jnp.exp(jnp.sum(lq2_ref[...] * lk2_ref[...], keepdims=True)) + LAM_INIT)
    halves = tile // sw
    n_tiles = seq // tile
    w_ref[...] = jnp.zeros_like(w_ref)
    sd = slope2 * d0_ref[...]
    bias_ref[0] = sd
    bias_ref[1] = -sd
    for half in range(halves):
        bias_ref[2 + half] = -jnp.abs(sd - slope2 * (half * sw))

    def scores(j, s_ref):
        k0 = pl.multiple_of(j * tile, tile)
        s_ref[...] = jnp.dot(k_ref[pl.ds(k0, tile), :], w_ref[...], preferred_element_type=F32)

    def softmax_pv(i, j, s_ref):
        k0 = pl.multiple_of(j * tile, tile)
        q0 = i * tile
        p_strips = []
        alphas = []
        for half in range(halves):
            cst = slope2 * (k0 - q0 - half * sw).astype(F32)
            idx = jnp.where(j < i, 0, jnp.where(j > i, 1, 2 + half))
            tc = jnp.where(j < i, cst, jnp.where(j > i, -cst, 0.0))
            for comp in range(2):
                cols = slice((2 * half + comp) * sw, (2 * half + comp + 1) * sw)
                t = s_ref[:, cols] + bias_ref[idx]
                m_old = m_ref[:, cols]
                m_new = jnp.maximum(m_old, jnp.max(t, axis=0, keepdims=True) + tc)
                alpha = jnp.exp2(m_old - m_new)
                p = jnp.exp2(t - (m_new - tc))
                l_ref[:, cols] = alpha * l_ref[:, cols] + jnp.sum(p, axis=0, keepdims=True)
                m_ref[:, cols] = m_new
                p_strips.append(p.astype(BF16))
                alphas.append(alpha)
        pv = jnp.dot(vt_ref[:, pl.ds(k0, tile)], jnp.concatenate(p_strips, axis=1),
                     preferred_element_type=F32)
        acc_ref[...] = jnp.concatenate(alphas, axis=1) * acc_ref[...] + pv

    def q_body(i, carry):
        q0 = pl.multiple_of(i * tile, tile)
        for half in range(halves):
            qcols = pl.ds(q0 + half * sw, sw)
            c0 = 2 * half * sw
            w_ref[0:B_QK_DIM, c0:c0 + sw] = qt_ref[0:B_QK_DIM, qcols]
            w_ref[B_QK_DIM:, c0 + sw:c0 + 2 * sw] = qt_ref[B_QK_DIM:, qcols]
        m_ref[...] = jnp.full_like(m_ref, NEG_INF)
        l_ref[...] = jnp.zeros_like(l_ref)
        acc_ref[...] = jnp.zeros_like(acc_ref)

        scores(0, s0_ref)

        def pair(jj, c2):
            j = 2 * jj
            scores(j + 1, s1_ref)
            softmax_pv(i, j, s0_ref)
            scores(j + 2, s0_ref)
            softmax_pv(i, j + 1, s1_ref)
            return c2

        lax.fori_loop(0, n_tiles // 2 - 1, pair, 0)
        scores(n_tiles - 1, s1_ref)
        softmax_pv(i, n_tiles - 2, s0_ref)
        softmax_pv(i, n_tiles - 1, s1_ref)

        o = acc_ref[...] / l_ref[...]
        for half in range(halves):
            c0 = 2 * half * sw
            od = (o[:, c0:c0 + sw] - lam * o[:, c0 + sw:c0 + 2 * sw]).T
            o_ref[pl.ds(q0 + half * sw, sw), :] = (
                _rms(od, g_ref[...]) * (1.0 - LAM_INIT)).astype(BF16)
        return carry

    lax.fori_loop(0, n_tiles, q_body, 0)


def _attn_b(slopes_b, lq1, lk1, lq2, lk2, gain, qdt, kd, vdt, *, batch, seq, tile=512, sw=256):
    b_w, t = qdt.shape
    heads = b_w // B_V_DIM
    tile = min(tile, seq // 2)
    assert seq % (2 * tile) == 0 and tile % sw == 0
    r = lax.broadcasted_iota(jnp.int32, (tile, sw), 0)
    c = lax.broadcasted_iota(jnp.int32, (tile, sw), 1)
    d0 = (r - c).astype(F32)
    kern = functools.partial(_attn_b_kernel, tile=tile, sw=sw, seq=seq)
    smem = pl.BlockSpec(memory_space=pltpu.SMEM)
    vec = lambda n: pl.BlockSpec((1, n), lambda b, h: (0, 0))
    tposed = pl.BlockSpec((B_V_DIM, seq), lambda b, h: (h, b))
    natural = pl.BlockSpec((seq, B_V_DIM), lambda b, h: (b, h))
    return pl.pallas_call(
        kern,
        grid=(batch, heads),
        in_specs=[smem, vec(B_QK_DIM), vec(B_QK_DIM), vec(B_QK_DIM), vec(B_QK_DIM),
                  vec(B_V_DIM),
                  pl.BlockSpec((tile, sw), lambda b, h: (0, 0)),
                  tposed, natural, tposed],
        out_specs=natural,
        out_shape=jax.ShapeDtypeStruct((t, b_w), BF16),
        scratch_shapes=[pltpu.VMEM((B_V_DIM, 2 * tile), BF16),
                        pltpu.VMEM((2 + tile // sw, tile, sw), F32),
                        pltpu.VMEM((tile, 2 * tile), F32),
                        pltpu.VMEM((tile, 2 * tile), F32),
                        pltpu.VMEM((1, 2 * tile), F32),
                        pltpu.VMEM((1, 2 * tile), F32),
                        pltpu.VMEM((B_V_DIM, 2 * tile), F32)],
        compiler_params=_cparams(("arbitrary", "arbitrary")),
        name="attn_diff",
    )(slopes_b, lq1, lk1, lq2, lk2, gain, d0, qdt, kd, vdt)


def _outproj_kernel(a_ref, b_ref, w_ref, x_ref, mod_ref, g_ref, x1_ref, h2_ref, *, a_w):
    mix = (jnp.dot(a_ref[...], w_ref[0:a_w, :], preferred_element_type=F32)
           + jnp.dot(b_ref[...], w_ref[a_w:, :], preferred_element_type=F32))
    g1 = mod_ref[0, 2:3, :]
    sh2 = mod_ref[0, 3:4, :]
    sc2 = mod_ref[0, 4:5, :]
    x1 = x_ref[...] + g1 * mix
    x1_ref[...] = x1
    h2_ref[...] = (_rms(x1, g_ref[...]) * (1.0 + sc2) + sh2).astype(BF16)


def _outproj(out_a, out_b, w_bf, x2, mod3, gain, *, seq, tm=512):
    t, d = x2.shape
    a_w = out_a.shape[1]
    b_w = out_b.shape[1]
    tiles_per_batch = seq // tm
    row = lambda width: pl.BlockSpec((tm, width), lambda i: (i, 0))
    return pl.pallas_call(
        functools.partial(_outproj_kernel, a_w=a_w),
        grid=(t // tm,),
        in_specs=[row(a_w), row(b_w), _resident((a_w + b_w, d)), row(d),
                  pl.BlockSpec((1, N_MOD, d), lambda i: (i // tiles_per_batch, 0, 0)),
                  pl.BlockSpec((1, d), lambda i: (0, 0))],
        out_specs=[row(d), row(d)],
        out_shape=[jax.ShapeDtypeStruct((t, d), F32), jax.ShapeDtypeStruct((t, d), BF16)],
        compiler_params=_cparams(("arbitrary",)),
        name="outproj_norm2",
    )(out_a, out_b, w_bf, x2, mod3, gain)


def _ffn_kernel(h_ref, wg_ref, wu_ref, wd_ref, x1_ref, mod_ref, fg_ref, o_ref, acc_ref):
    f = pl.program_id(1)

    @pl.when(f == 0)
    def _():
        acc_ref[...] = jnp.zeros_like(acc_ref)

    h = h_ref[...]
    g = jnp.dot(h, wg_ref[...], preferred_element_type=F32)
    u = jnp.dot(h, wu_ref[...], preferred_element_type=F32)
    a = (g * jax.nn.sigmoid(g) * u).astype(BF16)
    acc_ref[...] += jnp.dot(a, wd_ref[...], preferred_element_type=F32)

    @pl.when(f == pl.num_programs(1) - 1)
    def _():
        g2 = mod_ref[0, 5:6, :]
        x2 = x1_ref[...] + g2 * acc_ref[...]
        o_ref[...] = _rms(x2, fg_ref[...])


def _ffn(h2, wg, wu, wd, x1, mod3, final_gain, *, seq, tm=512, tf=512):
    t, d = h2.shape
    ff = wg.shape[1]
    if ff % tf:
        tf = 256
    assert ff % tf == 0 and t % tm == 0 and seq % tm == 0
    tiles_per_batch = seq // tm
    return pl.pallas_call(
        _ffn_kernel,
        grid=(t // tm, ff // tf),
        in_specs=[pl.BlockSpec((tm, d), lambda i, f: (i, 0)),
                  pl.BlockSpec((d, tf), lambda i, f: (0, f)),
                  pl.BlockSpec((d, tf), lambda i, f: (0, f)),
                  pl.BlockSpec((tf, d), lambda i, f: (f, 0)),
                  pl.BlockSpec((tm, d), lambda i, f: (i, 0)),
                  pl.BlockSpec((1, N_MOD, d), lambda i, f: (i // tiles_per_batch, 0, 0)),
                  pl.BlockSpec((1, d), lambda i, f: (0, 0))],
        out_specs=pl.BlockSpec((tm, d), lambda i, f: (i, 0)),
        out_shape=jax.ShapeDtypeStruct((t, d), F32),
        scratch_shapes=[pltpu.VMEM((tm, d), F32)],
        compiler_params=_cparams(("arbitrary", "arbitrary")),
        name="swiglu_ffn_final_norm",
    )(h2, wg, wu, wd, x1, mod3, final_gain)


def kernel(x, c, w_ada, b_ada, norm1_gain, w_in, a_sink, a_out_gain, diff_lq1, diff_lk1,
           diff_lq2, diff_lk2, diff_subln_gain, w_o, norm2_gain, w_gate, w_up, w_down,
           final_gain):
    batch, seq, d = x.shape
    assert w_ada.shape[0] == 1, "single-layer block"
    a_w = d // 2
    b_w = d - a_w
    a_heads = a_w // HEAD_DIM
    b_heads = b_w // B_V_DIM
    a_kv = A_KV_HEADS * HEAD_DIM
    n_heads = a_heads + b_heads
    slopes = 2.0 ** (-8.0 * jnp.arange(1, n_heads + 1, dtype=F32) / n_heads)

    rows = 8
    c_pad = jnp.zeros((rows, d), F32).at[:batch].set(c)
    mod = _ada(c_pad, w_ada[0], b_ada[0][None, :])[:batch]
    mod3 = mod.reshape(batch, N_MOD, d)

    x2 = x.reshape(batch * seq, d)
    qa, ka, va, qdt, kd, vdt = _inproj(
        x2, mod3, norm1_gain[0][None, :], w_in[0].astype(BF16),
        seq=seq, a_q=a_w, a_kv=a_kv, b_w=b_w)

    out_a = _attn_a(slopes[:a_heads], a_sink[0].astype(F32), qa, ka, va,
                    a_out_gain[0][None, :], batch=batch, seq=seq)
    out_b = _attn_b(slopes[a_heads:], diff_lq1[0][None, :], diff_lk1[0][None, :],
                    diff_lq2[0][None, :], diff_lk2[0][None, :], diff_subln_gain[0][None, :],
                    qdt, kd, vdt, batch=batch, seq=seq)

    x1, h2 = _outproj(out_a, out_b, w_o[0].astype(BF16), x2, mod3, norm2_gain[0][None, :],
                      seq=seq)
    out = _ffn(h2, w_gate[0].astype(BF16), w_up[0].astype(BF16), w_down[0].astype(BF16),
               x1, mod3, final_gain[None, :], seq=seq)
    return out.reshape(batch, seq, d)
```

```python
import functools
import math

import jax
import jax.numpy as jnp
from jax import lax
from jax.experimental import pallas as pl
from jax.experimental.pallas import tpu as pltpu

HEAD_DIM = 128
A_KV_HEADS = 2
WINDOW = 128
B_QK_DIM = 64
B_V_DIM = 2 * B_QK_DIM
N_MOD = 6
EPS = 1e-6
NEG_INF = -1e30
LAM_INIT = 0.8 - 0.6 * math.exp(-0.3 * 0)
LOG2E = math.log2(math.e)
GROUP = 6
N_BIAS_ROWS = 16

V7X_VMEM_LIMIT_BYTES = 56 * 1024 * 1024

BF16 = jnp.bfloat16
F32 = jnp.float32


def _cparams(semantics):
    return pltpu.CompilerParams(dimension_semantics=semantics,
                                vmem_limit_bytes=V7X_VMEM_LIMIT_BYTES)


def _rms(x, gain):
    return x * lax.rsqrt(jnp.mean(x * x, axis=-1, keepdims=True) + EPS) * gain


def _resident(shape):
    return pl.BlockSpec(shape, lambda *_: (0,) * len(shape), pipeline_mode=pl.Buffered(1))


def _ada_kernel(c_ref, w_ref, b_ref, o_ref):
    c = c_ref[...]
    sc = (c * jax.nn.sigmoid(c)).astype(BF16)
    o_ref[...] = jnp.dot(sc, w_ref[...].astype(BF16), preferred_element_type=F32) + b_ref[...]


def _ada(c_pad, w, b, tn=1024):
    rows, d = c_pad.shape
    n = w.shape[1]
    return pl.pallas_call(
        _ada_kernel,
        grid=(n // tn,),
        in_specs=[pl.BlockSpec((rows, d), lambda j: (0, 0)),
                  pl.BlockSpec((d, tn), lambda j: (0, j)),
                  pl.BlockSpec((1, tn), lambda j: (0, j))],
        out_specs=pl.BlockSpec((rows, tn), lambda j: (0, j)),
        out_shape=jax.ShapeDtypeStruct((rows, n), F32),
        compiler_params=_cparams(("arbitrary",)),
        name="ada_mod",
    )(c_pad, w, b)


def _inproj_kernel(x_ref, mod_ref, g_ref, w_ref,
                   qa_ref, ka_ref, va_ref, qdt_ref, kd_ref, vdt_ref, h_ref,
                   *, a_q, a_kv, b_w, chunk):
    x = x_ref[...]
    sh1 = mod_ref[0, 0:1, :]
    sc1 = mod_ref[0, 1:2, :]
    h_ref[...] = (_rms(x, g_ref[...]) * (1.0 + sc1) + sh1).astype(BF16)

    def proj(c0, width):
        return jnp.dot(h_ref[...], w_ref[:, c0:c0 + width], preferred_element_type=F32)

    o1 = a_q
    o2 = o1 + a_kv
    o3 = o2 + a_kv
    o4 = o3 + b_w
    o5 = o4 + b_w
    for c in range(0, a_q, chunk):
        qa_ref[:, c:c + chunk] = proj(c, chunk).astype(BF16)
    ka_ref[...] = proj(o1, a_kv).astype(BF16)
    va_ref[...] = proj(o2, a_kv).astype(BF16)
    for c in range(0, b_w, chunk):
        kd_ref[:, c:c + chunk] = proj(o4 + c, chunk).astype(BF16)
    qscale = B_QK_DIM ** -0.5 * LOG2E
    for c in range(0, b_w, B_V_DIM):
        qdt_ref[c:c + B_V_DIM, :] = (proj(o3 + c, B_V_DIM) * qscale).T.astype(BF16)
        vdt_ref[c:c + B_V_DIM, :] = proj(o5 + c, B_V_DIM).T.astype(BF16)


def _inproj(x2, mod3, gain, w_bf, *, seq, a_q, a_kv, b_w, tm=512):
    t, d = x2.shape
    n = w_bf.shape[1]
    tiles_per_batch = seq // tm
    chunk = min(512, a_q, b_w)
    kern = functools.partial(_inproj_kernel, a_q=a_q, a_kv=a_kv, b_w=b_w, chunk=chunk)
    row = lambda width: pl.BlockSpec((tm, width), lambda i: (i, 0))
    col = lambda height: pl.BlockSpec((height, tm), lambda i: (0, i))
    return pl.pallas_call(
        kern,
        grid=(t // tm,),
        in_specs=[row(d),
                  pl.BlockSpec((1, N_MOD, d), lambda i: (i // tiles_per_batch, 0, 0)),
                  pl.BlockSpec((1, d), lambda i: (0, 0)),
                  _resident((d, n))],
        out_specs=[row(a_q), row(a_kv), row(a_kv), col(b_w), row(b_w), col(b_w)],
        out_shape=[jax.ShapeDtypeStruct((t, a_q), BF16),
                   jax.ShapeDtypeStruct((t, a_kv), BF16),
                   jax.ShapeDtypeStruct((t, a_kv), BF16),
                   jax.ShapeDtypeStruct((b_w, t), BF16),
                   jax.ShapeDtypeStruct((t, b_w), BF16),
                   jax.ShapeDtypeStruct((b_w, t), BF16)],
        scratch_shapes=[pltpu.VMEM((tm, d), BF16)],
        compiler_params=_cparams(("arbitrary",)),
        name="norm1_inproj",
    )(x2, mod3, gain, w_bf)


def _attn_a_kernel(slopes_ref, sink_ref, q_ref, k_ref, v_ref, g_ref, o_ref, acc_ref,
                   *, tq, kw, seq, heads):
    i = pl.program_id(1)
    q0 = i * tq
    kstart = jnp.clip(q0 - WINDOW, 0, seq - kw)
    kstart = pl.multiple_of(kstart, WINDOW)
    group = heads // A_KV_HEADS
    r = lax.broadcasted_iota(jnp.int32, (tq, kw), 0)
    c = lax.broadcasted_iota(jnp.int32, (tq, kw), 1)
    dist_i = jnp.abs((r - c) + (q0 - kstart))
    valid = dist_i <= WINDOW
    dist = dist_i.astype(F32)
    scale = HEAD_DIM ** -0.5
    for kvh in range(A_KV_HEADS):
        kwin = k_ref[pl.ds(kstart, kw), kvh * HEAD_DIM:(kvh + 1) * HEAD_DIM]
        vwin = v_ref[pl.ds(kstart, kw), kvh * HEAD_DIM:(kvh + 1) * HEAD_DIM]
        for gi in range(group):
            h = kvh * group + gi
            q = q_ref[:, h * HEAD_DIM:(h + 1) * HEAD_DIM]
            s = lax.dot_general(q, kwin, (((1,), (1,)), ((), ())),
                                preferred_element_type=F32) * scale
            s = jnp.where(valid, s - slopes_ref[h] * dist, NEG_INF)
            sink = sink_ref[h]
            m = jnp.maximum(jnp.max(s, axis=-1, keepdims=True), sink)
            p = jnp.exp(s - m)
            denom = jnp.sum(p, axis=-1, keepdims=True) + jnp.exp(sink - m)
            probs = (p / denom).astype(BF16)
            acc_ref[:, h * HEAD_DIM:(h + 1) * HEAD_DIM] = jnp.dot(
                probs, vwin, preferred_element_type=F32)
    o_ref[...] = _rms(acc_ref[...], g_ref[...]).astype(BF16)


def _attn_a(slopes_a, sink, qa, ka, va, gain, *, batch, seq, tq=128):
    t, a_q = qa.shape
    a_kv = ka.shape[1]
    heads = a_q // HEAD_DIM
    kw = tq + 2 * WINDOW
    nq = seq // tq
    kern = functools.partial(_attn_a_kernel, tq=tq, kw=kw, seq=seq, heads=heads)
    smem = pl.BlockSpec(memory_space=pltpu.SMEM)
    return pl.pallas_call(
        kern,
        grid=(batch, nq),
        in_specs=[smem, smem,
                  pl.BlockSpec((tq, a_q), lambda b, i: (b * nq + i, 0)),
                  pl.BlockSpec((seq, a_kv), lambda b, i: (b, 0)),
                  pl.BlockSpec((seq, a_kv), lambda b, i: (b, 0)),
                  pl.BlockSpec((1, a_q), lambda b, i: (0, 0))],
        out_specs=pl.BlockSpec((tq, a_q), lambda b, i: (b * nq + i, 0)),
        out_shape=jax.ShapeDtypeStruct((t, a_q), BF16),
        scratch_shapes=[pltpu.VMEM((tq, a_q), F32)],
        compiler_params=_cparams(("arbitrary", "arbitrary")),
        name="attn_window_gqa",
    )(slopes_a, sink, qa, ka, va, gain)


def _attn_b_kernel(slopes_ref, lq1_ref, lk1_ref, lq2_ref, lk2_ref, g_ref, d0_ref, feat_ref,
                   qt_ref, k_ref, vt_ref, o_ref,
                   w_ref, diag_ref, s0_ref, s1_ref, m_ref, l_ref, acc_ref, *, tile, sw, seq):
    h = pl.program_id(1)
    slope2 = slopes_ref[h] * LOG2E
    lam = (jnp.exp(jnp.sum(lq1_ref[...] * lk1_ref[...], keepdims=True))
           - jnp.exp(jnp.sum(lq2_ref[...] * lk2_ref[...], keepdims=True)) + LAM_INIT)
    halves = tile // sw
    n_tiles = seq // tile

    rho = lax.broadcasted_iota(jnp.int32, (N_BIAS_ROWS, sw), 0)
    lane = lax.broadcasted_iota(jnp.int32, (N_BIAS_ROWS, sw), 1).astype(F32)
    coeff = jnp.where(rho < 3, slope2,
                      jnp.where(rho < 6, 256.0 * slope2,
                                jnp.where(rho < 9, -slope2 * lane, 0.0)))
    hi = coeff.astype(BF16).astype(F32)
    mid = (coeff - hi).astype(BF16).astype(F32)
    lo = (coeff - hi - mid).astype(BF16).astype(F32)
    level = rho % 3
    rows = jnp.where(level == 0, hi, jnp.where(level == 1, mid, lo))
    w_ref[...] = jnp.zeros_like(w_ref)
    for strip in range(2 * halves):
        cols = slice(strip * sw, (strip + 1) * sw)
        w_ref[1, B_V_DIM:B_V_DIM + N_BIAS_ROWS, cols] = rows.astype(BF16)
        w_ref[2, B_V_DIM:B_V_DIM + N_BIAS_ROWS, cols] = (-rows).astype(BF16)
    for half in range(halves):
        diag_ref[half] = -slope2 * jnp.abs(d0_ref[...] - float(half * sw))

    def key_tile(i, t):
        return jnp.where(t == n_tiles - 1, i, jnp.where(t < i, t, t + 1))

    def scores(i, t, s_ref):
        j = key_tile(i, t)
        k0 = pl.multiple_of(j * tile, tile)
        widx = jnp.where(j == i, 0, jnp.where(j < i, 1, 2))
        lhs = jnp.concatenate([k_ref[pl.ds(k0, tile), :], feat_ref[...]], axis=1)
        s_ref[...] = jnp.dot(lhs, w_ref[widx], preferred_element_type=F32)

    def softmax_pv(i, t, s_ref, same_tile=False):
        j = key_tile(i, t)
        k0 = pl.multiple_of(j * tile, tile)
        q0 = i * tile
        p_strips = []
        alphas = []
        for half in range(halves):
            cst = slope2 * (k0 - q0 - half * sw).astype(F32)
            tc = 0.0 if same_tile else jnp.where(j < i, cst, -cst)
            for comp in range(2):
                cols = slice((2 * half + comp) * sw, (2 * half + comp + 1) * sw)
                t_sc = s_ref[:, cols]
                if same_tile:
                    t_sc = t_sc + diag_ref[half]
                m_old = m_ref[:, cols]
                m_new = jnp.maximum(m_old, jnp.max(t_sc, axis=0, keepdims=True) + tc)
                alpha = jnp.exp2(m_old - m_new)
                p = jnp.exp2(t_sc - (m_new - tc))
                l_ref[:, cols] = alpha * l_ref[:, cols] + jnp.sum(p, axis=0, keepdims=True)
                m_ref[:, cols] = m_new
                p_strips.append(p.astype(BF16))
                alphas.append(alpha)
        pv = jnp.dot(vt_ref[:, pl.ds(k0, tile)], jnp.concatenate(p_strips, axis=1),
                     preferred_element_type=F32)
        acc_ref[...] = jnp.concatenate(alphas, axis=1) * acc_ref[...] + pv

    def q_body(i, carry):
        q0 = pl.multiple_of(i * tile, tile)
        for half in range(halves):
            qcols = pl.ds(q0 + half * sw, sw)
            c0 = 2 * half * sw
            for variant in range(3):
                w_ref[variant, 0:B_QK_DIM, c0:c0 + sw] = qt_ref[0:B_QK_DIM, qcols]
                w_ref[variant, B_QK_DIM:B_V_DIM, c0 + sw:c0 + 2 * sw] = qt_ref[B_QK_DIM:, qcols]
        m_ref[...] = jnp.full_like(m_ref, NEG_INF)
        l_ref[...] = jnp.zeros_like(l_ref)
        acc_ref[...] = jnp.zeros_like(acc_ref)

        s_refs = (s0_ref, s1_ref)
        scores(i, 0, s0_ref)

        def group(n, c2):
            for u in range(GROUP):
                scores(i, GROUP * n + u + 1, s_refs[(u + 1) % 2])
                softmax_pv(i, GROUP * n + u, s_refs[u % 2])
            return c2

        n_groups = (n_tiles - 1) // GROUP
        lax.fori_loop(0, n_groups, group, 0)
        for t in range(n_groups * GROUP, n_tiles):
            if t + 1 < n_tiles:
                scores(i, t + 1, s_refs[(t + 1) % 2])
            softmax_pv(i, t, s_refs[t % 2], same_tile=(t == n_tiles - 1))

        o = acc_ref[...] / l_ref[...]
        for half in range(halves):
            c0 = 2 * half * sw
            od = (o[:, c0:c0 + sw] - lam * o[:, c0 + sw:c0 + 2 * sw]).T
            o_ref[pl.ds(q0 + half * sw, sw), :] = (
                _rms(od, g_ref[...]) * (1.0 - LAM_INIT)).astype(BF16)
        return carry

    lax.fori_loop(0, n_tiles, q_body, 0)


def _attn_b(slopes_b, lq1, lk1, lq2, lk2, gain, qdt, kd, vdt, *, batch, seq, tile=512, sw=256):
    b_w, t = qdt.shape
    heads = b_w // B_V_DIM
    tile = min(tile, seq // 2)
    assert seq % (2 * tile) == 0 and tile % sw == 0
    r = lax.broadcasted_iota(jnp.int32, (tile, sw), 0)
    c = lax.broadcasted_iota(jnp.int32, (tile, sw), 1)
    d0 = (r - c).astype(F32)
    rk = jnp.arange(tile, dtype=jnp.int32)[:, None]
    fcol = jnp.arange(B_V_DIM, dtype=jnp.int32)[None, :]
    feat = jnp.where(fcol < 3, rk % 256,
                     jnp.where(fcol < 6, rk // 256, jnp.where(fcol < 9, 1, 0))).astype(BF16)
    kern = functools.partial(_attn_b_kernel, tile=tile, sw=sw, seq=seq)
    smem = pl.BlockSpec(memory_space=pltpu.SMEM)
    vec = lambda n: pl.BlockSpec((1, n), lambda b, h: (0, 0))
    tposed = pl.BlockSpec((B_V_DIM, seq), lambda b, h: (h, b))
    natural = pl.BlockSpec((seq, B_V_DIM), lambda b, h: (b, h))
    return pl.pallas_call(
        kern,
        grid=(batch, heads),
        in_specs=[smem, vec(B_QK_DIM), vec(B_QK_DIM), vec(B_QK_DIM), vec(B_QK_DIM),
                  vec(B_V_DIM),
                  pl.BlockSpec((tile, sw), lambda b, h: (0, 0)),
                  pl.BlockSpec((tile, B_V_DIM), lambda b, h: (0, 0)),
                  tposed, natural, tposed],
        out_specs=natural,
        out_shape=jax.ShapeDtypeStruct((t, b_w), BF16),
        scratch_shapes=[pltpu.VMEM((3, 2 * B_V_DIM, 2 * tile), BF16),
                        pltpu.VMEM((tile // sw, tile, sw), F32),
                        pltpu.VMEM((tile, 2 * tile), F32),
                        pltpu.VMEM((tile, 2 * tile), F32),
                        pltpu.VMEM((1, 2 * tile), F32),
                        pltpu.VMEM((1, 2 * tile), F32),
                        pltpu.VMEM((B_V_DIM, 2 * tile), F32)],
        compiler_params=_cparams(("arbitrary", "arbitrary")),
        name="attn_diff",
    )(slopes_b, lq1, lk1, lq2, lk2, gain, d0, feat, qdt, kd, vdt)


def _outproj_kernel(a_ref, b_ref, w_ref, x_ref, mod_ref, g_ref, x1_ref, h2_ref, *, a_w):
    mix = (jnp.dot(a_ref[...], w_ref[0:a_w, :], preferred_element_type=F32)
           + jnp.dot(b_ref[...], w_ref[a_w:, :], preferred_element_type=F32))
    g1 = mod_ref[0, 2:3, :]
    sh2 = mod_ref[0, 3:4, :]
    sc2 = mod_ref[0, 4:5, :]
    x1 = x_ref[...] + g1 * mix
    x1_ref[...] = x1
    h2_ref[...] = (_rms(x1, g_ref[...]) * (1.0 + sc2) + sh2).astype(BF16)


def _outproj(out_a, out_b, w_bf, x2, mod3, gain, *, seq, tm=512):
    t, d = x2.shape
    a_w = out_a.shape[1]
    b_w = out_b.shape[1]
    tiles_per_batch = seq // tm
    row = lambda width: pl.BlockSpec((tm, width), lambda i: (i, 0))
    return pl.pallas_call(
        functools.partial(_outproj_kernel, a_w=a_w),
        grid=(t // tm,),
        in_specs=[row(a_w), row(b_w), _resident((a_w + b_w, d)), row(d),
                  pl.BlockSpec((1, N_MOD, d), lambda i: (i // tiles_per_batch, 0, 0)),
                  pl.BlockSpec((1, d), lambda i: (0, 0))],
        out_specs=[row(d), row(d)],
        out_shape=[jax.ShapeDtypeStruct((t, d), F32), jax.ShapeDtypeStruct((t, d), BF16)],
        compiler_params=_cparams(("arbitrary",)),
        name="outproj_norm2",
    )(out_a, out_b, w_bf, x2, mod3, gain)


def _ffn_kernel(h_ref, wg_ref, wu_ref, wd_ref, x1_ref, mod_ref, fg_ref, o_ref, acc_ref):
    f = pl.program_id(1)

    @pl.when(f == 0)
    def _():
        acc_ref[...] = jnp.zeros_like(acc_ref)

    h = h_ref[...]
    g = jnp.dot(h, wg_ref[...], preferred_element_type=F32)
    u = jnp.dot(h, wu_ref[...], preferred_element_type=F32)
    a = (g * jax.nn.sigmoid(g) * u).astype(BF16)
    acc_ref[...] += jnp.dot(a, wd_ref[...], preferred_element_type=F32)

    @pl.when(f == pl.num_programs(1) - 1)
    def _():
        g2 = mod_ref[0, 5:6, :]
        x2 = x1_ref[...] + g2 * acc_ref[...]
        o_ref[...] = _rms(x2, fg_ref[...])


def _ffn(h2, wg, wu, wd, x1, mod3, final_gain, *, seq, tm=512, tf=512):
    t, d = h2.shape
    ff = wg.shape[1]
    if ff % tf:
        tf = 256
    assert ff % tf == 0 and t % tm == 0 and seq % tm == 0
    tiles_per_batch = seq // tm
    return pl.pallas_call(
        _ffn_kernel,
        grid=(t // tm, ff // tf),
        in_specs=[pl.BlockSpec((tm, d), lambda i, f: (i, 0)),
                  pl.BlockSpec((d, tf), lambda i, f: (0, f)),
                  pl.BlockSpec((d, tf), lambda i, f: (0, f)),
                  pl.BlockSpec((tf, d), lambda i, f: (f, 0)),
                  pl.BlockSpec((tm, d), lambda i, f: (i, 0)),
                  pl.BlockSpec((1, N_MOD, d), lambda i, f: (i // tiles_per_batch, 0, 0)),
                  pl.BlockSpec((1, d), lambda i, f: (0, 0))],
        out_specs=pl.BlockSpec((tm, d), lambda i, f: (i, 0)),
        out_shape=jax.ShapeDtypeStruct((t, d), F32),
        scratch_shapes=[pltpu.VMEM((tm, d), F32)],
        compiler_params=_cparams(("arbitrary", "arbitrary")),
        name="swiglu_ffn_final_norm",
    )(h2, wg, wu, wd, x1, mod3, final_gain)


def kernel(x, c, w_ada, b_ada, norm1_gain, w_in, a_sink, a_out_gain, diff_lq1, diff_lk1,
           diff_lq2, diff_lk2, diff_subln_gain, w_o, norm2_gain, w_gate, w_up, w_down,
           final_gain):
    batch, seq, d = x.shape
    assert w_ada.shape[0] == 1, "single-layer block"
    a_w = d // 2
    b_w = d - a_w
    a_heads = a_w // HEAD_DIM
    b_heads = b_w // B_V_DIM
    a_kv = A_KV_HEADS * HEAD_DIM
    n_heads = a_heads + b_heads
    slopes = 2.0 ** (-8.0 * jnp.arange(1, n_heads + 1, dtype=F32) / n_heads)

    rows = 8
    c_pad = jnp.zeros((rows, d), F32).at[:batch].set(c)
    mod = _ada(c_pad, w_ada[0], b_ada[0][None, :])[:batch]
    mod3 = mod.reshape(batch, N_MOD, d)

    x2 = x.reshape(batch * seq, d)
    qa, ka, va, qdt, kd, vdt = _inproj(
        x2, mod3, norm1_gain[0][None, :], w_in[0].astype(BF16),
        seq=seq, a_q=a_w, a_kv=a_kv, b_w=b_w)

    out_a = _attn_a(slopes[:a_heads], a_sink[0].astype(F32), qa, ka, va,
                    a_out_gain[0][None, :], batch=batch, seq=seq)
    out_b = _attn_b(slopes[a_heads:], diff_lq1[0][None, :], diff_lk1[0][None, :],
                    diff_lq2[0][None, :], diff_lk2[0][None, :], diff_subln_gain[0][None, :],
                    qdt, kd, vdt, batch=batch, seq=seq)

    x1, h2 = _outproj(out_a, out_b, w_o[0].astype(BF16), x2, mod3, norm2_gain[0][None, :],
                      seq=seq)
    out = _ffn(h2, w_gate[0].astype(BF16), w_up[0].astype(BF16), w_down[0].astype(BF16),
               x1, mod3, final_gain[None, :], seq=seq)
    return out.reshape(batch, seq, d)
```

```python
import functools
import math

import jax
import jax.numpy as jnp
from jax import lax
from jax.experimental import pallas as pl
from jax.experimental.pallas import tpu as pltpu

HEAD_DIM = 128
A_KV_HEADS = 2
WINDOW = 128
B_QK_DIM = 64
B_V_DIM = 2 * B_QK_DIM
N_MOD = 6
EPS = 1e-6
NEG_INF = -1e30
LAM_INIT = 0.8 - 0.6 * math.exp(-0.3 * 0)
LOG2E = math.log2(math.e)
GROUP = 6
N_BIAS_ROWS = 16

V7X_VMEM_LIMIT_BYTES = 56 * 1024 * 1024

BF16 = jnp.bfloat16
F32 = jnp.float32


def _cparams(semantics):
    return pltpu.CompilerParams(dimension_semantics=semantics,
                                vmem_limit_bytes=V7X_VMEM_LIMIT_BYTES)


def _rms(x, gain):
    return x * lax.rsqrt(jnp.mean(x * x, axis=-1, keepdims=True) + EPS) * gain


def _resident(shape):
    return pl.BlockSpec(shape, lambda *_: (0,) * len(shape), pipeline_mode=pl.Buffered(1))


def _ada_kernel(c_ref, w_ref, b_ref, o_ref):
    c = c_ref[...]
    sc = (c * jax.nn.sigmoid(c)).astype(BF16)
    o_ref[...] = jnp.dot(sc, w_ref[...].astype(BF16), preferred_element_type=F32) + b_ref[...]


def _ada(c_pad, w, b, tn=1024):
    rows, d = c_pad.shape
    n = w.shape[1]
    return pl.pallas_call(
        _ada_kernel,
        grid=(n // tn,),
        in_specs=[pl.BlockSpec((rows, d), lambda j: (0, 0)),
                  pl.BlockSpec((d, tn), lambda j: (0, j)),
                  pl.BlockSpec((1, tn), lambda j: (0, j))],
        out_specs=pl.BlockSpec((rows, tn), lambda j: (0, j)),
        out_shape=jax.ShapeDtypeStruct((rows, n), F32),
        compiler_params=_cparams(("arbitrary",)),
        name="ada_mod",
    )(c_pad, w, b)


def _inproj_kernel(x_ref, mod_ref, g_ref, w_ref,
                   qa_ref, ka_ref, va_ref, qdt_ref, kd_ref, vdt_ref, h_ref,
                   *, a_q, a_kv, b_w, chunk):
    x = x_ref[...]
    sh1 = mod_ref[0, 0:1, :]
    sc1 = mod_ref[0, 1:2, :]
    h_ref[...] = (_rms(x, g_ref[...]) * (1.0 + sc1) + sh1).astype(BF16)

    def proj(c0, width):
        return jnp.dot(h_ref[...], w_ref[:, c0:c0 + width], preferred_element_type=F32)

    o1 = a_q
    o2 = o1 + a_kv
    o3 = o2 + a_kv
    o4 = o3 + b_w
    o5 = o4 + b_w
    for c in range(0, a_q, chunk):
        qa_ref[:, c:c + chunk] = proj(c, chunk).astype(BF16)
    ka_ref[...] = proj(o1, a_kv).astype(BF16)
    va_ref[...] = proj(o2, a_kv).astype(BF16)
    for c in range(0, b_w, chunk):
        kd_ref[:, c:c + chunk] = proj(o4 + c, chunk).astype(BF16)
    qscale = B_QK_DIM ** -0.5 * LOG2E
    for c in range(0, b_w, B_V_DIM):
        qdt_ref[c:c + B_V_DIM, :] = (proj(o3 + c, B_V_DIM) * qscale).T.astype(BF16)
        vdt_ref[c:c + B_V_DIM, :] = proj(o5 + c, B_V_DIM).T.astype(BF16)


def _inproj(x2, mod3, gain, w_bf, *, seq, a_q, a_kv, b_w, tm=512):
    t, d = x2.shape
    n = w_bf.shape[1]
    tiles_per_batch = seq // tm
    chunk = min(512, a_q, b_w)
    kern = functools.partial(_inproj_kernel, a_q=a_q, a_kv=a_kv, b_w=b_w, chunk=chunk)
    row = lambda width: pl.BlockSpec((tm, width), lambda i: (i, 0))
    col = lambda height: pl.BlockSpec((height, tm), lambda i: (0, i))
    return pl.pallas_call(
        kern,
        grid=(t // tm,),
        in_specs=[row(d),
                  pl.BlockSpec((1, N_MOD, d), lambda i: (i // tiles_per_batch, 0, 0)),
                  pl.BlockSpec((1, d), lambda i: (0, 0)),
                  _resident((d, n))],
        out_specs=[row(a_q), row(a_kv), row(a_kv), col(b_w), row(b_w), col(b_w)],
        out_shape=[jax.ShapeDtypeStruct((t, a_q), BF16),
                   jax.ShapeDtypeStruct((t, a_kv), BF16),
                   jax.ShapeDtypeStruct((t, a_kv), BF16),
                   jax.ShapeDtypeStruct((b_w, t), BF16),
                   jax.ShapeDtypeStruct((t, b_w), BF16),
                   jax.ShapeDtypeStruct((b_w, t), BF16)],
        scratch_shapes=[pltpu.VMEM((tm, d), BF16)],
        compiler_params=_cparams(("arbitrary",)),
        name="norm1_inproj",
    )(x2, mod3, gain, w_bf)


def _attn_a_kernel(slopes_ref, sink_ref, q_ref, k_ref, v_ref, g_ref, o_ref, bias_ref, acc_ref,
                   *, tq, kw, seq, heads):
    i = pl.program_id(1)
    q0 = i * tq
    kstart = pl.multiple_of(jnp.clip(q0 - WINDOW, 0, seq - kw), WINDOW)
    group = heads // A_KV_HEADS

    @pl.when((pl.program_id(0) == 0) & (i == 0))
    def _():
        r = lax.broadcasted_iota(jnp.int32, (tq, kw), 0)
        c = lax.broadcasted_iota(jnp.int32, (tq, kw), 1)
        for case in range(3):
            dist = jnp.abs((r - c) + case * WINDOW)
            for h in range(heads):
                bias_ref[case, h] = jnp.where(dist <= WINDOW,
                                              -slopes_ref[h] * dist.astype(F32), NEG_INF)

    case = (q0 - kstart) // WINDOW
    scale = HEAD_DIM ** -0.5
    kwins = [k_ref[pl.ds(kstart, kw), kvh * HEAD_DIM:(kvh + 1) * HEAD_DIM]
             for kvh in range(A_KV_HEADS)]
    vwins = [v_ref[pl.ds(kstart, kw), kvh * HEAD_DIM:(kvh + 1) * HEAD_DIM]
             for kvh in range(A_KV_HEADS)]
    scores = [lax.dot_general(q_ref[:, h * HEAD_DIM:(h + 1) * HEAD_DIM], kwins[h // group],
                              (((1,), (1,)), ((), ())), preferred_element_type=F32)
              for h in range(heads)]
    probs = []
    for h in range(heads):
        s = scores[h] * scale + bias_ref[case, h]
        sink = sink_ref[h]
        m = jnp.maximum(jnp.max(s, axis=-1, keepdims=True), sink)
        p = jnp.exp(s - m)
        denom = jnp.sum(p, axis=-1, keepdims=True) + jnp.exp(sink - m)
        probs.append((p / denom).astype(BF16))
    for h in range(heads):
        acc_ref[:, h * HEAD_DIM:(h + 1) * HEAD_DIM] = jnp.dot(
            probs[h], vwins[h // group], preferred_element_type=F32)
    o_ref[...] = _rms(acc_ref[...], g_ref[...]).astype(BF16)


def _attn_a(slopes_a, sink, qa, ka, va, gain, *, batch, seq, tq=128):
    t, a_q = qa.shape
    a_kv = ka.shape[1]
    heads = a_q // HEAD_DIM
    kw = tq + 2 * WINDOW
    nq = seq // tq
    kern = functools.partial(_attn_a_kernel, tq=tq, kw=kw, seq=seq, heads=heads)
    smem = pl.BlockSpec(memory_space=pltpu.SMEM)
    return pl.pallas_call(
        kern,
        grid=(batch, nq),
        in_specs=[smem, smem,
                  pl.BlockSpec((tq, a_q), lambda b, i: (b * nq + i, 0)),
                  pl.BlockSpec((seq, a_kv), lambda b, i: (b, 0)),
                  pl.BlockSpec((seq, a_kv), lambda b, i: (b, 0)),
                  pl.BlockSpec((1, a_q), lambda b, i: (0, 0))],
        out_specs=pl.BlockSpec((tq, a_q), lambda b, i: (b * nq + i, 0)),
        out_shape=jax.ShapeDtypeStruct((t, a_q), BF16),
        scratch_shapes=[pltpu.VMEM((3, heads, tq, kw), F32), pltpu.VMEM((tq, a_q), F32)],
        compiler_params=_cparams(("arbitrary", "arbitrary")),
        name="attn_window_gqa",
    )(slopes_a, sink, qa, ka, va, gain)


def _attn_b_kernel(slopes_ref, lq1_ref, lk1_ref, lq2_ref, lk2_ref, g_ref, d0_ref, feat_ref,
                   qt_ref, k_ref, vt_ref, o_ref,
                   w_ref, diag_ref, s0_ref, s1_ref, m_ref, l_ref, acc_ref, *, tile, sw, seq):
    h = pl.program_id(1)
    slope2 = slopes_ref[h] * LOG2E
    lam = (jnp.exp(jnp.sum(lq1_ref[...] * lk1_ref[...], keepdims=True))
           - jnp.exp(jnp.sum(lq2_ref[...] * lk2_ref[...], keepdims=True)) + LAM_INIT)
    halves = tile // sw
    n_tiles = seq // tile

    rho = lax.broadcasted_iota(jnp.int32, (N_BIAS_ROWS, sw), 0)
    lane = lax.broadcasted_iota(jnp.int32, (N_BIAS_ROWS, sw), 1).astype(F32)
    coeff = jnp.where(rho < 3, slope2,
                      jnp.where(rho < 6, 256.0 * slope2,
                                jnp.where(rho < 9, -slope2 * lane, 0.0)))
    hi = coeff.astype(BF16).astype(F32)
    mid = (coeff - hi).astype(BF16).astype(F32)
    lo = (coeff - hi - mid).astype(BF16).astype(F32)
    level = rho % 3
    rows = jnp.where(level == 0, hi, jnp.where(level == 1, mid, lo))
    w_ref[...] = jnp.zeros_like(w_ref)
    for strip in range(2 * halves):
        cols = slice(strip * sw, (strip + 1) * sw)
        w_ref[1, B_V_DIM:B_V_DIM + N_BIAS_ROWS, cols] = rows.astype(BF16)
        w_ref[2, B_V_DIM:B_V_DIM + N_BIAS_ROWS, cols] = (-rows).astype(BF16)
    for half in range(halves):
        diag_ref[half] = -slope2 * jnp.abs(d0_ref[...] - float(half * sw))

    def key_tile(i, t):
        return jnp.where(t == n_tiles - 1, i, jnp.where(t < i, t, t + 1))

    def scores(i, t, s_ref):
        j = key_tile(i, t)
        k0 = pl.multiple_of(j * tile, tile)
        widx = jnp.where(j == i, 0, jnp.where(j < i, 1, 2))
        lhs = jnp.concatenate([k_ref[pl.ds(k0, tile), :], feat_ref[...]], axis=1)
        s_ref[...] = jnp.dot(lhs, w_ref[widx], preferred_element_type=F32)

    def softmax_pv(i, t, s_ref, same_tile=False):
        j = key_tile(i, t)
        k0 = pl.multiple_of(j * tile, tile)
        q0 = i * tile
        p_strips = []
        alphas = []
        for half in range(halves):
            cst = slope2 * (k0 - q0 - half * sw).astype(F32)
            tc = 0.0 if same_tile else jnp.where(j < i, cst, -cst)
            for comp in range(2):
                cols = slice((2 * half + comp) * sw, (2 * half + comp + 1) * sw)
                t_sc = s_ref[:, cols]
                if same_tile:
                    t_sc = t_sc + diag_ref[half]
                m_old = m_ref[:, cols]
                m_new = jnp.maximum(m_old, jnp.max(t_sc, axis=0, keepdims=True) + tc)
                alpha = jnp.exp2(m_old - m_new)
                p = jnp.exp2(t_sc - (m_new - tc))
                l_ref[:, cols] = alpha * l_ref[:, cols] + jnp.sum(p, axis=0, keepdims=True)
                m_ref[:, cols] = m_new
                p_strips.append(p.astype(BF16))
                alphas.append(alpha)
        pv = jnp.dot(vt_ref[:, pl.ds(k0, tile)], jnp.concatenate(p_strips, axis=1),
                     preferred_element_type=F32)
        acc_ref[...] = jnp.concatenate(alphas, axis=1) * acc_ref[...] + pv

    def q_body(i, carry):
        q0 = pl.multiple_of(i * tile, tile)
        for half in range(halves):
            qcols = pl.ds(q0 + half * sw, sw)
            c0 = 2 * half * sw
            for variant in range(3):
                w_ref[variant, 0:B_QK_DIM, c0:c0 + sw] = qt_ref[0:B_QK_DIM, qcols]
                w_ref[variant, B_QK_DIM:B_V_DIM, c0 + sw:c0 + 2 * sw] = qt_ref[B_QK_DIM:, qcols]
        m_ref[...] = jnp.full_like(m_ref, NEG_INF)
        l_ref[...] = jnp.zeros_like(l_ref)
        acc_ref[...] = jnp.zeros_like(acc_ref)

        s_refs = (s0_ref, s1_ref)
        scores(i, 0, s0_ref)

        def group(n, c2):
            for u in range(GROUP):
                scores(i, GROUP * n + u + 1, s_refs[(u + 1) % 2])
                softmax_pv(i, GROUP * n + u, s_refs[u % 2])
            return c2

        n_groups = (n_tiles - 1) // GROUP
        lax.fori_loop(0, n_groups, group, 0)
        for t in range(n_groups * GROUP, n_tiles):
            if t + 1 < n_tiles:
                scores(i, t + 1, s_refs[(t + 1) % 2])
            softmax_pv(i, t, s_refs[t % 2], same_tile=(t == n_tiles - 1))

        o = acc_ref[...] / l_ref[...]
        for half in range(halves):
            c0 = 2 * half * sw
            od = (o[:, c0:c0 + sw] - lam * o[:, c0 + sw:c0 + 2 * sw]).T
            o_ref[pl.ds(q0 + half * sw, sw), :] = (
                _rms(od, g_ref[...]) * (1.0 - LAM_INIT)).astype(BF16)
        return carry

    lax.fori_loop(0, n_tiles, q_body, 0)


def _attn_b(slopes_b, lq1, lk1, lq2, lk2, gain, qdt, kd, vdt, *, batch, seq, tile=512, sw=256):
    b_w, t = qdt.shape
    heads = b_w // B_V_DIM
    tile = min(tile, seq // 2)
    assert seq % (2 * tile) == 0 and tile % sw == 0
    r = lax.broadcasted_iota(jnp.int32, (tile, sw), 0)
    c = lax.broadcasted_iota(jnp.int32, (tile, sw), 1)
    d0 = (r - c).astype(F32)
    rk = jnp.arange(tile, dtype=jnp.int32)[:, None]
    fcol = jnp.arange(B_V_DIM, dtype=jnp.int32)[None, :]
    feat = jnp.where(fcol < 3, rk % 256,
                     jnp.where(fcol < 6, rk // 256, jnp.where(fcol < 9, 1, 0))).astype(BF16)
    kern = functools.partial(_attn_b_kernel, tile=tile, sw=sw, seq=seq)
    smem = pl.BlockSpec(memory_space=pltpu.SMEM)
    vec = lambda n: pl.BlockSpec((1, n), lambda b, h: (0, 0))
    tposed = pl.BlockSpec((B_V_DIM, seq), lambda b, h: (h, b))
    natural = pl.BlockSpec((seq, B_V_DIM), lambda b, h: (b, h))
    return pl.pallas_call(
        kern,
        grid=(batch, heads),
        in_specs=[smem, vec(B_QK_DIM), vec(B_QK_DIM), vec(B_QK_DIM), vec(B_QK_DIM),
                  vec(B_V_DIM),
                  pl.BlockSpec((tile, sw), lambda b, h: (0, 0)),
                  pl.BlockSpec((tile, B_V_DIM), lambda b, h: (0, 0)),
                  tposed, natural, tposed],
        out_specs=natural,
        out_shape=jax.ShapeDtypeStruct((t, b_w), BF16),
        scratch_shapes=[pltpu.VMEM((3, 2 * B_V_DIM, 2 * tile), BF16),
                        pltpu.VMEM((tile // sw, tile, sw), F32),
                        pltpu.VMEM((tile, 2 * tile), F32),
                        pltpu.VMEM((tile, 2 * tile), F32),
                        pltpu.VMEM((1, 2 * tile), F32),
                        pltpu.VMEM((1, 2 * tile), F32),
                        pltpu.VMEM((B_V_DIM, 2 * tile), F32)],
        compiler_params=_cparams(("arbitrary", "arbitrary")),
        name="attn_diff",
    )(slopes_b, lq1, lk1, lq2, lk2, gain, d0, feat, qdt, kd, vdt)


def _outproj_kernel(a_ref, b_ref, w_ref, x_ref, mod_ref, g_ref, x1_ref, h2_ref, *, a_w):
    mix = (jnp.dot(a_ref[...], w_ref[0:a_w, :], preferred_element_type=F32)
           + jnp.dot(b_ref[...], w_ref[a_w:, :], preferred_element_type=F32))
    g1 = mod_ref[0, 2:3, :]
    sh2 = mod_ref[0, 3:4, :]
    sc2 = mod_ref[0, 4:5, :]
    x1 = x_ref[...] + g1 * mix
    x1_ref[...] = x1
    h2_ref[...] = (_rms(x1, g_ref[...]) * (1.0 + sc2) + sh2).astype(BF16)


def _outproj(out_a, out_b, w_bf, x2, mod3, gain, *, seq, tm=512):
    t, d = x2.shape
    a_w = out_a.shape[1]
    b_w = out_b.shape[1]
    tiles_per_batch = seq // tm
    row = lambda width: pl.BlockSpec((tm, width), lambda i: (i, 0))
    return pl.pallas_call(
        functools.partial(_outproj_kernel, a_w=a_w),
        grid=(t // tm,),
        in_specs=[row(a_w), row(b_w), _resident((a_w + b_w, d)), row(d),
                  pl.BlockSpec((1, N_MOD, d), lambda i: (i // tiles_per_batch, 0, 0)),
                  pl.BlockSpec((1, d), lambda i: (0, 0))],
        out_specs=[row(d), row(d)],
        out_shape=[jax.ShapeDtypeStruct((t, d), F32), jax.ShapeDtypeStruct((t, d), BF16)],
        compiler_params=_cparams(("arbitrary",)),
        name="outproj_norm2",
    )(out_a, out_b, w_bf, x2, mod3, gain)


def _ffn_kernel(h_ref, wg_ref, wu_ref, wd_ref, x1_ref, mod_ref, fg_ref, o_ref):
    f = pl.program_id(1)

    @pl.when(f == 0)
    def _():
        o_ref[...] = jnp.zeros_like(o_ref)

    h = h_ref[...]
    g = jnp.dot(h, wg_ref[...], preferred_element_type=F32)
    u = jnp.dot(h, wu_ref[...], preferred_element_type=F32)
    a = (g * jax.nn.sigmoid(g) * u).astype(BF16)
    o_ref[...] += jnp.dot(a, wd_ref[...], preferred_element_type=F32)

    @pl.when(f == pl.num_programs(1) - 1)
    def _():
        g2 = mod_ref[0, 5:6, :]
        o_ref[...] = _rms(x1_ref[...] + g2 * o_ref[...], fg_ref[...])


def _ffn(h2, wg, wu, wd, x1, mod3, final_gain, *, seq, tm=512, tf=512):
    t, d = h2.shape
    ff = wg.shape[1]
    tm = min(tm, seq)
    if ff % tf:
        tf = 256
    assert ff % tf == 0 and t % tm == 0 and seq % tm == 0
    tiles_per_batch = seq // tm
    return pl.pallas_call(
        _ffn_kernel,
        grid=(t // tm, ff // tf),
        in_specs=[pl.BlockSpec((tm, d), lambda i, f: (i, 0)),
                  pl.BlockSpec((d, tf), lambda i, f: (0, f)),
                  pl.BlockSpec((d, tf), lambda i, f: (0, f)),
                  pl.BlockSpec((tf, d), lambda i, f: (f, 0)),
                  pl.BlockSpec((tm, d), lambda i, f: (i, 0)),
                  pl.BlockSpec((1, N_MOD, d), lambda i, f: (i // tiles_per_batch, 0, 0)),
                  pl.BlockSpec((1, d), lambda i, f: (0, 0))],
        out_specs=pl.BlockSpec((tm, d), lambda i, f: (i, 0)),
        out_shape=jax.ShapeDtypeStruct((t, d), F32),
        compiler_params=_cparams(("arbitrary", "arbitrary")),
        name="swiglu_ffn_final_norm",
    )(h2, wg, wu, wd, x1, mod3, final_gain)


def kernel(x, c, w_ada, b_ada, norm1_gain, w_in, a_sink, a_out_gain, diff_lq1, diff_lk1,
           diff_lq2, diff_lk2, diff_subln_gain, w_o, norm2_gain, w_gate, w_up, w_down,
           final_gain):
    batch, seq, d = x.shape
    assert w_ada.shape[0] == 1, "single-layer block"
    a_w = d // 2
    b_w = d - a_w
    a_heads = a_w // HEAD_DIM
    b_heads = b_w // B_V_DIM
    a_kv = A_KV_HEADS * HEAD_DIM
    n_heads = a_heads + b_heads
    slopes = 2.0 ** (-8.0 * jnp.arange(1, n_heads + 1, dtype=F32) / n_heads)

    rows = 8
    c_pad = jnp.zeros((rows, d), F32).at[:batch].set(c)
    mod = _ada(c_pad, w_ada[0], b_ada[0][None, :])[:batch]
    mod3 = mod.reshape(batch, N_MOD, d)

    x2 = x.reshape(batch * seq, d)
    qa, ka, va, qdt, kd, vdt = _inproj(
        x2, mod3, norm1_gain[0][None, :], w_in[0].astype(BF16),
        seq=seq, a_q=a_w, a_kv=a_kv, b_w=b_w)

    out_a = _attn_a(slopes[:a_heads], a_sink[0].astype(F32), qa, ka, va,
                    a_out_gain[0][None, :], batch=batch, seq=seq)
    out_b = _attn_b(slopes[a_heads:], diff_lq1[0][None, :], diff_lk1[0][None, :],
                    diff_lq2[0][None, :], diff_lk2[0][None, :], diff_subln_gain[0][None, :],
                    qdt, kd, vdt, batch=batch, seq=seq)

    x1, h2 = _outproj(out_a, out_b, w_o[0].astype(BF16), x2, mod3, norm2_gain[0][None, :],
                      seq=seq)
    out = _ffn(h2, w_gate[0].astype(BF16), w_up[0].astype(BF16), w_down[0].astype(BF16),
               x1, mod3, final_gain[None, :], seq=seq)
    return out.reshape(batch, seq, d)
```

```python
import functools
import math

import jax
import jax.numpy as jnp
from jax import lax
from jax.experimental import pallas as pl
from jax.experimental.pallas import tpu as pltpu

HEAD_DIM = 128
A_KV_HEADS = 2
WINDOW = 128
B_QK_DIM = 64
B_V_DIM = 2 * B_QK_DIM
N_MOD = 6
EPS = 1e-6
NEG_INF = -1e30
LAM_INIT = 0.8 - 0.6 * math.exp(-0.3 * 0)
LOG2E = math.log2(math.e)
GROUP = 6
N_BIAS_ROWS = 16

V7X_VMEM_LIMIT_BYTES = 56 * 1024 * 1024

BF16 = jnp.bfloat16
F32 = jnp.float32


def _cparams(semantics):
    return pltpu.CompilerParams(dimension_semantics=semantics,
                                vmem_limit_bytes=V7X_VMEM_LIMIT_BYTES)


def _rms(x, gain):
    return x * lax.rsqrt(jnp.mean(x * x, axis=-1, keepdims=True) + EPS) * gain


def _resident(shape):
    return pl.BlockSpec(shape, lambda *_: (0,) * len(shape), pipeline_mode=pl.Buffered(1))


def _ada_kernel(c_ref, w_ref, b_ref, o_ref):
    c = c_ref[...]
    sc = (c * jax.nn.sigmoid(c)).astype(BF16)
    o_ref[...] = jnp.dot(sc, w_ref[...].astype(BF16), preferred_element_type=F32) + b_ref[...]


def _ada(c_pad, w, b, tn=1024):
    rows, d = c_pad.shape
    n = w.shape[1]
    return pl.pallas_call(
        _ada_kernel,
        grid=(n // tn,),
        in_specs=[pl.BlockSpec((rows, d), lambda j: (0, 0)),
                  pl.BlockSpec((d, tn), lambda j: (0, j)),
                  pl.BlockSpec((1, tn), lambda j: (0, j))],
        out_specs=pl.BlockSpec((rows, tn), lambda j: (0, j)),
        out_shape=jax.ShapeDtypeStruct((rows, n), F32),
        compiler_params=_cparams(("arbitrary",)),
        name="ada_mod",
    )(c_pad, w, b)


def _inproj_kernel(x_ref, mod_ref, g_ref, w_ref,
                   qa_ref, ka_ref, va_ref, qdt_ref, kd_ref, vdt_ref, h_ref,
                   *, a_q, a_kv, b_w, chunk):
    x = x_ref[...]
    sh1 = mod_ref[0, 0:1, :]
    sc1 = mod_ref[0, 1:2, :]
    h_ref[...] = (_rms(x, g_ref[...]) * (1.0 + sc1) + sh1).astype(BF16)

    def proj(c0, width):
        return jnp.dot(h_ref[...], w_ref[:, c0:c0 + width], preferred_element_type=F32)

    o1 = a_q
    o2 = o1 + a_kv
    o3 = o2 + a_kv
    o4 = o3 + b_w
    o5 = o4 + b_w
    for c in range(0, a_q, chunk):
        qa_ref[:, c:c + chunk] = proj(c, chunk).astype(BF16)
    ka_ref[...] = proj(o1, a_kv).astype(BF16)
    va_ref[...] = proj(o2, a_kv).astype(BF16)
    for c in range(0, b_w, chunk):
        kd_ref[:, c:c + chunk] = proj(o4 + c, chunk).astype(BF16)
    qscale = B_QK_DIM ** -0.5 * LOG2E
    for c in range(0, b_w, chunk):
        q = proj(o3 + c, chunk) * qscale
        v = proj(o5 + c, chunk)
        for hc in range(0, chunk, B_V_DIM):
            qdt_ref[c + hc:c + hc + B_V_DIM, :] = q[:, hc:hc + B_V_DIM].T.astype(BF16)
            vdt_ref[c + hc:c + hc + B_V_DIM, :] = v[:, hc:hc + B_V_DIM].T.astype(BF16)


def _inproj(x2, mod3, gain, w_bf, *, seq, a_q, a_kv, b_w, tm=512):
    t, d = x2.shape
    n = w_bf.shape[1]
    tiles_per_batch = seq // tm
    chunk = min(512, a_q, b_w)
    kern = functools.partial(_inproj_kernel, a_q=a_q, a_kv=a_kv, b_w=b_w, chunk=chunk)
    row = lambda width: pl.BlockSpec((tm, width), lambda i: (i, 0))
    col = lambda height: pl.BlockSpec((height, tm), lambda i: (0, i))
    return pl.pallas_call(
        kern,
        grid=(t // tm,),
        in_specs=[row(d),
                  pl.BlockSpec((1, N_MOD, d), lambda i: (i // tiles_per_batch, 0, 0)),
                  pl.BlockSpec((1, d), lambda i: (0, 0)),
                  _resident((d, n))],
        out_specs=[row(a_q), row(a_kv), row(a_kv), col(b_w), row(b_w), col(b_w)],
        out_shape=[jax.ShapeDtypeStruct((t, a_q), BF16),
                   jax.ShapeDtypeStruct((t, a_kv), BF16),
                   jax.ShapeDtypeStruct((t, a_kv), BF16),
                   jax.ShapeDtypeStruct((b_w, t), BF16),
                   jax.ShapeDtypeStruct((t, b_w), BF16),
                   jax.ShapeDtypeStruct((b_w, t), BF16)],
        scratch_shapes=[pltpu.VMEM((tm, d), BF16)],
        compiler_params=_cparams(("arbitrary",)),
        name="norm1_inproj",
    )(x2, mod3, gain, w_bf)


def _attn_a_kernel(slopes_ref, sink_ref, q_ref, k_ref, v_ref, g_ref, o_ref, bias_ref, acc_ref,
                   *, tq, kw, seq, heads):
    i = pl.program_id(1)
    q0 = i * tq
    kstart = pl.multiple_of(jnp.clip(q0 - WINDOW, 0, seq - kw), WINDOW)
    group = heads // A_KV_HEADS

    @pl.when((pl.program_id(0) == 0) & (i == 0))
    def _():
        r = lax.broadcasted_iota(jnp.int32, (tq, kw), 0)
        c = lax.broadcasted_iota(jnp.int32, (tq, kw), 1)
        for case in range(3):
            dist = jnp.abs((r - c) + case * WINDOW)
            for h in range(heads):
                bias_ref[case, h] = jnp.where(dist <= WINDOW,
                                              -slopes_ref[h] * dist.astype(F32), NEG_INF)

    case = (q0 - kstart) // WINDOW
    scale = HEAD_DIM ** -0.5
    kwins = [k_ref[pl.ds(kstart, kw), kvh * HEAD_DIM:(kvh + 1) * HEAD_DIM]
             for kvh in range(A_KV_HEADS)]
    vwins = [v_ref[pl.ds(kstart, kw), kvh * HEAD_DIM:(kvh + 1) * HEAD_DIM]
             for kvh in range(A_KV_HEADS)]
    scores = [lax.dot_general(q_ref[:, h * HEAD_DIM:(h + 1) * HEAD_DIM], kwins[h // group],
                              (((1,), (1,)), ((), ())), preferred_element_type=F32)
              for h in range(heads)]
    probs = []
    for h in range(heads):
        s = scores[h] * scale + bias_ref[case, h]
        sink = sink_ref[h]
        m = jnp.maximum(jnp.max(s, axis=-1, keepdims=True), sink)
        p = jnp.exp(s - m)
        denom = jnp.sum(p, axis=-1, keepdims=True) + jnp.exp(sink - m)
        probs.append((p / denom).astype(BF16))
    for h in range(heads):
        acc_ref[:, h * HEAD_DIM:(h + 1) * HEAD_DIM] = jnp.dot(
            probs[h], vwins[h // group], preferred_element_type=F32)
    o_ref[...] = _rms(acc_ref[...], g_ref[...]).astype(BF16)


def _attn_a(slopes_a, sink, qa, ka, va, gain, *, batch, seq, tq=128):
    t, a_q = qa.shape
    a_kv = ka.shape[1]
    heads = a_q // HEAD_DIM
    kw = tq + 2 * WINDOW
    nq = seq // tq
    kern = functools.partial(_attn_a_kernel, tq=tq, kw=kw, seq=seq, heads=heads)
    smem = pl.BlockSpec(memory_space=pltpu.SMEM)
    return pl.pallas_call(
        kern,
        grid=(batch, nq),
        in_specs=[smem, smem,
                  pl.BlockSpec((tq, a_q), lambda b, i: (b * nq + i, 0)),
                  pl.BlockSpec((seq, a_kv), lambda b, i: (b, 0)),
                  pl.BlockSpec((seq, a_kv), lambda b, i: (b, 0)),
                  pl.BlockSpec((1, a_q), lambda b, i: (0, 0))],
        out_specs=pl.BlockSpec((tq, a_q), lambda b, i: (b * nq + i, 0)),
        out_shape=jax.ShapeDtypeStruct((t, a_q), BF16),
        scratch_shapes=[pltpu.VMEM((3, heads, tq, kw), F32), pltpu.VMEM((tq, a_q), F32)],
        compiler_params=_cparams(("arbitrary", "arbitrary")),
        name="attn_window_gqa",
    )(slopes_a, sink, qa, ka, va, gain)


def _attn_b_kernel(slopes_ref, lq1_ref, lk1_ref, lq2_ref, lk2_ref, g_ref, d0_ref, feat_ref,
                   qt_ref, k_ref, vt_ref, o_ref,
                   w_ref, diag_ref, s0_ref, s1_ref, mx0_ref, mx1_ref, m_ref, l_ref, acc_ref,
                   *, tile, sw, seq):
    h = pl.program_id(1)
    slope2 = slopes_ref[h] * LOG2E
    lam = (jnp.exp(jnp.sum(lq1_ref[...] * lk1_ref[...], keepdims=True))
           - jnp.exp(jnp.sum(lq2_ref[...] * lk2_ref[...], keepdims=True)) + LAM_INIT)
    halves = tile // sw
    n_tiles = seq // tile

    rho = lax.broadcasted_iota(jnp.int32, (N_BIAS_ROWS, sw), 0)
    lane = lax.broadcasted_iota(jnp.int32, (N_BIAS_ROWS, sw), 1).astype(F32)
    coeff = jnp.where(rho < 3, slope2,
                      jnp.where(rho < 6, 256.0 * slope2,
                                jnp.where(rho < 9, -slope2 * lane, 0.0)))
    hi = coeff.astype(BF16).astype(F32)
    mid = (coeff - hi).astype(BF16).astype(F32)
    lo = (coeff - hi - mid).astype(BF16).astype(F32)
    level = rho % 3
    rows = jnp.where(level == 0, hi, jnp.where(level == 1, mid, lo))
    w_ref[...] = jnp.zeros_like(w_ref)
    for strip in range(2 * halves):
        cols = slice(strip * sw, (strip + 1) * sw)
        w_ref[1, B_V_DIM:B_V_DIM + N_BIAS_ROWS, cols] = rows.astype(BF16)
        w_ref[2, B_V_DIM:B_V_DIM + N_BIAS_ROWS, cols] = (-rows).astype(BF16)
    for half in range(halves):
        diag_ref[half] = -slope2 * jnp.abs(d0_ref[...] - float(half * sw))

    def key_tile(i, t):
        return jnp.where(t == n_tiles - 1, i, jnp.where(t < i, t, t + 1))

    def scores(i, t, s_ref, mx_ref):
        j = key_tile(i, t)
        k0 = pl.multiple_of(j * tile, tile)
        widx = jnp.where(j == i, 0, jnp.where(j < i, 1, 2))
        lhs = jnp.concatenate([k_ref[pl.ds(k0, tile), :], feat_ref[...]], axis=1)
        s = jnp.dot(lhs, w_ref[widx], preferred_element_type=F32)
        s_ref[...] = s
        mx_ref[...] = jnp.max(s, axis=0, keepdims=True)

    def softmax_pv(i, t, s_ref, mx_ref, same_tile=False):
        j = key_tile(i, t)
        k0 = pl.multiple_of(j * tile, tile)
        q0 = i * tile
        p_strips = []
        alphas = []
        for half in range(halves):
            cst = slope2 * (k0 - q0 - half * sw).astype(F32)
            tc = 0.0 if same_tile else jnp.where(j < i, cst, -cst)
            for comp in range(2):
                cols = slice((2 * half + comp) * sw, (2 * half + comp + 1) * sw)
                t_sc = s_ref[:, cols]
                if same_tile:
                    t_sc = t_sc + diag_ref[half]
                    t_max = jnp.max(t_sc, axis=0, keepdims=True)
                else:
                    t_max = mx_ref[:, cols]
                m_old = m_ref[:, cols]
                m_new = jnp.maximum(m_old, t_max + tc)
                alpha = jnp.exp2(m_old - m_new)
                p = jnp.exp2(t_sc - (m_new - tc))
                l_ref[:, cols] = alpha * l_ref[:, cols] + jnp.sum(p, axis=0, keepdims=True)
                m_ref[:, cols] = m_new
                p_strips.append(p.astype(BF16))
                alphas.append(alpha)
        pv = jnp.dot(vt_ref[:, pl.ds(k0, tile)], jnp.concatenate(p_strips, axis=1),
                     preferred_element_type=F32)
        acc_ref[...] = jnp.concatenate(alphas, axis=1) * acc_ref[...] + pv

    def q_body(i, carry):
        q0 = pl.multiple_of(i * tile, tile)
        for half in range(halves):
            qcols = pl.ds(q0 + half * sw, sw)
            c0 = 2 * half * sw
            for variant in range(3):
                w_ref[variant, 0:B_QK_DIM, c0:c0 + sw] = qt_ref[0:B_QK_DIM, qcols]
                w_ref[variant, B_QK_DIM:B_V_DIM, c0 + sw:c0 + 2 * sw] = qt_ref[B_QK_DIM:, qcols]
        m_ref[...] = jnp.full_like(m_ref, NEG_INF)
        l_ref[...] = jnp.zeros_like(l_ref)
        acc_ref[...] = jnp.zeros_like(acc_ref)

        s_refs = ((s0_ref, mx0_ref), (s1_ref, mx1_ref))
        scores(i, 0, *s_refs[0])

        def group(n, c2):
            for u in range(GROUP):
                scores(i, GROUP * n + u + 1, *s_refs[(u + 1) % 2])
                softmax_pv(i, GROUP * n + u, *s_refs[u % 2])
            return c2

        n_groups = (n_tiles - 1) // GROUP
        lax.fori_loop(0, n_groups, group, 0)
        for t in range(n_groups * GROUP, n_tiles):
            if t + 1 < n_tiles:
                scores(i, t + 1, *s_refs[(t + 1) % 2])
            softmax_pv(i, t, *s_refs[t % 2], same_tile=(t == n_tiles - 1))

        o = acc_ref[...] / l_ref[...]
        for half in range(halves):
            c0 = 2 * half * sw
            od = (o[:, c0:c0 + sw] - lam * o[:, c0 + sw:c0 + 2 * sw]).T
            o_ref[pl.ds(q0 + half * sw, sw), :] = (
                _rms(od, g_ref[...]) * (1.0 - LAM_INIT)).astype(BF16)
        return carry

    lax.fori_loop(0, n_tiles, q_body, 0)


def _attn_b(slopes_b, lq1, lk1, lq2, lk2, gain, qdt, kd, vdt, *, batch, seq, tile=512, sw=256):
    b_w, t = qdt.shape
    heads = b_w // B_V_DIM
    tile = min(tile, seq // 2)
    assert seq % (2 * tile) == 0 and tile % sw == 0
    r = lax.broadcasted_iota(jnp.int32, (tile, sw), 0)
    c = lax.broadcasted_iota(jnp.int32, (tile, sw), 1)
    d0 = (r - c).astype(F32)
    rk = jnp.arange(tile, dtype=jnp.int32)[:, None]
    fcol = jnp.arange(B_V_DIM, dtype=jnp.int32)[None, :]
    feat = jnp.where(fcol < 3, rk % 256,
                     jnp.where(fcol < 6, rk // 256, jnp.where(fcol < 9, 1, 0))).astype(BF16)
    kern = functools.partial(_attn_b_kernel, tile=tile, sw=sw, seq=seq)
    smem = pl.BlockSpec(memory_space=pltpu.SMEM)
    vec = lambda n: pl.BlockSpec((1, n), lambda b, h: (0, 0))
    tposed = pl.BlockSpec((B_V_DIM, seq), lambda b, h: (h, b))
    natural = pl.BlockSpec((seq, B_V_DIM), lambda b, h: (b, h))
    return pl.pallas_call(
        kern,
        grid=(batch, heads),
        in_specs=[smem, vec(B_QK_DIM), vec(B_QK_DIM), vec(B_QK_DIM), vec(B_QK_DIM),
                  vec(B_V_DIM),
                  pl.BlockSpec((tile, sw), lambda b, h: (0, 0)),
                  pl.BlockSpec((tile, B_V_DIM), lambda b, h: (0, 0)),
                  tposed, natural, tposed],
        out_specs=natural,
        out_shape=jax.ShapeDtypeStruct((t, b_w), BF16),
        scratch_shapes=[pltpu.VMEM((3, 2 * B_V_DIM, 2 * tile), BF16),
                        pltpu.VMEM((tile // sw, tile, sw), F32),
                        pltpu.VMEM((tile, 2 * tile), F32),
                        pltpu.VMEM((tile, 2 * tile), F32),
                        pltpu.VMEM((1, 2 * tile), F32),
                        pltpu.VMEM((1, 2 * tile), F32),
                        pltpu.VMEM((1, 2 * tile), F32),
                        pltpu.VMEM((1, 2 * tile), F32),
                        pltpu.VMEM((B_V_DIM, 2 * tile), F32)],
        compiler_params=_cparams(("arbitrary", "arbitrary")),
        name="attn_diff",
    )(slopes_b, lq1, lk1, lq2, lk2, gain, d0, feat, qdt, kd, vdt)


def _outproj_kernel(a_ref, b_ref, w_ref, x_ref, mod_ref, g_ref, x1_ref, h2_ref, *, a_w):
    mix = (jnp.dot(a_ref[...], w_ref[0:a_w, :], preferred_element_type=F32)
           + jnp.dot(b_ref[...], w_ref[a_w:, :], preferred_element_type=F32))
    g1 = mod_ref[0, 2:3, :]
    sh2 = mod_ref[0, 3:4, :]
    sc2 = mod_ref[0, 4:5, :]
    x1 = x_ref[...] + g1 * mix
    x1_ref[...] = x1
    h2_ref[...] = (_rms(x1, g_ref[...]) * (1.0 + sc2) + sh2).astype(BF16)


def _outproj(out_a, out_b, w_bf, x2, mod3, gain, *, seq, tm=512):
    t, d = x2.shape
    a_w = out_a.shape[1]
    b_w = out_b.shape[1]
    tiles_per_batch = seq // tm
    row = lambda width: pl.BlockSpec((tm, width), lambda i: (i, 0))
    return pl.pallas_call(
        functools.partial(_outproj_kernel, a_w=a_w),
        grid=(t // tm,),
        in_specs=[row(a_w), row(b_w), _resident((a_w + b_w, d)), row(d),
                  pl.BlockSpec((1, N_MOD, d), lambda i: (i // tiles_per_batch, 0, 0)),
                  pl.BlockSpec((1, d), lambda i: (0, 0))],
        out_specs=[row(d), row(d)],
        out_shape=[jax.ShapeDtypeStruct((t, d), F32), jax.ShapeDtypeStruct((t, d), BF16)],
        compiler_params=_cparams(("arbitrary",)),
        name="outproj_norm2",
    )(out_a, out_b, w_bf, x2, mod3, gain)


def _ffn_kernel(h_ref, wg_ref, wu_ref, wd_ref, x1_ref, mod_ref, fg_ref, o_ref):
    f = pl.program_id(1)

    @pl.when(f == 0)
    def _():
        o_ref[...] = jnp.zeros_like(o_ref)

    h = h_ref[...]
    g = jnp.dot(h, wg_ref[...], preferred_element_type=F32)
    u = jnp.dot(h, wu_ref[...], preferred_element_type=F32)
    a = (g * jax.nn.sigmoid(g) * u).astype(BF16)
    o_ref[...] += jnp.dot(a, wd_ref[...], preferred_element_type=F32)

    @pl.when(f == pl.num_programs(1) - 1)
    def _():
        g2 = mod_ref[0, 5:6, :]
        o_ref[...] = _rms(x1_ref[...] + g2 * o_ref[...], fg_ref[...])


def _ffn(h2, wg, wu, wd, x1, mod3, final_gain, *, seq, tm=512, tf=512):
    t, d = h2.shape
    ff = wg.shape[1]
    tm = min(tm, seq)
    if ff % tf:
        tf = 256
    assert ff % tf == 0 and t % tm == 0 and seq % tm == 0
    tiles_per_batch = seq // tm
    return pl.pallas_call(
        _ffn_kernel,
        grid=(t // tm, ff // tf),
        in_specs=[pl.BlockSpec((tm, d), lambda i, f: (i, 0)),
                  pl.BlockSpec((d, tf), lambda i, f: (0, f)),
                  pl.BlockSpec((d, tf), lambda i, f: (0, f)),
                  pl.BlockSpec((tf, d), lambda i, f: (f, 0)),
                  pl.BlockSpec((tm, d), lambda i, f: (i, 0)),
                  pl.BlockSpec((1, N_MOD, d), lambda i, f: (i // tiles_per_batch, 0, 0)),
                  pl.BlockSpec((1, d), lambda i, f: (0, 0))],
        out_specs=pl.BlockSpec((tm, d), lambda i, f: (i, 0)),
        out_shape=jax.ShapeDtypeStruct((t, d), F32),
        compiler_params=_cparams(("arbitrary", "arbitrary")),
        name="swiglu_ffn_final_norm",
    )(h2, wg, wu, wd, x1, mod3, final_gain)


def kernel(x, c, w_ada, b_ada, norm1_gain, w_in, a_sink, a_out_gain, diff_lq1, diff_lk1,
           diff_lq2, diff_lk2, diff_subln_gain, w_o, norm2_gain, w_gate, w_up, w_down,
           final_gain):
    batch, seq, d = x.shape
    assert w_ada.shape[0] == 1, "single-layer block"
    a_w = d // 2
    b_w = d - a_w
    a_heads = a_w // HEAD_DIM
    b_heads = b_w // B_V_DIM
    a_kv = A_KV_HEADS * HEAD_DIM
    n_heads = a_heads + b_heads
    slopes = 2.0 ** (-8.0 * jnp.arange(1, n_heads + 1, dtype=F32) / n_heads)

    rows = 8
    c_pad = jnp.zeros((rows, d), F32).at[:batch].set(c)
    mod = _ada(c_pad, w_ada[0], b_ada[0][None, :])[:batch]
    mod3 = mod.reshape(batch, N_MOD, d)

    x2 = x.reshape(batch * seq, d)
    qa, ka, va, qdt, kd, vdt = _inproj(
        x2, mod3, norm1_gain[0][None, :], w_in[0].astype(BF16),
        seq=seq, a_q=a_w, a_kv=a_kv, b_w=b_w)

    out_a = _attn_a(slopes[:a_heads], a_sink[0].astype(F32), qa, ka, va,
                    a_out_gain[0][None, :], batch=batch, seq=seq)
    out_b = _attn_b(slopes[a_heads:], diff_lq1[0][None, :], diff_lk1[0][None, :],
                    diff_lq2[0][None, :], diff_lk2[0][None, :], diff_subln_gain[0][None, :],
                    qdt, kd, vdt, batch=batch, seq=seq)

    x1, h2 = _outproj(out_a, out_b, w_o[0].astype(BF16), x2, mod3, norm2_gain[0][None, :],
                      seq=seq)
    out = _ffn(h2, w_gate[0].astype(BF16), w_up[0].astype(BF16), w_down[0].astype(BF16),
               x1, mod3, final_gain[None, :], seq=seq)
    return out.reshape(batch, seq, d)
```

```python
import functools
import math

import jax
import jax.numpy as jnp
from jax import lax
from jax.experimental import pallas as pl
from jax.experimental.pallas import tpu as pltpu

HEAD_DIM = 128
A_KV_HEADS = 2
WINDOW = 128
B_QK_DIM = 64
B_V_DIM = 2 * B_QK_DIM
N_MOD = 6
EPS = 1e-6
NEG_INF = -1e30
LAM_INIT = 0.8 - 0.6 * math.exp(-0.3 * 0)
LOG2E = math.log2(math.e)
GROUP = 6
S_PAD_LANES = 128
N_BIAS_ROWS = 16

V7X_VMEM_LIMIT_BYTES = 56 * 1024 * 1024

BF16 = jnp.bfloat16
F32 = jnp.float32


def _cparams(semantics):
    return pltpu.CompilerParams(dimension_semantics=semantics,
                                vmem_limit_bytes=V7X_VMEM_LIMIT_BYTES)


def _rms(x, gain):
    return x * lax.rsqrt(jnp.mean(x * x, axis=-1, keepdims=True) + EPS) * gain


def _resident(shape):
    return pl.BlockSpec(shape, lambda *_: (0,) * len(shape), pipeline_mode=pl.Buffered(1))


def _ada_kernel(c_ref, w_ref, b_ref, o_ref):
    c = c_ref[...]
    sc = (c * jax.nn.sigmoid(c)).astype(BF16)
    o_ref[...] = jnp.dot(sc, w_ref[...].astype(BF16), preferred_element_type=F32) + b_ref[...]


def _ada(c_pad, w, b, tn=1024):
    rows, d = c_pad.shape
    n = w.shape[1]
    return pl.pallas_call(
        _ada_kernel,
        grid=(n // tn,),
        in_specs=[pl.BlockSpec((rows, d), lambda j: (0, 0)),
                  pl.BlockSpec((d, tn), lambda j: (0, j)),
                  pl.BlockSpec((1, tn), lambda j: (0, j))],
        out_specs=pl.BlockSpec((rows, tn), lambda j: (0, j)),
        out_shape=jax.ShapeDtypeStruct((rows, n), F32),
        compiler_params=_cparams(("arbitrary",)),
        name="ada_mod",
    )(c_pad, w, b)


def _inproj_kernel(x_ref, mod_ref, g_ref, w_ref,
                   qa_ref, ka_ref, va_ref, qdt_ref, kd_ref, vdt_ref, h_ref,
                   *, a_q, a_kv, b_w, chunk):
    x = x_ref[...]
    sh1 = mod_ref[0, 0:1, :]
    sc1 = mod_ref[0, 1:2, :]
    h_ref[...] = (_rms(x, g_ref[...]) * (1.0 + sc1) + sh1).astype(BF16)

    def proj(c0, width):
        return jnp.dot(h_ref[...], w_ref[:, c0:c0 + width], preferred_element_type=F32)

    o1 = a_q
    o2 = o1 + a_kv
    o3 = o2 + a_kv
    o4 = o3 + b_w
    o5 = o4 + b_w
    for c in range(0, a_q, chunk):
        qa_ref[:, c:c + chunk] = proj(c, chunk).astype(BF16)
    ka_ref[...] = proj(o1, a_kv).astype(BF16)
    va_ref[...] = proj(o2, a_kv).astype(BF16)
    for c in range(0, b_w, chunk):
        kd_ref[:, c:c + chunk] = proj(o4 + c, chunk).astype(BF16)
    qscale = B_QK_DIM ** -0.5 * LOG2E
    for c in range(0, b_w, chunk):
        q = proj(o3 + c, chunk) * qscale
        v = proj(o5 + c, chunk)
        for hc in range(0, chunk, B_V_DIM):
            qdt_ref[c + hc:c + hc + B_V_DIM, :] = q[:, hc:hc + B_V_DIM].T.astype(BF16)
            vdt_ref[c + hc:c + hc + B_V_DIM, :] = v[:, hc:hc + B_V_DIM].T.astype(BF16)


def _inproj(x2, mod3, gain, w_bf, *, seq, a_q, a_kv, b_w, tm=512):
    t, d = x2.shape
    n = w_bf.shape[1]
    tiles_per_batch = seq // tm
    chunk = min(512, a_q, b_w)
    kern = functools.partial(_inproj_kernel, a_q=a_q, a_kv=a_kv, b_w=b_w, chunk=chunk)
    row = lambda width: pl.BlockSpec((tm, width), lambda i: (i, 0))
    col = lambda height: pl.BlockSpec((height, tm), lambda i: (0, i))
    return pl.pallas_call(
        kern,
        grid=(t // tm,),
        in_specs=[row(d),
                  pl.BlockSpec((1, N_MOD, d), lambda i: (i // tiles_per_batch, 0, 0)),
                  pl.BlockSpec((1, d), lambda i: (0, 0)),
                  _resident((d, n))],
        out_specs=[row(a_q), row(a_kv), row(a_kv), col(b_w), row(b_w), col(b_w)],
        out_shape=[jax.ShapeDtypeStruct((t, a_q), BF16),
                   jax.ShapeDtypeStruct((t, a_kv), BF16),
                   jax.ShapeDtypeStruct((t, a_kv), BF16),
                   jax.ShapeDtypeStruct((b_w, t), BF16),
                   jax.ShapeDtypeStruct((t, b_w), BF16),
                   jax.ShapeDtypeStruct((b_w, t), BF16)],
        scratch_shapes=[pltpu.VMEM((tm, d), BF16)],
        compiler_params=_cparams(("arbitrary",)),
        name="norm1_inproj",
    )(x2, mod3, gain, w_bf)


def _attn_a_kernel(slopes_ref, sink_ref, q_ref, k_ref, v_ref, g_ref, o_ref, bias_ref, acc_ref,
                   *, tq, kw, seq, heads):
    i = pl.program_id(1)
    q0 = i * tq
    kstart = pl.multiple_of(jnp.clip(q0 - WINDOW, 0, seq - kw), WINDOW)
    group = heads // A_KV_HEADS

    @pl.when((pl.program_id(0) == 0) & (i == 0))
    def _():
        r = lax.broadcasted_iota(jnp.int32, (tq, kw), 0)
        c = lax.broadcasted_iota(jnp.int32, (tq, kw), 1)
        for case in range(3):
            dist = jnp.abs((r - c) + case * WINDOW)
            for h in range(heads):
                bias_ref[case, h] = jnp.where(dist <= WINDOW,
                                              -(slopes_ref[h] * LOG2E) * dist.astype(F32),
                                              NEG_INF)

    case = (q0 - kstart) // WINDOW
    scale = HEAD_DIM ** -0.5
    kwins = [k_ref[pl.ds(kstart, kw), kvh * HEAD_DIM:(kvh + 1) * HEAD_DIM]
             for kvh in range(A_KV_HEADS)]
    vwins = [v_ref[pl.ds(kstart, kw), kvh * HEAD_DIM:(kvh + 1) * HEAD_DIM]
             for kvh in range(A_KV_HEADS)]
    scores = [lax.dot_general(q_ref[:, h * HEAD_DIM:(h + 1) * HEAD_DIM], kwins[h // group],
                              (((1,), (1,)), ((), ())), preferred_element_type=F32)
              for h in range(heads)]
    probs = []
    inv_denoms = []
    for h in range(heads):
        s = scores[h] * (scale * LOG2E) + bias_ref[case, h]
        sink = sink_ref[h] * LOG2E
        m = jnp.maximum(jnp.max(s, axis=-1, keepdims=True), sink)
        p = jnp.exp2(s - m)
        denom = jnp.sum(p, axis=-1, keepdims=True) + jnp.exp2(sink - m)
        probs.append(p.astype(BF16))
        inv_denoms.append(1.0 / denom)
    for h in range(heads):
        acc_ref[:, h * HEAD_DIM:(h + 1) * HEAD_DIM] = inv_denoms[h] * jnp.dot(
            probs[h], vwins[h // group], preferred_element_type=F32)
    o_ref[...] = _rms(acc_ref[...], g_ref[...]).astype(BF16)


def _attn_a(slopes_a, sink, qa, ka, va, gain, *, batch, seq, tq=128):
    t, a_q = qa.shape
    a_kv = ka.shape[1]
    heads = a_q // HEAD_DIM
    kw = tq + 2 * WINDOW
    nq = seq // tq
    kern = functools.partial(_attn_a_kernel, tq=tq, kw=kw, seq=seq, heads=heads)
    smem = pl.BlockSpec(memory_space=pltpu.SMEM)
    return pl.pallas_call(
        kern,
        grid=(batch, nq),
        in_specs=[smem, smem,
                  pl.BlockSpec((tq, a_q), lambda b, i: (b * nq + i, 0)),
                  pl.BlockSpec((seq, a_kv), lambda b, i: (b, 0)),
                  pl.BlockSpec((seq, a_kv), lambda b, i: (b, 0)),
                  pl.BlockSpec((1, a_q), lambda b, i: (0, 0))],
        out_specs=pl.BlockSpec((tq, a_q), lambda b, i: (b * nq + i, 0)),
        out_shape=jax.ShapeDtypeStruct((t, a_q), BF16),
        scratch_shapes=[pltpu.VMEM((3, heads, tq, kw), F32), pltpu.VMEM((tq, a_q), F32)],
        compiler_params=_cparams(("arbitrary", "arbitrary")),
        name="attn_window_gqa",
    )(slopes_a, sink, qa, ka, va, gain)


def _attn_b_kernel(slopes_ref, lq1_ref, lk1_ref, lq2_ref, lk2_ref, g_ref, d0_ref, feat_ref,
                   qt_ref, k_ref, vt_ref, o_ref,
                   w_ref, diag_ref, s0_ref, s1_ref, mx0_ref, mx1_ref, m_ref, l_ref, acc_ref,
                   *, tile, sw, seq):
    h = pl.program_id(1)
    slope2 = slopes_ref[h] * LOG2E
    lam = (jnp.exp(jnp.sum(lq1_ref[...] * lk1_ref[...], keepdims=True))
           - jnp.exp(jnp.sum(lq2_ref[...] * lk2_ref[...], keepdims=True)) + LAM_INIT)
    halves = tile // sw
    n_tiles = seq // tile

    rho = lax.broadcasted_iota(jnp.int32, (N_BIAS_ROWS, sw), 0)
    lane = lax.broadcasted_iota(jnp.int32, (N_BIAS_ROWS, sw), 1).astype(F32)
    coeff = jnp.where(rho < 3, slope2,
                      jnp.where(rho < 6, 256.0 * slope2,
                                jnp.where(rho < 9, -slope2 * lane, 0.0)))
    hi = coeff.astype(BF16).astype(F32)
    mid = (coeff - hi).astype(BF16).astype(F32)
    lo = (coeff - hi - mid).astype(BF16).astype(F32)
    level = rho % 3
    rows = jnp.where(level == 0, hi, jnp.where(level == 1, mid, lo))
    w_ref[...] = jnp.zeros_like(w_ref)
    for strip in range(2 * halves):
        cols = slice(strip * sw, (strip + 1) * sw)
        w_ref[1, B_V_DIM:B_V_DIM + N_BIAS_ROWS, cols] = rows.astype(BF16)
        w_ref[2, B_V_DIM:B_V_DIM + N_BIAS_ROWS, cols] = (-rows).astype(BF16)
    for half in range(halves):
        diag_ref[half] = -slope2 * jnp.abs(d0_ref[...] - float(half * sw))

    def key_tile(i, t):
        return jnp.where(t == n_tiles - 1, i, jnp.where(t < i, t, t + 1))

    def scores(i, t, s_ref, mx_ref):
        j = key_tile(i, t)
        k0 = pl.multiple_of(j * tile, tile)
        widx = jnp.where(j == i, 0, jnp.where(j < i, 1, 2))
        lhs = jnp.concatenate([k_ref[pl.ds(k0, tile), :], feat_ref[...]], axis=1)
        s = jnp.dot(lhs, w_ref[widx], preferred_element_type=F32)
        s_ref[:, 0:2 * tile] = s
        mx_ref[...] = jnp.max(s, axis=0, keepdims=True)

    def softmax_pv(i, t, s_ref, mx_ref, same_tile=False):
        j = key_tile(i, t)
        k0 = pl.multiple_of(j * tile, tile)
        q0 = i * tile
        p_strips = []
        alphas = []
        for half in range(halves):
            cst = slope2 * (k0 - q0 - half * sw).astype(F32)
            tc = 0.0 if same_tile else jnp.where(j < i, cst, -cst)
            for comp in range(2):
                cols = slice((2 * half + comp) * sw, (2 * half + comp + 1) * sw)
                t_sc = s_ref[:, cols]
                if same_tile:
                    t_sc = t_sc + diag_ref[half]
                    t_max = jnp.max(t_sc, axis=0, keepdims=True)
                else:
                    t_max = mx_ref[:, cols]
                m_old = m_ref[:, cols]
                m_new = jnp.maximum(m_old, t_max + tc)
                alpha = jnp.exp2(m_old - m_new)
                p = jnp.exp2(t_sc - (m_new - tc))
                l_ref[:, cols] = alpha * l_ref[:, cols] + jnp.sum(p, axis=0, keepdims=True)
                m_ref[:, cols] = m_new
                p_strips.append(p.astype(BF16))
                alphas.append(alpha)
        pv = jnp.dot(vt_ref[:, pl.ds(k0, tile)], jnp.concatenate(p_strips, axis=1),
                     preferred_element_type=F32)
        acc_ref[...] = jnp.concatenate(alphas, axis=1) * acc_ref[...] + pv

    def q_body(i, carry):
        q0 = pl.multiple_of(i * tile, tile)
        for half in range(halves):
            qcols = pl.ds(q0 + half * sw, sw)
            c0 = 2 * half * sw
            for variant in range(3):
                w_ref[variant, 0:B_QK_DIM, c0:c0 + sw] = qt_ref[0:B_QK_DIM, qcols]
                w_ref[variant, B_QK_DIM:B_V_DIM, c0 + sw:c0 + 2 * sw] = qt_ref[B_QK_DIM:, qcols]
        m_ref[...] = jnp.full_like(m_ref, NEG_INF)
        l_ref[...] = jnp.zeros_like(l_ref)
        acc_ref[...] = jnp.zeros_like(acc_ref)

        s_refs = ((s0_ref, mx0_ref), (s1_ref, mx1_ref))
        scores(i, 0, *s_refs[0])

        def group(n, c2):
            for u in range(GROUP):
                scores(i, GROUP * n + u + 1, *s_refs[(u + 1) % 2])
                softmax_pv(i, GROUP * n + u, *s_refs[u % 2])
            return c2

        n_groups = (n_tiles - 1) // GROUP
        lax.fori_loop(0, n_groups, group, 0)
        for t in range(n_groups * GROUP, n_tiles):
            if t + 1 < n_tiles:
                scores(i, t + 1, *s_refs[(t + 1) % 2])
            softmax_pv(i, t, *s_refs[t % 2], same_tile=(t == n_tiles - 1))

        o = acc_ref[...] / l_ref[...]
        for half in range(halves):
            c0 = 2 * half * sw
            od = (o[:, c0:c0 + sw] - lam * o[:, c0 + sw:c0 + 2 * sw]).T
            o_ref[pl.ds(q0 + half * sw, sw), :] = (
                _rms(od, g_ref[...]) * (1.0 - LAM_INIT)).astype(BF16)
        return carry

    lax.fori_loop(0, n_tiles, q_body, 0)


def _attn_b(slopes_b, lq1, lk1, lq2, lk2, gain, qdt, kd, vdt, *, batch, seq, tile=512, sw=256):
    b_w, t = qdt.shape
    heads = b_w // B_V_DIM
    tile = min(tile, seq // 2)
    assert seq % (2 * tile) == 0 and tile % sw == 0
    r = lax.broadcasted_iota(jnp.int32, (tile, sw), 0)
    c = lax.broadcasted_iota(jnp.int32, (tile, sw), 1)
    d0 = (r - c).astype(F32)
    rk = jnp.arange(tile, dtype=jnp.int32)[:, None]
    fcol = jnp.arange(B_V_DIM, dtype=jnp.int32)[None, :]
    feat = jnp.where(fcol < 3, rk % 256,
                     jnp.where(fcol < 6, rk // 256, jnp.where(fcol < 9, 1, 0))).astype(BF16)
    kern = functools.partial(_attn_b_kernel, tile=tile, sw=sw, seq=seq)
    smem = pl.BlockSpec(memory_space=pltpu.SMEM)
    vec = lambda n: pl.BlockSpec((1, n), lambda b, h: (0, 0))
    tposed = pl.BlockSpec((B_V_DIM, seq), lambda b, h: (h, b))
    natural = pl.BlockSpec((seq, B_V_DIM), lambda b, h: (b, h))
    return pl.pallas_call(
        kern,
        grid=(batch, heads),
        in_specs=[smem, vec(B_QK_DIM), vec(B_QK_DIM), vec(B_QK_DIM), vec(B_QK_DIM),
                  vec(B_V_DIM),
                  pl.BlockSpec((tile, sw), lambda b, h: (0, 0)),
                  pl.BlockSpec((tile, B_V_DIM), lambda b, h: (0, 0)),
                  tposed, natural, tposed],
        out_specs=natural,
        out_shape=jax.ShapeDtypeStruct((t, b_w), BF16),
        scratch_shapes=[pltpu.VMEM((3, 2 * B_V_DIM, 2 * tile), BF16),
                        pltpu.VMEM((tile // sw, tile, sw), F32),
                        pltpu.VMEM((tile, 2 * tile + S_PAD_LANES), F32),
                        pltpu.VMEM((tile, 2 * tile + S_PAD_LANES), F32),
                        pltpu.VMEM((1, 2 * tile), F32),
                        pltpu.VMEM((1, 2 * tile), F32),
                        pltpu.VMEM((1, 2 * tile), F32),
                        pltpu.VMEM((1, 2 * tile), F32),
                        pltpu.VMEM((B_V_DIM, 2 * tile), F32)],
        compiler_params=_cparams(("arbitrary", "arbitrary")),
        name="attn_diff",
    )(slopes_b, lq1, lk1, lq2, lk2, gain, d0, feat, qdt, kd, vdt)


def _outproj_kernel(a_ref, b_ref, w_ref, x_ref, mod_ref, g_ref, x1_ref, h2_ref, *, a_w):
    mix = (jnp.dot(a_ref[...], w_ref[0:a_w, :], preferred_element_type=F32)
           + jnp.dot(b_ref[...], w_ref[a_w:, :], preferred_element_type=F32))
    g1 = mod_ref[0, 2:3, :]
    sh2 = mod_ref[0, 3:4, :]
    sc2 = mod_ref[0, 4:5, :]
    x1 = x_ref[...] + g1 * mix
    x1_ref[...] = x1
    h2_ref[...] = (_rms(x1, g_ref[...]) * (1.0 + sc2) + sh2).astype(BF16)


def _outproj(out_a, out_b, w_bf, x2, mod3, gain, *, seq, tm=512):
    t, d = x2.shape
    a_w = out_a.shape[1]
    b_w = out_b.shape[1]
    tiles_per_batch = seq // tm
    row = lambda width: pl.BlockSpec((tm, width), lambda i: (i, 0))
    return pl.pallas_call(
        functools.partial(_outproj_kernel, a_w=a_w),
        grid=(t // tm,),
        in_specs=[row(a_w), row(b_w), _resident((a_w + b_w, d)), row(d),
                  pl.BlockSpec((1, N_MOD, d), lambda i: (i // tiles_per_batch, 0, 0)),
                  pl.BlockSpec((1, d), lambda i: (0, 0))],
        out_specs=[row(d), row(d)],
        out_shape=[jax.ShapeDtypeStruct((t, d), F32), jax.ShapeDtypeStruct((t, d), BF16)],
        compiler_params=_cparams(("arbitrary",)),
        name="outproj_norm2",
    )(out_a, out_b, w_bf, x2, mod3, gain)


def _ffn_kernel(h_ref, wg_ref, wu_ref, wd_ref, x1_ref, mod_ref, fg_ref, o_ref):
    f = pl.program_id(1)

    @pl.when(f == 0)
    def _():
        o_ref[...] = jnp.zeros_like(o_ref)

    h = h_ref[...]
    g = jnp.dot(h, wg_ref[...], preferred_element_type=F32)
    u = jnp.dot(h, wu_ref[...], preferred_element_type=F32)
    a = (g * jax.nn.sigmoid(g) * u).astype(BF16)
    o_ref[...] += jnp.dot(a, wd_ref[...], preferred_element_type=F32)

    @pl.when(f == pl.num_programs(1) - 1)
    def _():
        g2 = mod_ref[0, 5:6, :]
        o_ref[...] = _rms(x1_ref[...] + g2 * o_ref[...], fg_ref[...])


def _ffn(h2, wg, wu, wd, x1, mod3, final_gain, *, seq, tm=512, tf=512):
    t, d = h2.shape
    ff = wg.shape[1]
    tm = min(tm, seq)
    if ff % tf:
        tf = 256
    assert ff % tf == 0 and t % tm == 0 and seq % tm == 0
    tiles_per_batch = seq // tm
    return pl.pallas_call(
        _ffn_kernel,
        grid=(t // tm, ff // tf),
        in_specs=[pl.BlockSpec((tm, d), lambda i, f: (i, 0)),
                  pl.BlockSpec((d, tf), lambda i, f: (0, f)),
                  pl.BlockSpec((d, tf), lambda i, f: (0, f)),
                  pl.BlockSpec((tf, d), lambda i, f: (f, 0)),
                  pl.BlockSpec((tm, d), lambda i, f: (i, 0)),
                  pl.BlockSpec((1, N_MOD, d), lambda i, f: (i // tiles_per_batch, 0, 0)),
                  pl.BlockSpec((1, d), lambda i, f: (0, 0))],
        out_specs=pl.BlockSpec((tm, d), lambda i, f: (i, 0)),
        out_shape=jax.ShapeDtypeStruct((t, d), F32),
        compiler_params=_cparams(("arbitrary", "arbitrary")),
        name="swiglu_ffn_final_norm",
    )(h2, wg, wu, wd, x1, mod3, final_gain)


def kernel(x, c, w_ada, b_ada, norm1_gain, w_in, a_sink, a_out_gain, diff_lq1, diff_lk1,
           diff_lq2, diff_lk2, diff_subln_gain, w_o, norm2_gain, w_gate, w_up, w_down,
           final_gain):
    batch, seq, d = x.shape
    assert w_ada.shape[0] == 1, "single-layer block"
    a_w = d // 2
    b_w = d - a_w
    a_heads = a_w // HEAD_DIM
    b_heads = b_w // B_V_DIM
    a_kv = A_KV_HEADS * HEAD_DIM
    n_heads = a_heads + b_heads
    slopes = 2.0 ** (-8.0 * jnp.arange(1, n_heads + 1, dtype=F32) / n_heads)

    rows = 8
    c_pad = jnp.zeros((rows, d), F32).at[:batch].set(c)
    mod = _ada(c_pad, w_ada[0], b_ada[0][None, :])[:batch]
    mod3 = mod.reshape(batch, N_MOD, d)

    x2 = x.reshape(batch * seq, d)
    qa, ka, va, qdt, kd, vdt = _inproj(
        x2, mod3, norm1_gain[0][None, :], w_in[0].astype(BF16),
        seq=seq, a_q=a_w, a_kv=a_kv, b_w=b_w)

    out_a = _attn_a(slopes[:a_heads], a_sink[0].astype(F32), qa, ka, va,
                    a_out_gain[0][None, :], batch=batch, seq=seq)
    out_b = _attn_b(slopes[a_heads:], diff_lq1[0][None, :], diff_lk1[0][None, :],
                    diff_lq2[0][None, :], diff_lk2[0][None, :], diff_subln_gain[0][None, :],
                    qdt, kd, vdt, batch=batch, seq=seq)

    x1, h2 = _outproj(out_a, out_b, w_o[0].astype(BF16), x2, mod3, norm2_gain[0][None, :],
                      seq=seq)
    out = _ffn(h2, w_gate[0].astype(BF16), w_up[0].astype(BF16), w_down[0].astype(BF16),
               x1, mod3, final_gain[None, :], seq=seq)
    return out.reshape(batch, seq, d)
```

```python
import functools
import math

import jax
import jax.numpy as jnp
from jax import lax
from jax.experimental import pallas as pl
from jax.experimental.pallas import tpu as pltpu

HEAD_DIM = 128
A_KV_HEADS = 2
WINDOW = 128
B_QK_DIM = 64
B_V_DIM = 2 * B_QK_DIM
N_MOD = 6
EPS = 1e-6
NEG_INF = -1e30
LAM_INIT = 0.8 - 0.6 * math.exp(-0.3 * 0)
LOG2E = math.log2(math.e)
GROUP = 6
S_PAD_LANES = 128
N_BIAS_ROWS = 16

V7X_VMEM_LIMIT_BYTES = 56 * 1024 * 1024

BF16 = jnp.bfloat16
F32 = jnp.float32


def _cparams(semantics):
    return pltpu.CompilerParams(dimension_semantics=semantics,
                                vmem_limit_bytes=V7X_VMEM_LIMIT_BYTES)


def _rms(x, gain):
    return x * lax.rsqrt(jnp.mean(x * x, axis=-1, keepdims=True) + EPS) * gain


def _resident(shape):
    return pl.BlockSpec(shape, lambda *_: (0,) * len(shape), pipeline_mode=pl.Buffered(1))


def _ada_kernel(c_ref, w_ref, b_ref, o_ref):
    c = c_ref[...]
    sc = (c * jax.nn.sigmoid(c)).astype(BF16)
    o_ref[...] = jnp.dot(sc, w_ref[...].astype(BF16), preferred_element_type=F32) + b_ref[...]


def _ada(c_pad, w, b, tn=1024):
    rows, d = c_pad.shape
    n = w.shape[1]
    return pl.pallas_call(
        _ada_kernel,
        grid=(n // tn,),
        in_specs=[pl.BlockSpec((rows, d), lambda j: (0, 0)),
                  pl.BlockSpec((d, tn), lambda j: (0, j)),
                  pl.BlockSpec((1, tn), lambda j: (0, j))],
        out_specs=pl.BlockSpec((rows, tn), lambda j: (0, j)),
        out_shape=jax.ShapeDtypeStruct((rows, n), F32),
        compiler_params=_cparams(("arbitrary",)),
        name="ada_mod",
    )(c_pad, w, b)


def _inproj_kernel(x_ref, mod_ref, g_ref, w_ref,
                   qa_ref, ka_ref, va_ref, qdt_ref, kd_ref, vdt_ref, h_ref,
                   *, a_q, a_kv, b_w, chunk):
    x = x_ref[...]
    sh1 = mod_ref[0, 0:1, :]
    sc1 = mod_ref[0, 1:2, :]
    h_ref[...] = (_rms(x, g_ref[...]) * (1.0 + sc1) + sh1).astype(BF16)

    def proj(c0, width):
        return jnp.dot(h_ref[...], w_ref[:, c0:c0 + width], preferred_element_type=F32)

    o1 = a_q
    o2 = o1 + a_kv
    o3 = o2 + a_kv
    o4 = o3 + b_w
    o5 = o4 + b_w
    for c in range(0, a_q, chunk):
        qa_ref[:, c:c + chunk] = proj(c, chunk).astype(BF16)
    ka_ref[...] = proj(o1, a_kv).astype(BF16)
    va_ref[...] = proj(o2, a_kv).astype(BF16)
    for c in range(0, b_w, chunk):
        kd_ref[:, c:c + chunk] = proj(o4 + c, chunk).astype(BF16)
    qscale = B_QK_DIM ** -0.5 * LOG2E
    for c in range(0, b_w, chunk):
        q = proj(o3 + c, chunk) * qscale
        v = proj(o5 + c, chunk)
        for hc in range(0, chunk, B_V_DIM):
            qdt_ref[c + hc:c + hc + B_V_DIM, :] = q[:, hc:hc + B_V_DIM].T.astype(BF16)
            vdt_ref[c + hc:c + hc + B_V_DIM, :] = v[:, hc:hc + B_V_DIM].T.astype(BF16)


def _inproj(x2, mod3, gain, w_bf, *, seq, a_q, a_kv, b_w, tm=512):
    t, d = x2.shape
    n = w_bf.shape[1]
    tiles_per_batch = seq // tm
    chunk = min(512, a_q, b_w)
    kern = functools.partial(_inproj_kernel, a_q=a_q, a_kv=a_kv, b_w=b_w, chunk=chunk)
    row = lambda width: pl.BlockSpec((tm, width), lambda i: (i, 0))
    col = lambda height: pl.BlockSpec((height, tm), lambda i: (0, i))
    return pl.pallas_call(
        kern,
        grid=(t // tm,),
        in_specs=[row(d),
                  pl.BlockSpec((1, N_MOD, d), lambda i: (i // tiles_per_batch, 0, 0)),
                  pl.BlockSpec((1, d), lambda i: (0, 0)),
                  _resident((d, n))],
        out_specs=[row(a_q), row(a_kv), row(a_kv), col(b_w), row(b_w), col(b_w)],
        out_shape=[jax.ShapeDtypeStruct((t, a_q), BF16),
                   jax.ShapeDtypeStruct((t, a_kv), BF16),
                   jax.ShapeDtypeStruct((t, a_kv), BF16),
                   jax.ShapeDtypeStruct((b_w, t), BF16),
                   jax.ShapeDtypeStruct((t, b_w), BF16),
                   jax.ShapeDtypeStruct((b_w, t), BF16)],
        scratch_shapes=[pltpu.VMEM((tm, d), BF16)],
        compiler_params=_cparams(("arbitrary",)),
        name="norm1_inproj",
    )(x2, mod3, gain, w_bf)


def _attn_a_kernel(slopes_ref, sink_ref, q_ref, k_ref, v_ref, g_ref, o_ref, bias_ref, acc_ref,
                   *, tq, kw, seq, heads):
    i = pl.program_id(1)
    q0 = i * tq
    kstart = pl.multiple_of(jnp.clip(q0 - WINDOW, 0, seq - kw), WINDOW)
    group = heads // A_KV_HEADS

    @pl.when((pl.program_id(0) == 0) & (i == 0))
    def _():
        r = lax.broadcasted_iota(jnp.int32, (tq, kw), 0)
        c = lax.broadcasted_iota(jnp.int32, (tq, kw), 1)
        for case in range(3):
            dist = jnp.abs((r - c) + case * WINDOW)
            for h in range(heads):
                bias_ref[case, h] = jnp.where(dist <= WINDOW,
                                              -(slopes_ref[h] * LOG2E) * dist.astype(F32),
                                              NEG_INF)

    case = (q0 - kstart) // WINDOW
    scale = HEAD_DIM ** -0.5
    kwins = [k_ref[pl.ds(kstart, kw), kvh * HEAD_DIM:(kvh + 1) * HEAD_DIM]
             for kvh in range(A_KV_HEADS)]
    vwins = [v_ref[pl.ds(kstart, kw), kvh * HEAD_DIM:(kvh + 1) * HEAD_DIM]
             for kvh in range(A_KV_HEADS)]
    scores = [lax.dot_general(q_ref[:, h * HEAD_DIM:(h + 1) * HEAD_DIM], kwins[h // group],
                              (((1,), (1,)), ((), ())), preferred_element_type=F32)
              for h in range(heads)]
    probs = []
    inv_denoms = []
    for h in range(heads):
        s = scores[h] * (scale * LOG2E) + bias_ref[case, h]
        sink = sink_ref[h] * LOG2E
        m = jnp.maximum(jnp.max(s, axis=-1, keepdims=True), sink)
        p = jnp.exp2(s - m)
        denom = jnp.sum(p, axis=-1, keepdims=True) + jnp.exp2(sink - m)
        probs.append(p.astype(BF16))
        inv_denoms.append(1.0 / denom)
    for h in range(heads):
        acc_ref[:, h * HEAD_DIM:(h + 1) * HEAD_DIM] = inv_denoms[h] * jnp.dot(
            probs[h], vwins[h // group], preferred_element_type=F32)
    o_ref[...] = _rms(acc_ref[...], g_ref[...]).astype(BF16)


def _attn_a(slopes_a, sink, qa, ka, va, gain, *, batch, seq, tq=128):
    t, a_q = qa.shape
    a_kv = ka.shape[1]
    heads = a_q // HEAD_DIM
    kw = tq + 2 * WINDOW
    nq = seq // tq
    kern = functools.partial(_attn_a_kernel, tq=tq, kw=kw, seq=seq, heads=heads)
    smem = pl.BlockSpec(memory_space=pltpu.SMEM)
    return pl.pallas_call(
        kern,
        grid=(batch, nq),
        in_specs=[smem, smem,
                  pl.BlockSpec((tq, a_q), lambda b, i: (b * nq + i, 0)),
                  pl.BlockSpec((seq, a_kv), lambda b, i: (b, 0)),
                  pl.BlockSpec((seq, a_kv), lambda b, i: (b, 0)),
                  pl.BlockSpec((1, a_q), lambda b, i: (0, 0))],
        out_specs=pl.BlockSpec((tq, a_q), lambda b, i: (b * nq + i, 0)),
        out_shape=jax.ShapeDtypeStruct((t, a_q), BF16),
        scratch_shapes=[pltpu.VMEM((3, heads, tq, kw), F32), pltpu.VMEM((tq, a_q), F32)],
        compiler_params=_cparams(("arbitrary", "arbitrary")),
        name="attn_window_gqa",
    )(slopes_a, sink, qa, ka, va, gain)


def _attn_b_kernel(slopes_ref, lq1_ref, lk1_ref, lq2_ref, lk2_ref, g_ref, d0_ref, feat_ref,
                   qt_ref, k_ref, vt_ref, o_ref,
                   w_ref, diag_ref, s0_ref, s1_ref, mx0_ref, mx1_ref, m_ref, l_ref, acc_ref,
                   *, tile, sw, seq):
    h = pl.program_id(1)
    slope2 = slopes_ref[h] * LOG2E
    lam = (jnp.exp(jnp.sum(lq1_ref[...] * lk1_ref[...], keepdims=True))
           - jnp.exp(jnp.sum(lq2_ref[...] * lk2_ref[...], keepdims=True)) + LAM_INIT)
    halves = tile // sw
    n_tiles = seq // tile

    rho = lax.broadcasted_iota(jnp.int32, (N_BIAS_ROWS, sw), 0)
    lane = lax.broadcasted_iota(jnp.int32, (N_BIAS_ROWS, sw), 1).astype(F32)
    coeff = jnp.where(rho < 3, slope2,
                      jnp.where(rho < 6, 256.0 * slope2,
                                jnp.where(rho < 9, -slope2 * lane, 0.0)))
    hi = coeff.astype(BF16).astype(F32)
    mid = (coeff - hi).astype(BF16).astype(F32)
    lo = (coeff - hi - mid).astype(BF16).astype(F32)
    level = rho % 3
    rows = jnp.where(level == 0, hi, jnp.where(level == 1, mid, lo))
    w_ref[...] = jnp.zeros_like(w_ref)
    for strip in range(2 * halves):
        cols = slice(strip * sw, (strip + 1) * sw)
        w_ref[1, B_V_DIM:B_V_DIM + N_BIAS_ROWS, cols] = rows.astype(BF16)
        w_ref[2, B_V_DIM:B_V_DIM + N_BIAS_ROWS, cols] = (-rows).astype(BF16)
    for half in range(halves):
        diag_ref[half] = -slope2 * jnp.abs(d0_ref[...] - float(half * sw))

    def key_tile(i, t):
        return jnp.where(t == n_tiles - 1, i, jnp.where(t < i, t, t + 1))

    def scores(i, t, s_ref, mx_ref):
        j = key_tile(i, t)
        k0 = pl.multiple_of(j * tile, tile)
        widx = jnp.where(j == i, 0, jnp.where(j < i, 1, 2))
        lhs = jnp.concatenate([k_ref[pl.ds(k0, tile), :], feat_ref[...]], axis=1)
        s = jnp.dot(lhs, w_ref[widx], preferred_element_type=F32)
        s_ref[:, 0:2 * tile] = s
        mx_ref[...] = jnp.max(s, axis=0, keepdims=True)

    def softmax_pv(i, t, s_ref, mx_ref, same_tile=False):
        j = key_tile(i, t)
        k0 = pl.multiple_of(j * tile, tile)
        q0 = i * tile
        p_strips = []
        alphas = []
        for half in range(halves):
            cst = slope2 * (k0 - q0 - half * sw).astype(F32)
            tc = 0.0 if same_tile else jnp.where(j < i, cst, -cst)
            for comp in range(2):
                cols = slice((2 * half + comp) * sw, (2 * half + comp + 1) * sw)
                t_sc = s_ref[:, cols]
                if same_tile:
                    t_sc = t_sc + diag_ref[half]
                    t_max = jnp.max(t_sc, axis=0, keepdims=True)
                else:
                    t_max = mx_ref[:, cols]
                m_old = m_ref[:, cols]
                m_new = jnp.maximum(m_old, t_max + tc)
                alpha = jnp.exp2(m_old - m_new)
                p = jnp.exp2(t_sc - (m_new - tc))
                l_ref[:, cols] = alpha * l_ref[:, cols] + jnp.sum(p, axis=0, keepdims=True)
                m_ref[:, cols] = m_new
                pv = jnp.dot(vt_ref[:, pl.ds(k0, tile)], p.astype(BF16),
                             preferred_element_type=F32)
                acc_ref[:, cols] = alpha * acc_ref[:, cols] + pv

    s_refs = ((s0_ref, mx0_ref), (s1_ref, mx1_ref))

    def start_query_tile(i):
        q0 = pl.multiple_of(i * tile, tile)
        for half in range(halves):
            qcols = pl.ds(q0 + half * sw, sw)
            c0 = 2 * half * sw
            for variant in range(3):
                w_ref[variant, 0:B_QK_DIM, c0:c0 + sw] = qt_ref[0:B_QK_DIM, qcols]
                w_ref[variant, B_QK_DIM:B_V_DIM, c0 + sw:c0 + 2 * sw] = qt_ref[B_QK_DIM:, qcols]
        scores(i, 0, *s_refs[0])

    def q_body(i, carry):
        q0 = pl.multiple_of(i * tile, tile)
        m_ref[...] = jnp.full_like(m_ref, NEG_INF)
        l_ref[...] = jnp.zeros_like(l_ref)
        acc_ref[...] = jnp.zeros_like(acc_ref)

        def group(n, c2):
            for u in range(GROUP):
                scores(i, GROUP * n + u + 1, *s_refs[(u + 1) % 2])
                softmax_pv(i, GROUP * n + u, *s_refs[u % 2])
            return c2

        n_groups = (n_tiles - 1) // GROUP
        lax.fori_loop(0, n_groups, group, 0)
        for t in range(n_groups * GROUP, n_tiles):
            if t + 1 < n_tiles:
                scores(i, t + 1, *s_refs[(t + 1) % 2])
            softmax_pv(i, t, *s_refs[t % 2], same_tile=(t == n_tiles - 1))

        start_query_tile(jnp.minimum(i + 1, n_tiles - 1))

        o = acc_ref[...] / l_ref[...]
        for half in range(halves):
            c0 = 2 * half * sw
            od = (o[:, c0:c0 + sw] - lam * o[:, c0 + sw:c0 + 2 * sw]).T
            o_ref[pl.ds(q0 + half * sw, sw), :] = (
                _rms(od, g_ref[...]) * (1.0 - LAM_INIT)).astype(BF16)
        return carry

    start_query_tile(0)
    lax.fori_loop(0, n_tiles, q_body, 0)


def _attn_b(slopes_b, lq1, lk1, lq2, lk2, gain, qdt, kd, vdt, *, batch, seq, tile=512, sw=256):
    b_w, t = qdt.shape
    heads = b_w // B_V_DIM
    tile = min(tile, seq // 2)
    assert seq % (2 * tile) == 0 and tile % sw == 0
    r = lax.broadcasted_iota(jnp.int32, (tile, sw), 0)
    c = lax.broadcasted_iota(jnp.int32, (tile, sw), 1)
    d0 = (r - c).astype(F32)
    rk = jnp.arange(tile, dtype=jnp.int32)[:, None]
    fcol = jnp.arange(B_V_DIM, dtype=jnp.int32)[None, :]
    feat = jnp.where(fcol < 3, rk % 256,
                     jnp.where(fcol < 6, rk // 256, jnp.where(fcol < 9, 1, 0))).astype(BF16)
    kern = functools.partial(_attn_b_kernel, tile=tile, sw=sw, seq=seq)
    smem = pl.BlockSpec(memory_space=pltpu.SMEM)
    vec = lambda n: pl.BlockSpec((1, n), lambda b, h: (0, 0))
    tposed = pl.BlockSpec((B_V_DIM, seq), lambda b, h: (h, b))
    natural = pl.BlockSpec((seq, B_V_DIM), lambda b, h: (b, h))
    return pl.pallas_call(
        kern,
        grid=(batch, heads),
        in_specs=[smem, vec(B_QK_DIM), vec(B_QK_DIM), vec(B_QK_DIM), vec(B_QK_DIM),
                  vec(B_V_DIM),
                  pl.BlockSpec((tile, sw), lambda b, h: (0, 0)),
                  pl.BlockSpec((tile, B_V_DIM), lambda b, h: (0, 0)),
                  tposed, natural, tposed],
        out_specs=natural,
        out_shape=jax.ShapeDtypeStruct((t, b_w), BF16),
        scratch_shapes=[pltpu.VMEM((3, 2 * B_V_DIM, 2 * tile), BF16),
                        pltpu.VMEM((tile // sw, tile, sw), F32),
                        pltpu.VMEM((tile, 2 * tile + S_PAD_LANES), F32),
                        pltpu.VMEM((tile, 2 * tile + S_PAD_LANES), F32),
                        pltpu.VMEM((1, 2 * tile), F32),
                        pltpu.VMEM((1, 2 * tile), F32),
                        pltpu.VMEM((1, 2 * tile), F32),
                        pltpu.VMEM((1, 2 * tile), F32),
                        pltpu.VMEM((B_V_DIM, 2 * tile), F32)],
        compiler_params=_cparams(("arbitrary", "arbitrary")),
        name="attn_diff",
    )(slopes_b, lq1, lk1, lq2, lk2, gain, d0, feat, qdt, kd, vdt)


def _outproj_kernel(a_ref, b_ref, w_ref, x_ref, mod_ref, g_ref, x1_ref, h2_ref, *, a_w):
    mix = (jnp.dot(a_ref[...], w_ref[0:a_w, :], preferred_element_type=F32)
           + jnp.dot(b_ref[...], w_ref[a_w:, :], preferred_element_type=F32))
    g1 = mod_ref[0, 2:3, :]
    sh2 = mod_ref[0, 3:4, :]
    sc2 = mod_ref[0, 4:5, :]
    x1 = x_ref[...] + g1 * mix
    x1_ref[...] = x1
    h2_ref[...] = (_rms(x1, g_ref[...]) * (1.0 + sc2) + sh2).astype(BF16)


def _outproj(out_a, out_b, w_bf, x2, mod3, gain, *, seq, tm=512):
    t, d = x2.shape
    a_w = out_a.shape[1]
    b_w = out_b.shape[1]
    tiles_per_batch = seq // tm
    row = lambda width: pl.BlockSpec((tm, width), lambda i: (i, 0))
    return pl.pallas_call(
        functools.partial(_outproj_kernel, a_w=a_w),
        grid=(t // tm,),
        in_specs=[row(a_w), row(b_w), _resident((a_w + b_w, d)), row(d),
                  pl.BlockSpec((1, N_MOD, d), lambda i: (i // tiles_per_batch, 0, 0)),
                  pl.BlockSpec((1, d), lambda i: (0, 0))],
        out_specs=[row(d), row(d)],
        out_shape=[jax.ShapeDtypeStruct((t, d), F32), jax.ShapeDtypeStruct((t, d), BF16)],
        compiler_params=_cparams(("arbitrary",)),
        name="outproj_norm2",
    )(out_a, out_b, w_bf, x2, mod3, gain)


def _ffn_kernel(h_ref, wg_ref, wu_ref, wd_ref, x1_ref, mod_ref, fg_ref, o_ref):
    f = pl.program_id(1)

    @pl.when(f == 0)
    def _():
        o_ref[...] = jnp.zeros_like(o_ref)

    h = h_ref[...]
    g = jnp.dot(h, wg_ref[...], preferred_element_type=F32)
    u = jnp.dot(h, wu_ref[...], preferred_element_type=F32)
    a = (g * jax.nn.sigmoid(g) * u).astype(BF16)
    o_ref[...] += jnp.dot(a, wd_ref[...], preferred_element_type=F32)

    @pl.when(f == pl.num_programs(1) - 1)
    def _():
        g2 = mod_ref[0, 5:6, :]
        o_ref[...] = _rms(x1_ref[...] + g2 * o_ref[...], fg_ref[...])


def _ffn(h2, wg, wu, wd, x1, mod3, final_gain, *, seq, tm=512, tf=512):
    t, d = h2.shape
    ff = wg.shape[1]
    tm = min(tm, seq)
    if ff % tf:
        tf = 256
    assert ff % tf == 0 and t % tm == 0 and seq % tm == 0
    tiles_per_batch = seq // tm
    return pl.pallas_call(
        _ffn_kernel,
        grid=(t // tm, ff // tf),
        in_specs=[pl.BlockSpec((tm, d), lambda i, f: (i, 0)),
                  pl.BlockSpec((d, tf), lambda i, f: (0, f)),
                  pl.BlockSpec((d, tf), lambda i, f: (0, f)),
                  pl.BlockSpec((tf, d), lambda i, f: (f, 0)),
                  pl.BlockSpec((tm, d), lambda i, f: (i, 0)),
                  pl.BlockSpec((1, N_MOD, d), lambda i, f: (i // tiles_per_batch, 0, 0)),
                  pl.BlockSpec((1, d), lambda i, f: (0, 0))],
        out_specs=pl.BlockSpec((tm, d), lambda i, f: (i, 0)),
        out_shape=jax.ShapeDtypeStruct((t, d), F32),
        compiler_params=_cparams(("arbitrary", "arbitrary")),
        name="swiglu_ffn_final_norm",
    )(h2, wg, wu, wd, x1, mod3, final_gain)


def kernel(x, c, w_ada, b_ada, norm1_gain, w_in, a_sink, a_out_gain, diff_lq1, diff_lk1,
           diff_lq2, diff_lk2, diff_subln_gain, w_o, norm2_gain, w_gate, w_up, w_down,
           final_gain):
    batch, seq, d = x.shape
    assert w_ada.shape[0] == 1, "single-layer block"
    a_w = d // 2
    b_w = d - a_w
    a_heads = a_w // HEAD_DIM
    b_heads = b_w // B_V_DIM
    a_kv = A_KV_HEADS * HEAD_DIM
    n_heads = a_heads + b_heads
    slopes = 2.0 ** (-8.0 * jnp.arange(1, n_heads + 1, dtype=F32) / n_heads)

    rows = 8
    c_pad = jnp.zeros((rows, d), F32).at[:batch].set(c)
    mod = _ada(c_pad, w_ada[0], b_ada[0][None, :])[:batch]
    mod3 = mod.reshape(batch, N_MOD, d)

    x2 = x.reshape(batch * seq, d)
    qa, ka, va, qdt, kd, vdt = _inproj(
        x2, mod3, norm1_gain[0][None, :], w_in[0].astype(BF16),
        seq=seq, a_q=a_w, a_kv=a_kv, b_w=b_w)

    out_a = _attn_a(slopes[:a_heads], a_sink[0].astype(F32), qa, ka, va,
                    a_out_gain[0][None, :], batch=batch, seq=seq)
    out_b = _attn_b(slopes[a_heads:], diff_lq1[0][None, :], diff_lk1[0][None, :],
                    diff_lq2[0][None, :], diff_lk2[0][None, :], diff_subln_gain[0][None, :],
                    qdt, kd, vdt, batch=batch, seq=seq)

    x1, h2 = _outproj(out_a, out_b, w_o[0].astype(BF16), x2, mod3, norm2_gain[0][None, :],
                      seq=seq)
    out = _ffn(h2, w_gate[0].astype(BF16), w_up[0].astype(BF16), w_down[0].astype(BF16),
               x1, mod3, final_gain[None, :], seq=seq)
    return out.reshape(batch, seq, d)
```

```python
import functools
import math

import jax
import jax.numpy as jnp
from jax import lax
from jax.experimental import pallas as pl
from jax.experimental.pallas import tpu as pltpu

HEAD_DIM = 128
A_KV_HEADS = 2
WINDOW = 128
B_QK_DIM = 64
B_V_DIM = 2 * B_QK_DIM
N_MOD = 6
EPS = 1e-6
NEG_INF = -1e30
LAM_INIT = 0.8 - 0.6 * math.exp(-0.3 * 0)
LOG2E = math.log2(math.e)
GROUP = 6
FIXED_REF_MAX_EXCESS = 64.0
FIXED_REF_MAX_VALUE = 2.0 ** 30
N_BIAS_ROWS = 16

V7X_VMEM_LIMIT_BYTES = 56 * 1024 * 1024

BF16 = jnp.bfloat16
F32 = jnp.float32


def _cparams(semantics):
    return pltpu.CompilerParams(dimension_semantics=semantics,
                                vmem_limit_bytes=V7X_VMEM_LIMIT_BYTES)


def _rms(x, gain):
    return x * lax.rsqrt(jnp.mean(x * x, axis=-1, keepdims=True) + EPS) * gain


def _resident(shape):
    return pl.BlockSpec(shape, lambda *_: (0,) * len(shape), pipeline_mode=pl.Buffered(1))


def _ada_kernel(c_ref, w_ref, b_ref, o_ref):
    c = c_ref[...]
    sc = (c * jax.nn.sigmoid(c)).astype(BF16)
    o_ref[...] = jnp.dot(sc, w_ref[...].astype(BF16), preferred_element_type=F32) + b_ref[...]


def _ada(c_pad, w, b, tn=1024):
    rows, d = c_pad.shape
    n = w.shape[1]
    return pl.pallas_call(
        _ada_kernel,
        grid=(n // tn,),
        in_specs=[pl.BlockSpec((rows, d), lambda j: (0, 0)),
                  pl.BlockSpec((d, tn), lambda j: (0, j)),
                  pl.BlockSpec((1, tn), lambda j: (0, j))],
        out_specs=pl.BlockSpec((rows, tn), lambda j: (0, j)),
        out_shape=jax.ShapeDtypeStruct((rows, n), F32),
        compiler_params=_cparams(("arbitrary",)),
        name="ada_mod",
    )(c_pad, w, b)


def _inproj_kernel(x_ref, mod_ref, g_ref, w_ref,
                   qa_ref, ka_ref, va_ref, qdt_ref, kd_ref, vdt_ref, h_ref,
                   *, a_q, a_kv, b_w, chunk):
    x = x_ref[...]
    sh1 = mod_ref[0, 0:1, :]
    sc1 = mod_ref[0, 1:2, :]
    h_ref[...] = (_rms(x, g_ref[...]) * (1.0 + sc1) + sh1).astype(BF16)

    def proj(c0, width):
        return jnp.dot(h_ref[...], w_ref[:, c0:c0 + width], preferred_element_type=F32)

    o1 = a_q
    o2 = o1 + a_kv
    o3 = o2 + a_kv
    o4 = o3 + b_w
    o5 = o4 + b_w
    for c in range(0, a_q, chunk):
        qa_ref[:, c:c + chunk] = proj(c, chunk).astype(BF16)
    ka_ref[...] = proj(o1, a_kv).astype(BF16)
    va_ref[...] = proj(o2, a_kv).astype(BF16)
    for c in range(0, b_w, chunk):
        kd_ref[:, c:c + chunk] = proj(o4 + c, chunk).astype(BF16)
    qscale = B_QK_DIM ** -0.5 * LOG2E
    for c in range(0, b_w, chunk):
        q = proj(o3 + c, chunk) * qscale
        v = proj(o5 + c, chunk)
        for hc in range(0, chunk, B_V_DIM):
            qdt_ref[c + hc:c + hc + B_V_DIM, :] = q[:, hc:hc + B_V_DIM].T.astype(BF16)
            vdt_ref[c + hc:c + hc + B_V_DIM, :] = v[:, hc:hc + B_V_DIM].T.astype(BF16)


def _inproj(x2, mod3, gain, w_bf, *, seq, a_q, a_kv, b_w, tm=512):
    t, d = x2.shape
    n = w_bf.shape[1]
    tiles_per_batch = seq // tm
    chunk = min(512, a_q, b_w)
    kern = functools.partial(_inproj_kernel, a_q=a_q, a_kv=a_kv, b_w=b_w, chunk=chunk)
    row = lambda width: pl.BlockSpec((tm, width), lambda i: (i, 0))
    col = lambda height: pl.BlockSpec((height, tm), lambda i: (0, i))
    return pl.pallas_call(
        kern,
        grid=(t // tm,),
        in_specs=[row(d),
                  pl.BlockSpec((1, N_MOD, d), lambda i: (i // tiles_per_batch, 0, 0)),
                  pl.BlockSpec((1, d), lambda i: (0, 0)),
                  _resident((d, n))],
        out_specs=[row(a_q), row(a_kv), row(a_kv), col(b_w), row(b_w), col(b_w)],
        out_shape=[jax.ShapeDtypeStruct((t, a_q), BF16),
                   jax.ShapeDtypeStruct((t, a_kv), BF16),
                   jax.ShapeDtypeStruct((t, a_kv), BF16),
                   jax.ShapeDtypeStruct((b_w, t), BF16),
                   jax.ShapeDtypeStruct((t, b_w), BF16),
                   jax.ShapeDtypeStruct((b_w, t), BF16)],
        scratch_shapes=[pltpu.VMEM((tm, d), BF16)],
        compiler_params=_cparams(("arbitrary",)),
        name="norm1_inproj",
    )(x2, mod3, gain, w_bf)


def _attn_a_kernel(slopes_ref, sink_ref, q_ref, k_ref, v_ref, g_ref, o_ref, bias_ref, acc_ref,
                   *, tq, kw, seq, heads):
    i = pl.program_id(1)
    q0 = i * tq
    kstart = pl.multiple_of(jnp.clip(q0 - WINDOW, 0, seq - kw), WINDOW)
    group = heads // A_KV_HEADS

    @pl.when((pl.program_id(0) == 0) & (i == 0))
    def _():
        r = lax.broadcasted_iota(jnp.int32, (tq, kw), 0)
        c = lax.broadcasted_iota(jnp.int32, (tq, kw), 1)
        for case in range(3):
            dist = jnp.abs((r - c) + case * WINDOW)
            for h in range(heads):
                bias_ref[case, h] = jnp.where(dist <= WINDOW,
                                              -(slopes_ref[h] * LOG2E) * dist.astype(F32),
                                              NEG_INF)

    case = (q0 - kstart) // WINDOW
    scale = HEAD_DIM ** -0.5
    kwins = [k_ref[pl.ds(kstart, kw), kvh * HEAD_DIM:(kvh + 1) * HEAD_DIM]
             for kvh in range(A_KV_HEADS)]
    vwins = [v_ref[pl.ds(kstart, kw), kvh * HEAD_DIM:(kvh + 1) * HEAD_DIM]
             for kvh in range(A_KV_HEADS)]
    scores = [lax.dot_general(q_ref[:, h * HEAD_DIM:(h + 1) * HEAD_DIM], kwins[h // group],
                              (((1,), (1,)), ((), ())), preferred_element_type=F32)
              for h in range(heads)]
    probs = []
    inv_denoms = []
    for h in range(heads):
        s = scores[h] * (scale * LOG2E) + bias_ref[case, h]
        sink = sink_ref[h] * LOG2E
        m = jnp.maximum(jnp.max(s, axis=-1, keepdims=True), sink)
        p = jnp.exp2(s - m)
        denom = jnp.sum(p, axis=-1, keepdims=True) + jnp.exp2(sink - m)
        probs.append(p.astype(BF16))
        inv_denoms.append(1.0 / denom)
    for h in range(heads):
        acc_ref[:, h * HEAD_DIM:(h + 1) * HEAD_DIM] = inv_denoms[h] * jnp.dot(
            probs[h], vwins[h // group], preferred_element_type=F32)
    o_ref[...] = _rms(acc_ref[...], g_ref[...]).astype(BF16)


def _attn_a(slopes_a, sink, qa, ka, va, gain, *, batch, seq, tq=128):
    t, a_q = qa.shape
    a_kv = ka.shape[1]
    heads = a_q // HEAD_DIM
    kw = tq + 2 * WINDOW
    nq = seq // tq
    kern = functools.partial(_attn_a_kernel, tq=tq, kw=kw, seq=seq, heads=heads)
    smem = pl.BlockSpec(memory_space=pltpu.SMEM)
    return pl.pallas_call(
        kern,
        grid=(batch, nq),
        in_specs=[smem, smem,
                  pl.BlockSpec((tq, a_q), lambda b, i: (b * nq + i, 0)),
                  pl.BlockSpec((seq, a_kv), lambda b, i: (b, 0)),
                  pl.BlockSpec((seq, a_kv), lambda b, i: (b, 0)),
                  pl.BlockSpec((1, a_q), lambda b, i: (0, 0))],
        out_specs=pl.BlockSpec((tq, a_q), lambda b, i: (b * nq + i, 0)),
        out_shape=jax.ShapeDtypeStruct((t, a_q), BF16),
        scratch_shapes=[pltpu.VMEM((3, heads, tq, kw), F32), pltpu.VMEM((tq, a_q), F32)],
        compiler_params=_cparams(("arbitrary", "arbitrary")),
        name="attn_window_gqa",
    )(slopes_a, sink, qa, ka, va, gain)


def _attn_b_kernel(slopes_ref, lq1_ref, lk1_ref, lq2_ref, lk2_ref, g_ref, d0_ref, feat_ref,
                   qt_ref, k_ref, vt_ref, o_ref,
                   w_ref, diag_ref, s0_ref, s1_ref, m_ref, l_ref, acc_ref, *, tile, sw, seq):
    h = pl.program_id(1)
    slope2 = slopes_ref[h] * LOG2E
    lam = (jnp.exp(jnp.sum(lq1_ref[...] * lk1_ref[...], keepdims=True))
           - jnp.exp(jnp.sum(lq2_ref[...] * lk2_ref[...], keepdims=True)) + LAM_INIT)
    halves = tile // sw
    n_tiles = seq // tile

    rho = lax.broadcasted_iota(jnp.int32, (N_BIAS_ROWS, sw), 0)
    lane = lax.broadcasted_iota(jnp.int32, (N_BIAS_ROWS, sw), 1).astype(F32)
    coeff = jnp.where(rho < 3, slope2,
                      jnp.where(rho < 6, 256.0 * slope2,
                                jnp.where(rho < 9, -slope2 * lane, 0.0)))
    hi = coeff.astype(BF16).astype(F32)
    mid = (coeff - hi).astype(BF16).astype(F32)
    lo = (coeff - hi - mid).astype(BF16).astype(F32)
    level = rho % 3
    rows = jnp.where(level == 0, hi, jnp.where(level == 1, mid, lo))
    w_ref[...] = jnp.zeros_like(w_ref)
    for strip in range(2 * halves):
        cols = slice(strip * sw, (strip + 1) * sw)
        w_ref[1, B_V_DIM:B_V_DIM + N_BIAS_ROWS, cols] = rows.astype(BF16)
        w_ref[2, B_V_DIM:B_V_DIM + N_BIAS_ROWS, cols] = (-rows).astype(BF16)
    for half in range(halves):
        diag_ref[half] = -slope2 * jnp.abs(d0_ref[...] - float(half * sw))

    def norm_chunk(n, carry):
        kmax2, vmax = carry
        r0 = pl.multiple_of(n * tile, tile)
        kf = k_ref[pl.ds(r0, tile), :].astype(F32)
        kn2 = jnp.max(jnp.sum(kf * kf, axis=1, keepdims=True), axis=0, keepdims=True)
        va = jnp.abs(vt_ref[:, pl.ds(r0, tile)].astype(F32))
        vm = jnp.max(jnp.max(va, axis=1, keepdims=True), axis=0, keepdims=True)
        return jnp.maximum(kmax2, kn2), jnp.maximum(vmax, vm)

    kmax2, vmax = lax.fori_loop(0, n_tiles, norm_chunk,
                                (jnp.zeros((1, 1), F32), jnp.zeros((1, 1), F32)))

    def key_tile(i, t):
        return jnp.where(t == 0, i, jnp.where(t <= i, t - 1, t))

    def score_tile(i, t):
        j = key_tile(i, t)
        k0 = pl.multiple_of(j * tile, tile)
        widx = jnp.where(j == i, 0, jnp.where(j < i, 1, 2))
        lhs = jnp.concatenate([k_ref[pl.ds(k0, tile), :], feat_ref[...]], axis=1)
        return jnp.dot(lhs, w_ref[widx], preferred_element_type=F32)

    def scores(i, t, s_ref):
        s_ref[...] = score_tile(i, t)

    def softmax_pv(i, t, s_ref, fixed, same_tile=False):
        j = key_tile(i, t)
        k0 = pl.multiple_of(j * tile, tile)
        q0 = i * tile
        for half in range(halves):
            cst = slope2 * (k0 - q0 - half * sw).astype(F32)
            tc = 0.0 if same_tile else jnp.where(j < i, cst, -cst)
            for comp in range(2):
                cols = slice((2 * half + comp) * sw, (2 * half + comp + 1) * sw)
                t_sc = s_ref[:, cols]
                if same_tile:
                    t_sc = t_sc + diag_ref[half]
                if fixed:
                    p = jnp.exp2(t_sc - (m_ref[:, cols] - tc))
                    l_ref[:, cols] += jnp.sum(p, axis=0, keepdims=True)
                    acc_ref[:, cols] += jnp.dot(vt_ref[:, pl.ds(k0, tile)], p.astype(BF16),
                                                preferred_element_type=F32)
                else:
                    m_old = m_ref[:, cols]
                    m_new = jnp.maximum(m_old, jnp.max(t_sc, axis=0, keepdims=True) + tc)
                    alpha = jnp.exp2(m_old - m_new)
                    p = jnp.exp2(t_sc - (m_new - tc))
                    l_ref[:, cols] = alpha * l_ref[:, cols] + jnp.sum(p, axis=0, keepdims=True)
                    m_ref[:, cols] = m_new
                    pv = jnp.dot(vt_ref[:, pl.ds(k0, tile)], p.astype(BF16),
                                 preferred_element_type=F32)
                    acc_ref[:, cols] = alpha * acc_ref[:, cols] + pv

    s_refs = (s0_ref, s1_ref)

    def start_query_tile(i):
        q0 = pl.multiple_of(i * tile, tile)
        for half in range(halves):
            qcols = pl.ds(q0 + half * sw, sw)
            c0 = 2 * half * sw
            for variant in range(3):
                w_ref[variant, 0:B_QK_DIM, c0:c0 + sw] = qt_ref[0:B_QK_DIM, qcols]
                w_ref[variant, B_QK_DIM:B_V_DIM, c0 + sw:c0 + 2 * sw] = qt_ref[B_QK_DIM:, qcols]
        s = score_tile(i, 0)
        s_refs[0][...] = s
        ref = jnp.max(s, axis=0, keepdims=True)
        m_ref[...] = ref
        qf = w_ref[0, 0:B_V_DIM, :].astype(F32)
        upper = jnp.sqrt(jnp.sum(qf * qf, axis=0, keepdims=True) * kmax2) * 1.01 + 1.0
        ok = (jnp.max(upper - ref) <= FIXED_REF_MAX_EXCESS) & (
            jnp.max(vmax) <= FIXED_REF_MAX_VALUE)
        return ok.astype(jnp.int32)

    def key_tile_pipeline(i, fixed):
        scores(i, 1, s_refs[1])
        softmax_pv(i, 0, s_refs[0], fixed, same_tile=True)

        def group(n, c2):
            for u in range(GROUP):
                scores(i, GROUP * n + u + 2, s_refs[u % 2])
                softmax_pv(i, GROUP * n + u + 1, s_refs[(u + 1) % 2], fixed)
            return c2

        n_groups = (n_tiles - 2) // GROUP
        lax.fori_loop(0, n_groups, group, 0)
        for t in range(n_groups * GROUP + 1, n_tiles):
            if t + 1 < n_tiles:
                scores(i, t + 1, s_refs[(t + 1) % 2])
            softmax_pv(i, t, s_refs[t % 2], fixed)

    def q_body(i, use_fixed):
        q0 = pl.multiple_of(i * tile, tile)
        l_ref[...] = jnp.zeros_like(l_ref)
        acc_ref[...] = jnp.zeros_like(acc_ref)

        @pl.when(use_fixed == 1)
        def _():
            key_tile_pipeline(i, True)

        @pl.when(use_fixed != 1)
        def _():
            m_ref[...] = jnp.full_like(m_ref, NEG_INF)
            key_tile_pipeline(i, False)

        next_use_fixed = start_query_tile(jnp.minimum(i + 1, n_tiles - 1))

        o = acc_ref[...] / l_ref[...]
        for half in range(halves):
            c0 = 2 * half * sw
            od = (o[:, c0:c0 + sw] - lam * o[:, c0 + sw:c0 + 2 * sw]).T
            o_ref[pl.ds(q0 + half * sw, sw), :] = (
                _rms(od, g_ref[...]) * (1.0 - LAM_INIT)).astype(BF16)
        return next_use_fixed

    lax.fori_loop(0, n_tiles, q_body, start_query_tile(0))


def _attn_b(slopes_b, lq1, lk1, lq2, lk2, gain, qdt, kd, vdt, *, batch, seq, tile=512, sw=256):
    b_w, t = qdt.shape
    heads = b_w // B_V_DIM
    tile = min(tile, seq // 2)
    assert seq % (2 * tile) == 0 and tile % sw == 0
    r = lax.broadcasted_iota(jnp.int32, (tile, sw), 0)
    c = lax.broadcasted_iota(jnp.int32, (tile, sw), 1)
    d0 = (r - c).astype(F32)
    rk = jnp.arange(tile, dtype=jnp.int32)[:, None]
    fcol = jnp.arange(B_V_DIM, dtype=jnp.int32)[None, :]
    feat = jnp.where(fcol < 3, rk % 256,
                     jnp.where(fcol < 6, rk // 256, jnp.where(fcol < 9, 1, 0))).astype(BF16)
    kern = functools.partial(_attn_b_kernel, tile=tile, sw=sw, seq=seq)
    smem = pl.BlockSpec(memory_space=pltpu.SMEM)
    vec = lambda n: pl.BlockSpec((1, n), lambda b, h: (0, 0))
    tposed = pl.BlockSpec((B_V_DIM, seq), lambda b, h: (h, b))
    natural = pl.BlockSpec((seq, B_V_DIM), lambda b, h: (b, h))
    return pl.pallas_call(
        kern,
        grid=(batch, heads),
        in_specs=[smem, vec(B_QK_DIM), vec(B_QK_DIM), vec(B_QK_DIM), vec(B_QK_DIM),
                  vec(B_V_DIM),
                  pl.BlockSpec((tile, sw), lambda b, h: (0, 0)),
                  pl.BlockSpec((tile, B_V_DIM), lambda b, h: (0, 0)),
                  tposed, natural, tposed],
        out_specs=natural,
        out_shape=jax.ShapeDtypeStruct((t, b_w), BF16),
        scratch_shapes=[pltpu.VMEM((3, 2 * B_V_DIM, 2 * tile), BF16),
                        pltpu.VMEM((tile // sw, tile, sw), F32),
                        pltpu.VMEM((tile, 2 * tile), F32),
                        pltpu.VMEM((tile, 2 * tile), F32),
                        pltpu.VMEM((1, 2 * tile), F32),
                        pltpu.VMEM((1, 2 * tile), F32),
                        pltpu.VMEM((B_V_DIM, 2 * tile), F32)],
        compiler_params=_cparams(("arbitrary", "arbitrary")),
        name="attn_diff",
    )(slopes_b, lq1, lk1, lq2, lk2, gain, d0, feat, qdt, kd, vdt)


def _outproj_kernel(a_ref, b_ref, w_ref, x_ref, mod_ref, g_ref, x1_ref, h2_ref, *, a_w):
    mix = (jnp.dot(a_ref[...], w_ref[0:a_w, :], preferred_element_type=F32)
           + jnp.dot(b_ref[...], w_ref[a_w:, :], preferred_element_type=F32))
    g1 = mod_ref[0, 2:3, :]
    sh2 = mod_ref[0, 3:4, :]
    sc2 = mod_ref[0, 4:5, :]
    x1 = x_ref[...] + g1 * mix
    x1_ref[...] = x1
    h2_ref[...] = (_rms(x1, g_ref[...]) * (1.0 + sc2) + sh2).astype(BF16)


def _outproj(out_a, out_b, w_bf, x2, mod3, gain, *, seq, tm=512):
    t, d = x2.shape
    a_w = out_a.shape[1]
    b_w = out_b.shape[1]
    tiles_per_batch = seq // tm
    row = lambda width: pl.BlockSpec((tm, width), lambda i: (i, 0))
    return pl.pallas_call(
        functools.partial(_outproj_kernel, a_w=a_w),
        grid=(t // tm,),
        in_specs=[row(a_w), row(b_w), _resident((a_w + b_w, d)), row(d),
                  pl.BlockSpec((1, N_MOD, d), lambda i: (i // tiles_per_batch, 0, 0)),
                  pl.BlockSpec((1, d), lambda i: (0, 0))],
        out_specs=[row(d), row(d)],
        out_shape=[jax.ShapeDtypeStruct((t, d), F32), jax.ShapeDtypeStruct((t, d), BF16)],
        compiler_params=_cparams(("arbitrary",)),
        name="outproj_norm2",
    )(out_a, out_b, w_bf, x2, mod3, gain)


def _ffn_kernel(h_ref, wg_ref, wu_ref, wd_ref, x1_ref, mod_ref, fg_ref, o_ref):
    f = pl.program_id(1)

    @pl.when(f == 0)
    def _():
        o_ref[...] = jnp.zeros_like(o_ref)

    h = h_ref[...]
    g = jnp.dot(h, wg_ref[...], preferred_element_type=F32)
    u = jnp.dot(h, wu_ref[...], preferred_element_type=F32)
    a = (g * jax.nn.sigmoid(g) * u).astype(BF16)
    o_ref[...] += jnp.dot(a, wd_ref[...], preferred_element_type=F32)

    @pl.when(f == pl.num_programs(1) - 1)
    def _():
        g2 = mod_ref[0, 5:6, :]
        o_ref[...] = _rms(x1_ref[...] + g2 * o_ref[...], fg_ref[...])


def _ffn(h2, wg, wu, wd, x1, mod3, final_gain, *, seq, tm=512, tf=512):
    t, d = h2.shape
    ff = wg.shape[1]
    tm = min(tm, seq)
    if ff % tf:
        tf = 256
    assert ff % tf == 0 and t % tm == 0 and seq % tm == 0
    tiles_per_batch = seq // tm
    return pl.pallas_call(
        _ffn_kernel,
        grid=(t // tm, ff // tf),
        in_specs=[pl.BlockSpec((tm, d), lambda i, f: (i, 0)),
                  pl.BlockSpec((d, tf), lambda i, f: (0, f)),
                  pl.BlockSpec((d, tf), lambda i, f: (0, f)),
                  pl.BlockSpec((tf, d), lambda i, f: (f, 0)),
                  pl.BlockSpec((tm, d), lambda i, f: (i, 0)),
                  pl.BlockSpec((1, N_MOD, d), lambda i, f: (i // tiles_per_batch, 0, 0)),
                  pl.BlockSpec((1, d), lambda i, f: (0, 0))],
        out_specs=pl.BlockSpec((tm, d), lambda i, f: (i, 0)),
        out_shape=jax.ShapeDtypeStruct((t, d), F32),
        compiler_params=_cparams(("arbitrary", "arbitrary")),
        name="swiglu_ffn_final_norm",
    )(h2, wg, wu, wd, x1, mod3, final_gain)


def kernel(x, c, w_ada, b_ada, norm1_gain, w_in, a_sink, a_out_gain, diff_lq1, diff_lk1,
           diff_lq2, diff_lk2, diff_subln_gain, w_o, norm2_gain, w_gate, w_up, w_down,
           final_gain):
    batch, seq, d = x.shape
    assert w_ada.shape[0] == 1, "single-layer block"
    a_w = d // 2
    b_w = d - a_w
    a_heads = a_w // HEAD_DIM
    b_heads = b_w // B_V_DIM
    a_kv = A_KV_HEADS * HEAD_DIM
    n_heads = a_heads + b_heads
    slopes = 2.0 ** (-8.0 * jnp.arange(1, n_heads + 1, dtype=F32) / n_heads)

    rows = 8
    c_pad = jnp.zeros((rows, d), F32).at[:batch].set(c)
    mod = _ada(c_pad, w_ada[0], b_ada[0][None, :])[:batch]
    mod3 = mod.reshape(batch, N_MOD, d)

    x2 = x.reshape(batch * seq, d)
    qa, ka, va, qdt, kd, vdt = _inproj(
        x2, mod3, norm1_gain[0][None, :], w_in[0].astype(BF16),
        seq=seq, a_q=a_w, a_kv=a_kv, b_w=b_w)

    out_a = _attn_a(slopes[:a_heads], a_sink[0].astype(F32), qa, ka, va,
                    a_out_gain[0][None, :], batch=batch, seq=seq)
    out_b = _attn_b(slopes[a_heads:], diff_lq1[0][None, :], diff_lk1[0][None, :],
                    diff_lq2[0][None, :], diff_lk2[0][None, :], diff_subln_gain[0][None, :],
                    qdt, kd, vdt, batch=batch, seq=seq)

    x1, h2 = _outproj(out_a, out_b, w_o[0].astype(BF16), x2, mod3, norm2_gain[0][None, :],
                      seq=seq)
    out = _ffn(h2, w_gate[0].astype(BF16), w_up[0].astype(BF16), w_down[0].astype(BF16),
               x1, mod3, final_gain[None, :], seq=seq)
    return out.reshape(batch, seq, d)
```

```python
import functools
import math

import jax
import jax.numpy as jnp
from jax import lax
from jax.experimental import pallas as pl
from jax.experimental.pallas import tpu as pltpu

HEAD_DIM = 128
A_KV_HEADS = 2
WINDOW = 128
B_QK_DIM = 64
B_V_DIM = 2 * B_QK_DIM
N_MOD = 6
EPS = 1e-6
NEG_INF = -1e30
LAM_INIT = 0.8 - 0.6 * math.exp(-0.3 * 0)
LOG2E = math.log2(math.e)
GROUP = 6
FIXED_REF_MAX_EXCESS = 64.0
FIXED_REF_MAX_VALUE = 2.0 ** 30
ONES_ROWS = 16
N_BIAS_ROWS = 16

V7X_VMEM_LIMIT_BYTES = 56 * 1024 * 1024

BF16 = jnp.bfloat16
F32 = jnp.float32


def _cparams(semantics):
    return pltpu.CompilerParams(dimension_semantics=semantics,
                                vmem_limit_bytes=V7X_VMEM_LIMIT_BYTES)


def _rms(x, gain):
    return x * lax.rsqrt(jnp.mean(x * x, axis=-1, keepdims=True) + EPS) * gain


def _resident(shape):
    return pl.BlockSpec(shape, lambda *_: (0,) * len(shape), pipeline_mode=pl.Buffered(1))


def _ada_kernel(c_ref, w_ref, b_ref, o_ref):
    c = c_ref[...]
    sc = (c * jax.nn.sigmoid(c)).astype(BF16)
    o_ref[...] = jnp.dot(sc, w_ref[...].astype(BF16), preferred_element_type=F32) + b_ref[...]


def _ada(c_pad, w, b, tn=1024):
    rows, d = c_pad.shape
    n = w.shape[1]
    return pl.pallas_call(
        _ada_kernel,
        grid=(n // tn,),
        in_specs=[pl.BlockSpec((rows, d), lambda j: (0, 0)),
                  pl.BlockSpec((d, tn), lambda j: (0, j)),
                  pl.BlockSpec((1, tn), lambda j: (0, j))],
        out_specs=pl.BlockSpec((rows, tn), lambda j: (0, j)),
        out_shape=jax.ShapeDtypeStruct((rows, n), F32),
        compiler_params=_cparams(("arbitrary",)),
        name="ada_mod",
    )(c_pad, w, b)


def _inproj_kernel(x_ref, mod_ref, g_ref, w_ref,
                   qa_ref, ka_ref, va_ref, qdt_ref, kd_ref, vdt_ref, h_ref,
                   *, a_q, a_kv, b_w, chunk):
    x = x_ref[...]
    sh1 = mod_ref[0, 0:1, :]
    sc1 = mod_ref[0, 1:2, :]
    h_ref[...] = (_rms(x, g_ref[...]) * (1.0 + sc1) + sh1).astype(BF16)

    def proj(c0, width):
        return jnp.dot(h_ref[...], w_ref[:, c0:c0 + width], preferred_element_type=F32)

    o1 = a_q
    o2 = o1 + a_kv
    o3 = o2 + a_kv
    o4 = o3 + b_w
    o5 = o4 + b_w
    for c in range(0, a_q, chunk):
        qa_ref[:, c:c + chunk] = proj(c, chunk).astype(BF16)
    ka_ref[...] = proj(o1, a_kv).astype(BF16)
    va_ref[...] = proj(o2, a_kv).astype(BF16)
    for c in range(0, b_w, chunk):
        kd_ref[:, c:c + chunk] = proj(o4 + c, chunk).astype(BF16)
    qscale = B_QK_DIM ** -0.5 * LOG2E
    for c in range(0, b_w, chunk):
        q = proj(o3 + c, chunk) * qscale
        v = proj(o5 + c, chunk)
        for hc in range(0, chunk, B_V_DIM):
            qdt_ref[c + hc:c + hc + B_V_DIM, :] = q[:, hc:hc + B_V_DIM].T.astype(BF16)
            vdt_ref[c + hc:c + hc + B_V_DIM, :] = v[:, hc:hc + B_V_DIM].T.astype(BF16)


def _inproj(x2, mod3, gain, w_bf, *, seq, a_q, a_kv, b_w, tm=512):
    t, d = x2.shape
    n = w_bf.shape[1]
    tiles_per_batch = seq // tm
    chunk = min(512, a_q, b_w)
    kern = functools.partial(_inproj_kernel, a_q=a_q, a_kv=a_kv, b_w=b_w, chunk=chunk)
    row = lambda width: pl.BlockSpec((tm, width), lambda i: (i, 0))
    col = lambda height: pl.BlockSpec((height, tm), lambda i: (0, i))
    return pl.pallas_call(
        kern,
        grid=(t // tm,),
        in_specs=[row(d),
                  pl.BlockSpec((1, N_MOD, d), lambda i: (i // tiles_per_batch, 0, 0)),
                  pl.BlockSpec((1, d), lambda i: (0, 0)),
                  _resident((d, n))],
        out_specs=[row(a_q), row(a_kv), row(a_kv), col(b_w), row(b_w), col(b_w)],
        out_shape=[jax.ShapeDtypeStruct((t, a_q), BF16),
                   jax.ShapeDtypeStruct((t, a_kv), BF16),
                   jax.ShapeDtypeStruct((t, a_kv), BF16),
                   jax.ShapeDtypeStruct((b_w, t), BF16),
                   jax.ShapeDtypeStruct((t, b_w), BF16),
                   jax.ShapeDtypeStruct((b_w, t), BF16)],
        scratch_shapes=[pltpu.VMEM((tm, d), BF16)],
        compiler_params=_cparams(("arbitrary",)),
        name="norm1_inproj",
    )(x2, mod3, gain, w_bf)


def _attn_a_kernel(slopes_ref, sink_ref, q_ref, k_ref, v_ref, g_ref, o_ref, bias_ref, acc_ref,
                   *, tq, kw, seq, heads):
    i = pl.program_id(1)
    q0 = i * tq
    kstart = pl.multiple_of(jnp.clip(q0 - WINDOW, 0, seq - kw), WINDOW)
    group = heads // A_KV_HEADS

    @pl.when((pl.program_id(0) == 0) & (i == 0))
    def _():
        r = lax.broadcasted_iota(jnp.int32, (tq, kw), 0)
        c = lax.broadcasted_iota(jnp.int32, (tq, kw), 1)
        for case in range(3):
            dist = jnp.abs((r - c) + case * WINDOW)
            for h in range(heads):
                bias_ref[case, h] = jnp.where(dist <= WINDOW,
                                              -(slopes_ref[h] * LOG2E) * dist.astype(F32),
                                              NEG_INF)

    case = (q0 - kstart) // WINDOW
    scale = HEAD_DIM ** -0.5
    kwins = [k_ref[pl.ds(kstart, kw), kvh * HEAD_DIM:(kvh + 1) * HEAD_DIM]
             for kvh in range(A_KV_HEADS)]
    vwins = [v_ref[pl.ds(kstart, kw), kvh * HEAD_DIM:(kvh + 1) * HEAD_DIM]
             for kvh in range(A_KV_HEADS)]
    scores = [lax.dot_general(q_ref[:, h * HEAD_DIM:(h + 1) * HEAD_DIM], kwins[h // group],
                              (((1,), (1,)), ((), ())), preferred_element_type=F32)
              for h in range(heads)]
    probs = []
    inv_denoms = []
    for h in range(heads):
        s = scores[h] * (scale * LOG2E) + bias_ref[case, h]
        sink = sink_ref[h] * LOG2E
        m = jnp.maximum(jnp.max(s, axis=-1, keepdims=True), sink)
        p = jnp.exp2(s - m)
        denom = jnp.sum(p, axis=-1, keepdims=True) + jnp.exp2(sink - m)
        probs.append(p.astype(BF16))
        inv_denoms.append(1.0 / denom)
    for h in range(heads):
        acc_ref[:, h * HEAD_DIM:(h + 1) * HEAD_DIM] = inv_denoms[h] * jnp.dot(
            probs[h], vwins[h // group], preferred_element_type=F32)
    o_ref[...] = _rms(acc_ref[...], g_ref[...]).astype(BF16)


def _attn_a(slopes_a, sink, qa, ka, va, gain, *, batch, seq, tq=128):
    t, a_q = qa.shape
    a_kv = ka.shape[1]
    heads = a_q // HEAD_DIM
    kw = tq + 2 * WINDOW
    nq = seq // tq
    kern = functools.partial(_attn_a_kernel, tq=tq, kw=kw, seq=seq, heads=heads)
    smem = pl.BlockSpec(memory_space=pltpu.SMEM)
    return pl.pallas_call(
        kern,
        grid=(batch, nq),
        in_specs=[smem, smem,
                  pl.BlockSpec((tq, a_q), lambda b, i: (b * nq + i, 0)),
                  pl.BlockSpec((seq, a_kv), lambda b, i: (b, 0)),
                  pl.BlockSpec((seq, a_kv), lambda b, i: (b, 0)),
                  pl.BlockSpec((1, a_q), lambda b, i: (0, 0))],
        out_specs=pl.BlockSpec((tq, a_q), lambda b, i: (b * nq + i, 0)),
        out_shape=jax.ShapeDtypeStruct((t, a_q), BF16),
        scratch_shapes=[pltpu.VMEM((3, heads, tq, kw), F32), pltpu.VMEM((tq, a_q), F32)],
        compiler_params=_cparams(("arbitrary", "arbitrary")),
        name="attn_window_gqa",
    )(slopes_a, sink, qa, ka, va, gain)


def _attn_b_kernel(slopes_ref, lq1_ref, lk1_ref, lq2_ref, lk2_ref, g_ref, d0_ref, feat_ref,
                   qt_ref, k_ref, vt_ref, o_ref,
                   w_ref, diag_ref, s0_ref, s1_ref, m_ref, l_ref, acc_ref, vt1_ref, acc1_ref,
                   *, tile, sw, seq):
    h = pl.program_id(1)
    slope2 = slopes_ref[h] * LOG2E
    lam = (jnp.exp(jnp.sum(lq1_ref[...] * lk1_ref[...], keepdims=True))
           - jnp.exp(jnp.sum(lq2_ref[...] * lk2_ref[...], keepdims=True)) + LAM_INIT)
    halves = tile // sw
    n_tiles = seq // tile

    rho = lax.broadcasted_iota(jnp.int32, (N_BIAS_ROWS, sw), 0)
    lane = lax.broadcasted_iota(jnp.int32, (N_BIAS_ROWS, sw), 1).astype(F32)
    coeff = jnp.where(rho < 3, slope2,
                      jnp.where(rho < 6, 256.0 * slope2,
                                jnp.where(rho < 9, -slope2 * lane, 0.0)))
    hi = coeff.astype(BF16).astype(F32)
    mid = (coeff - hi).astype(BF16).astype(F32)
    lo = (coeff - hi - mid).astype(BF16).astype(F32)
    level = rho % 3
    rows = jnp.where(level == 0, hi, jnp.where(level == 1, mid, lo))
    w_ref[...] = jnp.zeros_like(w_ref)
    for strip in range(2 * halves):
        cols = slice(strip * sw, (strip + 1) * sw)
        w_ref[1, B_V_DIM:B_V_DIM + N_BIAS_ROWS, cols] = rows.astype(BF16)
        w_ref[2, B_V_DIM:B_V_DIM + N_BIAS_ROWS, cols] = (-rows).astype(BF16)
    for half in range(halves):
        diag_ref[half] = -slope2 * jnp.abs(d0_ref[...] - float(half * sw))

    def norm_chunk(n, carry):
        kmax2, vmax = carry
        r0 = pl.multiple_of(n * tile, tile)
        kf = k_ref[pl.ds(r0, tile), :].astype(F32)
        kn2 = jnp.max(jnp.sum(kf * kf, axis=1, keepdims=True), axis=0, keepdims=True)
        va = jnp.abs(vt_ref[:, pl.ds(r0, tile)].astype(F32))
        vm = jnp.max(jnp.max(va, axis=1, keepdims=True), axis=0, keepdims=True)
        return jnp.maximum(kmax2, kn2), jnp.maximum(vmax, vm)

    kmax2, vmax = lax.fori_loop(0, n_tiles, norm_chunk,
                                (jnp.zeros((1, 1), F32), jnp.zeros((1, 1), F32)))
    vt1_ref[0:B_V_DIM, :] = vt_ref[...]
    extra = lax.broadcasted_iota(jnp.int32, (ONES_ROWS, seq), 0)
    vt1_ref[B_V_DIM:, :] = jnp.where(extra == 0, 1.0, 0.0).astype(BF16)

    def key_tile(i, t):
        return jnp.where(t == 0, i, jnp.where(t <= i, t - 1, t))

    def score_tile(i, t):
        j = key_tile(i, t)
        k0 = pl.multiple_of(j * tile, tile)
        widx = jnp.where(j == i, 0, jnp.where(j < i, 1, 2))
        lhs = jnp.concatenate([k_ref[pl.ds(k0, tile), :], feat_ref[...]], axis=1)
        return jnp.dot(lhs, w_ref[widx], preferred_element_type=F32)

    def scores(i, t, s_ref):
        s_ref[...] = score_tile(i, t)

    def softmax_pv(i, t, s_ref, fixed, same_tile=False):
        j = key_tile(i, t)
        k0 = pl.multiple_of(j * tile, tile)
        q0 = i * tile
        for half in range(halves):
            cst = slope2 * (k0 - q0 - half * sw).astype(F32)
            tc = 0.0 if same_tile else jnp.where(j < i, cst, -cst)
            for comp in range(2):
                cols = slice((2 * half + comp) * sw, (2 * half + comp + 1) * sw)
                t_sc = s_ref[:, cols]
                if same_tile:
                    t_sc = t_sc + diag_ref[half]
                if fixed:
                    p = jnp.exp2(t_sc - (m_ref[:, cols] - tc))
                    acc1_ref[:, cols] += jnp.dot(vt1_ref[:, pl.ds(k0, tile)], p.astype(BF16),
                                                 preferred_element_type=F32)
                else:
                    m_old = m_ref[:, cols]
                    m_new = jnp.maximum(m_old, jnp.max(t_sc, axis=0, keepdims=True) + tc)
                    alpha = jnp.exp2(m_old - m_new)
                    p = jnp.exp2(t_sc - (m_new - tc))
                    l_ref[:, cols] = alpha * l_ref[:, cols] + jnp.sum(p, axis=0, keepdims=True)
                    m_ref[:, cols] = m_new
                    pv = jnp.dot(vt_ref[:, pl.ds(k0, tile)], p.astype(BF16),
                                 preferred_element_type=F32)
                    acc_ref[:, cols] = alpha * acc_ref[:, cols] + pv

    s_refs = (s0_ref, s1_ref)

    def start_query_tile(i):
        q0 = pl.multiple_of(i * tile, tile)
        for half in range(halves):
            qcols = pl.ds(q0 + half * sw, sw)
            c0 = 2 * half * sw
            for variant in range(3):
                w_ref[variant, 0:B_QK_DIM, c0:c0 + sw] = qt_ref[0:B_QK_DIM, qcols]
                w_ref[variant, B_QK_DIM:B_V_DIM, c0 + sw:c0 + 2 * sw] = qt_ref[B_QK_DIM:, qcols]
        s = score_tile(i, 0)
        s_refs[0][...] = s
        ref = jnp.max(s, axis=0, keepdims=True)
        m_ref[...] = ref
        qf = w_ref[0, 0:B_V_DIM, :].astype(F32)
        upper = jnp.sqrt(jnp.sum(qf * qf, axis=0, keepdims=True) * kmax2) * 1.01 + 1.0
        ok = (jnp.max(upper - ref) <= FIXED_REF_MAX_EXCESS) & (
            jnp.max(vmax) <= FIXED_REF_MAX_VALUE)
        return ok.astype(jnp.int32)

    def key_tile_pipeline(i, fixed):
        scores(i, 1, s_refs[1])
        softmax_pv(i, 0, s_refs[0], fixed, same_tile=True)

        def group(n, c2):
            for u in range(GROUP):
                scores(i, GROUP * n + u + 2, s_refs[u % 2])
                softmax_pv(i, GROUP * n + u + 1, s_refs[(u + 1) % 2], fixed)
            return c2

        n_groups = (n_tiles - 2) // GROUP
        lax.fori_loop(0, n_groups, group, 0)
        for t in range(n_groups * GROUP + 1, n_tiles):
            if t + 1 < n_tiles:
                scores(i, t + 1, s_refs[(t + 1) % 2])
            softmax_pv(i, t, s_refs[t % 2], fixed)

    def q_body(i, use_fixed):
        q0 = pl.multiple_of(i * tile, tile)
        l_ref[...] = jnp.zeros_like(l_ref)
        acc_ref[...] = jnp.zeros_like(acc_ref)

        @pl.when(use_fixed == 1)
        def _():
            acc1_ref[...] = jnp.zeros_like(acc1_ref)
            key_tile_pipeline(i, True)
            acc_ref[...] = acc1_ref[0:B_V_DIM, :]
            l_ref[...] = acc1_ref[B_V_DIM:B_V_DIM + 1, :]

        @pl.when(use_fixed != 1)
        def _():
            m_ref[...] = jnp.full_like(m_ref, NEG_INF)
            key_tile_pipeline(i, False)

        next_use_fixed = start_query_tile(jnp.minimum(i + 1, n_tiles - 1))

        o = acc_ref[...] / l_ref[...]
        for half in range(halves):
            c0 = 2 * half * sw
            od = (o[:, c0:c0 + sw] - lam * o[:, c0 + sw:c0 + 2 * sw]).T
            o_ref[pl.ds(q0 + half * sw, sw), :] = (
                _rms(od, g_ref[...]) * (1.0 - LAM_INIT)).astype(BF16)
        return next_use_fixed

    lax.fori_loop(0, n_tiles, q_body, start_query_tile(0))


def _attn_b(slopes_b, lq1, lk1, lq2, lk2, gain, qdt, kd, vdt, *, batch, seq, tile=512, sw=256):
    b_w, t = qdt.shape
    heads = b_w // B_V_DIM
    tile = min(tile, seq // 2)
    assert seq % (2 * tile) == 0 and tile % sw == 0
    r = lax.broadcasted_iota(jnp.int32, (tile, sw), 0)
    c = lax.broadcasted_iota(jnp.int32, (tile, sw), 1)
    d0 = (r - c).astype(F32)
    rk = jnp.arange(tile, dtype=jnp.int32)[:, None]
    fcol = jnp.arange(B_V_DIM, dtype=jnp.int32)[None, :]
    feat = jnp.where(fcol < 3, rk % 256,
                     jnp.where(fcol < 6, rk // 256, jnp.where(fcol < 9, 1, 0))).astype(BF16)
    kern = functools.partial(_attn_b_kernel, tile=tile, sw=sw, seq=seq)
    smem = pl.BlockSpec(memory_space=pltpu.SMEM)
    vec = lambda n: pl.BlockSpec((1, n), lambda b, h: (0, 0))
    tposed = pl.BlockSpec((B_V_DIM, seq), lambda b, h: (h, b))
    natural = pl.BlockSpec((seq, B_V_DIM), lambda b, h: (b, h))
    return pl.pallas_call(
        kern,
        grid=(batch, heads),
        in_specs=[smem, vec(B_QK_DIM), vec(B_QK_DIM), vec(B_QK_DIM), vec(B_QK_DIM),
                  vec(B_V_DIM),
                  pl.BlockSpec((tile, sw), lambda b, h: (0, 0)),
                  pl.BlockSpec((tile, B_V_DIM), lambda b, h: (0, 0)),
                  tposed, natural, tposed],
        out_specs=natural,
        out_shape=jax.ShapeDtypeStruct((t, b_w), BF16),
        scratch_shapes=[pltpu.VMEM((3, 2 * B_V_DIM, 2 * tile), BF16),
                        pltpu.VMEM((tile // sw, tile, sw), F32),
                        pltpu.VMEM((tile, 2 * tile), F32),
                        pltpu.VMEM((tile, 2 * tile), F32),
                        pltpu.VMEM((1, 2 * tile), F32),
                        pltpu.VMEM((1, 2 * tile), F32),
                        pltpu.VMEM((B_V_DIM, 2 * tile), F32),
                        pltpu.VMEM((B_V_DIM + ONES_ROWS, seq), BF16),
                        pltpu.VMEM((B_V_DIM + ONES_ROWS, 2 * tile), F32)],
        compiler_params=_cparams(("arbitrary", "arbitrary")),
        name="attn_diff",
    )(slopes_b, lq1, lk1, lq2, lk2, gain, d0, feat, qdt, kd, vdt)


def _outproj_kernel(a_ref, b_ref, w_ref, x_ref, mod_ref, g_ref, x1_ref, h2_ref, *, a_w):
    mix = (jnp.dot(a_ref[...], w_ref[0:a_w, :], preferred_element_type=F32)
           + jnp.dot(b_ref[...], w_ref[a_w:, :], preferred_element_type=F32))
    g1 = mod_ref[0, 2:3, :]
    sh2 = mod_ref[0, 3:4, :]
    sc2 = mod_ref[0, 4:5, :]
    x1 = x_ref[...] + g1 * mix
    x1_ref[...] = x1
    h2_ref[...] = (_rms(x1, g_ref[...]) * (1.0 + sc2) + sh2).astype(BF16)


def _outproj(out_a, out_b, w_bf, x2, mod3, gain, *, seq, tm=512):
    t, d = x2.shape
    a_w = out_a.shape[1]
    b_w = out_b.shape[1]
    tiles_per_batch = seq // tm
    row = lambda width: pl.BlockSpec((tm, width), lambda i: (i, 0))
    return pl.pallas_call(
        functools.partial(_outproj_kernel, a_w=a_w),
        grid=(t // tm,),
        in_specs=[row(a_w), row(b_w), _resident((a_w + b_w, d)), row(d),
                  pl.BlockSpec((1, N_MOD, d), lambda i: (i // tiles_per_batch, 0, 0)),
                  pl.BlockSpec((1, d), lambda i: (0, 0))],
        out_specs=[row(d), row(d)],
        out_shape=[jax.ShapeDtypeStruct((t, d), F32), jax.ShapeDtypeStruct((t, d), BF16)],
        compiler_params=_cparams(("arbitrary",)),
        name="outproj_norm2",
    )(out_a, out_b, w_bf, x2, mod3, gain)


def _ffn_kernel(h_ref, wg_ref, wu_ref, wd_ref, x1_ref, mod_ref, fg_ref, o_ref):
    f = pl.program_id(1)

    @pl.when(f == 0)
    def _():
        o_ref[...] = jnp.zeros_like(o_ref)

    h = h_ref[...]
    g = jnp.dot(h, wg_ref[...], preferred_element_type=F32)
    u = jnp.dot(h, wu_ref[...], preferred_element_type=F32)
    a = (g * jax.nn.sigmoid(g) * u).astype(BF16)
    o_ref[...] += jnp.dot(a, wd_ref[...], preferred_element_type=F32)

    @pl.when(f == pl.num_programs(1) - 1)
    def _():
        g2 = mod_ref[0, 5:6, :]
        o_ref[...] = _rms(x1_ref[...] + g2 * o_ref[...], fg_ref[...])


def _ffn(h2, wg, wu, wd, x1, mod3, final_gain, *, seq, tm=512, tf=512):
    t, d = h2.shape
    ff = wg.shape[1]
    tm = min(tm, seq)
    if ff % tf:
        tf = 256
    assert ff % tf == 0 and t % tm == 0 and seq % tm == 0
    tiles_per_batch = seq // tm
    return pl.pallas_call(
        _ffn_kernel,
        grid=(t // tm, ff // tf),
        in_specs=[pl.BlockSpec((tm, d), lambda i, f: (i, 0)),
                  pl.BlockSpec((d, tf), lambda i, f: (0, f)),
                  pl.BlockSpec((d, tf), lambda i, f: (0, f)),
                  pl.BlockSpec((tf, d), lambda i, f: (f, 0)),
                  pl.BlockSpec((tm, d), lambda i, f: (i, 0)),
                  pl.BlockSpec((1, N_MOD, d), lambda i, f: (i // tiles_per_batch, 0, 0)),
                  pl.BlockSpec((1, d), lambda i, f: (0, 0))],
        out_specs=pl.BlockSpec((tm, d), lambda i, f: (i, 0)),
        out_shape=jax.ShapeDtypeStruct((t, d), F32),
        compiler_params=_cparams(("arbitrary", "arbitrary")),
        name="swiglu_ffn_final_norm",
    )(h2, wg, wu, wd, x1, mod3, final_gain)


def kernel(x, c, w_ada, b_ada, norm1_gain, w_in, a_sink, a_out_gain, diff_lq1, diff_lk1,
           diff_lq2, diff_lk2, diff_subln_gain, w_o, norm2_gain, w_gate, w_up, w_down,
           final_gain):
    batch, seq, d = x.shape
    assert w_ada.shape[0] == 1, "single-layer block"
    a_w = d // 2
    b_w = d - a_w
    a_heads = a_w // HEAD_DIM
    b_heads = b_w // B_V_DIM
    a_kv = A_KV_HEADS * HEAD_DIM
    n_heads = a_heads + b_heads
    slopes = 2.0 ** (-8.0 * jnp.arange(1, n_heads + 1, dtype=F32) / n_heads)

    rows = 8
    c_pad = jnp.zeros((rows, d), F32).at[:batch].set(c)
    mod = _ada(c_pad, w_ada[0], b_ada[0][None, :])[:batch]
    mod3 = mod.reshape(batch, N_MOD, d)

    x2 = x.reshape(batch * seq, d)
    qa, ka, va, qdt, kd, vdt = _inproj(
        x2, mod3, norm1_gain[0][None, :], w_in[0].astype(BF16),
        seq=seq, a_q=a_w, a_kv=a_kv, b_w=b_w)

    out_a = _attn_a(slopes[:a_heads], a_sink[0].astype(F32), qa, ka, va,
                    a_out_gain[0][None, :], batch=batch, seq=seq)
    out_b = _attn_b(slopes[a_heads:], diff_lq1[0][None, :], diff_lk1[0][None, :],
                    diff_lq2[0][None, :], diff_lk2[0][None, :], diff_subln_gain[0][None, :],
                    qdt, kd, vdt, batch=batch, seq=seq)

    x1, h2 = _outproj(out_a, out_b, w_o[0].astype(BF16), x2, mod3, norm2_gain[0][None, :],
                      seq=seq)
    out = _ffn(h2, w_gate[0].astype(BF16), w_up[0].astype(BF16), w_down[0].astype(BF16),
               x1, mod3, final_gain[None, :], seq=seq)
    return out.reshape(batch, seq, d)
```

```python
import functools
import math

import jax
import jax.numpy as jnp
from jax import lax
from jax.experimental import pallas as pl
from jax.experimental.pallas import tpu as pltpu

HEAD_DIM = 128
A_KV_HEADS = 2
WINDOW = 128
B_QK_DIM = 64
B_V_DIM = 2 * B_QK_DIM
N_MOD = 6
EPS = 1e-6
NEG_INF = -1e30
LAM_INIT = 0.8 - 0.6 * math.exp(-0.3 * 0)
LOG2E = math.log2(math.e)
GROUP = 6
FIXED_REF_MAX_EXCESS = 64.0
FIXED_REF_MAX_VALUE = 2.0 ** 30
ONES_ROWS = 16
N_BIAS_ROWS = 16

V7X_VMEM_LIMIT_BYTES = 56 * 1024 * 1024

BF16 = jnp.bfloat16
F32 = jnp.float32


def _cparams(semantics):
    return pltpu.CompilerParams(dimension_semantics=semantics,
                                vmem_limit_bytes=V7X_VMEM_LIMIT_BYTES)


def _rms(x, gain):
    return x * lax.rsqrt(jnp.mean(x * x, axis=-1, keepdims=True) + EPS) * gain


def _resident(shape):
    return pl.BlockSpec(shape, lambda *_: (0,) * len(shape), pipeline_mode=pl.Buffered(1))


def _ada_kernel(c_ref, w_ref, b_ref, o_ref):
    c = c_ref[...]
    sc = (c * jax.nn.sigmoid(c)).astype(BF16)
    o_ref[...] = jnp.dot(sc, w_ref[...].astype(BF16), preferred_element_type=F32) + b_ref[...]


def _ada(c_pad, w, b, tn=1024):
    rows, d = c_pad.shape
    n = w.shape[1]
    return pl.pallas_call(
        _ada_kernel,
        grid=(n // tn,),
        in_specs=[pl.BlockSpec((rows, d), lambda j: (0, 0)),
                  pl.BlockSpec((d, tn), lambda j: (0, j)),
                  pl.BlockSpec((1, tn), lambda j: (0, j))],
        out_specs=pl.BlockSpec((rows, tn), lambda j: (0, j)),
        out_shape=jax.ShapeDtypeStruct((rows, n), F32),
        compiler_params=_cparams(("arbitrary",)),
        name="ada_mod",
    )(c_pad, w, b)


def _inproj_kernel(x_ref, mod_ref, g_ref, w_ref,
                   qa_ref, ka_ref, va_ref, qdt_ref, kd_ref, vdt_ref, h_ref,
                   *, a_q, a_kv, b_w, chunk):
    x = x_ref[...]
    sh1 = mod_ref[0, 0:1, :]
    sc1 = mod_ref[0, 1:2, :]
    h_ref[...] = (_rms(x, g_ref[...]) * (1.0 + sc1) + sh1).astype(BF16)

    def proj(c0, width):
        return jnp.dot(h_ref[...], w_ref[:, c0:c0 + width], preferred_element_type=F32)

    o1 = a_q
    o2 = o1 + a_kv
    o3 = o2 + a_kv
    o4 = o3 + b_w
    o5 = o4 + b_w
    for c in range(0, a_q, chunk):
        qa_ref[:, c:c + chunk] = proj(c, chunk).astype(BF16)
    ka_ref[...] = proj(o1, a_kv).astype(BF16)
    va_ref[...] = proj(o2, a_kv).astype(BF16)
    for c in range(0, b_w, chunk):
        kd_ref[:, c:c + chunk] = proj(o4 + c, chunk).astype(BF16)
    qscale = B_QK_DIM ** -0.5 * LOG2E
    for c in range(0, b_w, chunk):
        q = proj(o3 + c, chunk) * qscale
        v = proj(o5 + c, chunk)
        for hc in range(0, chunk, B_V_DIM):
            qdt_ref[c + hc:c + hc + B_V_DIM, :] = q[:, hc:hc + B_V_DIM].T.astype(BF16)
            vdt_ref[c + hc:c + hc + B_V_DIM, :] = v[:, hc:hc + B_V_DIM].T.astype(BF16)


def _inproj(x2, mod3, gain, w_bf, *, seq, a_q, a_kv, b_w, tm=512):
    t, d = x2.shape
    n = w_bf.shape[1]
    tiles_per_batch = seq // tm
    chunk = min(512, a_q, b_w)
    kern = functools.partial(_inproj_kernel, a_q=a_q, a_kv=a_kv, b_w=b_w, chunk=chunk)
    row = lambda width: pl.BlockSpec((tm, width), lambda i: (i, 0))
    col = lambda height: pl.BlockSpec((height, tm), lambda i: (0, i))
    return pl.pallas_call(
        kern,
        grid=(t // tm,),
        in_specs=[row(d),
                  pl.BlockSpec((1, N_MOD, d), lambda i: (i // tiles_per_batch, 0, 0)),
                  pl.BlockSpec((1, d), lambda i: (0, 0)),
                  _resident((d, n))],
        out_specs=[row(a_q), row(a_kv), row(a_kv), col(b_w), row(b_w), col(b_w)],
        out_shape=[jax.ShapeDtypeStruct((t, a_q), BF16),
                   jax.ShapeDtypeStruct((t, a_kv), BF16),
                   jax.ShapeDtypeStruct((t, a_kv), BF16),
                   jax.ShapeDtypeStruct((b_w, t), BF16),
                   jax.ShapeDtypeStruct((t, b_w), BF16),
                   jax.ShapeDtypeStruct((b_w, t), BF16)],
        scratch_shapes=[pltpu.VMEM((tm, d), BF16)],
        compiler_params=_cparams(("arbitrary",)),
        name="norm1_inproj",
    )(x2, mod3, gain, w_bf)


def _attn_a_kernel(slopes_ref, sink_ref, q_ref, k_ref, v_ref, g_ref, o_ref, bias_ref, acc_ref,
                   *, tq, kw, seq, heads):
    i = pl.program_id(1)
    q0 = i * tq
    kstart = pl.multiple_of(jnp.clip(q0 - WINDOW, 0, seq - kw), WINDOW)
    group = heads // A_KV_HEADS

    @pl.when((pl.program_id(0) == 0) & (i == 0))
    def _():
        r = lax.broadcasted_iota(jnp.int32, (tq, kw), 0)
        c = lax.broadcasted_iota(jnp.int32, (tq, kw), 1)
        for case in range(3):
            dist = jnp.abs((r - c) + case * WINDOW)
            for h in range(heads):
                bias_ref[case, h] = jnp.where(dist <= WINDOW,
                                              -(slopes_ref[h] * LOG2E) * dist.astype(F32),
                                              NEG_INF)

    case = (q0 - kstart) // WINDOW
    scale = HEAD_DIM ** -0.5
    kwins = [k_ref[pl.ds(kstart, kw), kvh * HEAD_DIM:(kvh + 1) * HEAD_DIM]
             for kvh in range(A_KV_HEADS)]
    vwins = [v_ref[pl.ds(kstart, kw), kvh * HEAD_DIM:(kvh + 1) * HEAD_DIM]
             for kvh in range(A_KV_HEADS)]
    scores = [lax.dot_general(q_ref[:, h * HEAD_DIM:(h + 1) * HEAD_DIM], kwins[h // group],
                              (((1,), (1,)), ((), ())), preferred_element_type=F32)
              for h in range(heads)]
    probs = []
    inv_denoms = []
    for h in range(heads):
        s = scores[h] * (scale * LOG2E) + bias_ref[case, h]
        sink = sink_ref[h] * LOG2E
        m = jnp.maximum(jnp.max(s, axis=-1, keepdims=True), sink)
        p = jnp.exp2(s - m)
        denom = jnp.sum(p, axis=-1, keepdims=True) + jnp.exp2(sink - m)
        probs.append(p.astype(BF16))
        inv_denoms.append(1.0 / denom)
    for h in range(heads):
        acc_ref[:, h * HEAD_DIM:(h + 1) * HEAD_DIM] = inv_denoms[h] * jnp.dot(
            probs[h], vwins[h // group], preferred_element_type=F32)
    o_ref[...] = _rms(acc_ref[...], g_ref[...]).astype(BF16)


def _attn_a(slopes_a, sink, qa, ka, va, gain, *, batch, seq, tq=128):
    t, a_q = qa.shape
    a_kv = ka.shape[1]
    heads = a_q // HEAD_DIM
    kw = tq + 2 * WINDOW
    nq = seq // tq
    kern = functools.partial(_attn_a_kernel, tq=tq, kw=kw, seq=seq, heads=heads)
    smem = pl.BlockSpec(memory_space=pltpu.SMEM)
    return pl.pallas_call(
        kern,
        grid=(batch, nq),
        in_specs=[smem, smem,
                  pl.BlockSpec((tq, a_q), lambda b, i: (b * nq + i, 0)),
                  pl.BlockSpec((seq, a_kv), lambda b, i: (b, 0)),
                  pl.BlockSpec((seq, a_kv), lambda b, i: (b, 0)),
                  pl.BlockSpec((1, a_q), lambda b, i: (0, 0))],
        out_specs=pl.BlockSpec((tq, a_q), lambda b, i: (b * nq + i, 0)),
        out_shape=jax.ShapeDtypeStruct((t, a_q), BF16),
        scratch_shapes=[pltpu.VMEM((3, heads, tq, kw), F32), pltpu.VMEM((tq, a_q), F32)],
        compiler_params=_cparams(("arbitrary", "arbitrary")),
        name="attn_window_gqa",
    )(slopes_a, sink, qa, ka, va, gain)


def _attn_b_kernel(slopes_ref, lq1_ref, lk1_ref, lq2_ref, lk2_ref, g_ref, d0_ref, feat_ref,
                   qt_ref, k_ref, vt_ref, o_ref,
                   w_ref, diag_ref, s0_ref, s1_ref, ref_ref, m_ref, l_ref, acc_ref, vt1_ref,
                   acc1_ref,
                   *, tile, sw, seq):
    h = pl.program_id(1)
    slope2 = slopes_ref[h] * LOG2E
    lam = (jnp.exp(jnp.sum(lq1_ref[...] * lk1_ref[...], keepdims=True))
           - jnp.exp(jnp.sum(lq2_ref[...] * lk2_ref[...], keepdims=True)) + LAM_INIT)
    halves = tile // sw
    n_tiles = seq // tile

    rho = lax.broadcasted_iota(jnp.int32, (N_BIAS_ROWS, 2 * tile), 0)
    lane = lax.broadcasted_iota(jnp.int32, (1, 2 * tile), 1)
    q_offset = ((lane // (2 * sw)) * sw + lane % sw).astype(F32)

    def bias_rows(sign, const):
        coeff = jnp.where(rho < 3, sign * slope2,
                          jnp.where(rho < 6, sign * 256.0 * slope2,
                                    jnp.where(rho < 9, const, 0.0)))
        hi = coeff.astype(BF16).astype(F32)
        mid = (coeff - hi).astype(BF16).astype(F32)
        lo = (coeff - hi - mid).astype(BF16).astype(F32)
        level = rho % 3
        return jnp.where(level == 0, hi, jnp.where(level == 1, mid, lo)).astype(BF16)

    w_ref[...] = jnp.zeros_like(w_ref)
    for half in range(halves):
        diag_ref[half] = -slope2 * jnp.abs(d0_ref[...] - float(half * sw))

    def norm_chunk(n, carry):
        kmax2, vmax = carry
        r0 = pl.multiple_of(n * tile, tile)
        kf = k_ref[pl.ds(r0, tile), :].astype(F32)
        kn2 = jnp.max(jnp.sum(kf * kf, axis=1, keepdims=True), axis=0, keepdims=True)
        va = jnp.abs(vt_ref[:, pl.ds(r0, tile)].astype(F32))
        vm = jnp.max(jnp.max(va, axis=1, keepdims=True), axis=0, keepdims=True)
        return jnp.maximum(kmax2, kn2), jnp.maximum(vmax, vm)

    kmax2, vmax = lax.fori_loop(0, n_tiles, norm_chunk,
                                (jnp.zeros((1, 1), F32), jnp.zeros((1, 1), F32)))
    vt1_ref[0:B_V_DIM, :] = vt_ref[...]
    extra = lax.broadcasted_iota(jnp.int32, (ONES_ROWS, seq), 0)
    vt1_ref[B_V_DIM:, :] = jnp.where(extra == 0, 1.0, 0.0).astype(BF16)

    def key_tile(i, t):
        return jnp.where(t == 0, i, jnp.where(t <= i, t - 1, t))

    def score_tile(i, t):
        j = key_tile(i, t)
        k0 = pl.multiple_of(j * tile, tile)
        widx = jnp.where(j == i, 0, jnp.where(j < i, 1, 2))
        lhs = jnp.concatenate([k_ref[pl.ds(k0, tile), :], feat_ref[pl.ds(k0, tile), :]], axis=1)
        return jnp.dot(lhs, w_ref[widx], preferred_element_type=F32)

    def scores(i, t, s_ref):
        s_ref[...] = score_tile(i, t)

    def softmax_pv(i, t, s_ref, fixed, same_tile=False):
        k0 = pl.multiple_of(key_tile(i, t) * tile, tile)
        for half in range(halves):
            for comp in range(2):
                cols = slice((2 * half + comp) * sw, (2 * half + comp + 1) * sw)
                t_sc = s_ref[:, cols]
                if same_tile:
                    t_sc = t_sc + diag_ref[half] - ref_ref[:, cols]
                if fixed:
                    acc1_ref[:, cols] += jnp.dot(vt1_ref[:, pl.ds(k0, tile)],
                                                 jnp.exp2(t_sc).astype(BF16),
                                                 preferred_element_type=F32)
                else:
                    m_old = m_ref[:, cols]
                    m_new = jnp.maximum(m_old, jnp.max(t_sc, axis=0, keepdims=True))
                    alpha = jnp.exp2(m_old - m_new)
                    p = jnp.exp2(t_sc - m_new)
                    l_ref[:, cols] = alpha * l_ref[:, cols] + jnp.sum(p, axis=0, keepdims=True)
                    m_ref[:, cols] = m_new
                    pv = jnp.dot(vt_ref[:, pl.ds(k0, tile)], p.astype(BF16),
                                 preferred_element_type=F32)
                    acc_ref[:, cols] = alpha * acc_ref[:, cols] + pv

    s_refs = (s0_ref, s1_ref)

    def start_query_tile(i):
        q0 = pl.multiple_of(i * tile, tile)
        for half in range(halves):
            qcols = pl.ds(q0 + half * sw, sw)
            c0 = 2 * half * sw
            for variant in range(3):
                w_ref[variant, 0:B_QK_DIM, c0:c0 + sw] = qt_ref[0:B_QK_DIM, qcols]
                w_ref[variant, B_QK_DIM:B_V_DIM, c0 + sw:c0 + 2 * sw] = qt_ref[B_QK_DIM:, qcols]
        s = score_tile(i, 0)
        s_refs[0][...] = s
        ref = jnp.max(s, axis=0, keepdims=True)
        ref_ref[...] = ref
        q_bias = slope2 * (q_offset + q0.astype(F32))
        w_ref[1, B_V_DIM:B_V_DIM + N_BIAS_ROWS, :] = bias_rows(1.0, -q_bias - ref)
        w_ref[2, B_V_DIM:B_V_DIM + N_BIAS_ROWS, :] = bias_rows(-1.0, q_bias - ref)
        qf = w_ref[0, 0:B_V_DIM, :].astype(F32)
        upper = jnp.sqrt(jnp.sum(qf * qf, axis=0, keepdims=True) * kmax2) * 1.01 + 1.0
        ok = (jnp.max(upper - ref) <= FIXED_REF_MAX_EXCESS) & (
            jnp.max(vmax) <= FIXED_REF_MAX_VALUE)
        return ok.astype(jnp.int32)

    def key_tile_pipeline(i, fixed):
        scores(i, 1, s_refs[1])
        softmax_pv(i, 0, s_refs[0], fixed, same_tile=True)

        def group(n, c2):
            for u in range(GROUP):
                scores(i, GROUP * n + u + 2, s_refs[u % 2])
                softmax_pv(i, GROUP * n + u + 1, s_refs[(u + 1) % 2], fixed)
            return c2

        n_groups = (n_tiles - 2) // GROUP
        lax.fori_loop(0, n_groups, group, 0)
        for t in range(n_groups * GROUP + 1, n_tiles):
            if t + 1 < n_tiles:
                scores(i, t + 1, s_refs[(t + 1) % 2])
            softmax_pv(i, t, s_refs[t % 2], fixed)

    def q_body(i, use_fixed):
        q0 = pl.multiple_of(i * tile, tile)
        l_ref[...] = jnp.zeros_like(l_ref)
        acc_ref[...] = jnp.zeros_like(acc_ref)

        @pl.when(use_fixed == 1)
        def _():
            acc1_ref[...] = jnp.zeros_like(acc1_ref)
            key_tile_pipeline(i, True)
            acc_ref[...] = acc1_ref[0:B_V_DIM, :]
            l_ref[...] = acc1_ref[B_V_DIM:B_V_DIM + 1, :]

        @pl.when(use_fixed != 1)
        def _():
            m_ref[...] = jnp.full_like(m_ref, NEG_INF)
            key_tile_pipeline(i, False)

        next_use_fixed = start_query_tile(jnp.minimum(i + 1, n_tiles - 1))

        o = acc_ref[...] / l_ref[...]
        for half in range(halves):
            c0 = 2 * half * sw
            od = (o[:, c0:c0 + sw] - lam * o[:, c0 + sw:c0 + 2 * sw]).T
            o_ref[pl.ds(q0 + half * sw, sw), :] = (
                _rms(od, g_ref[...]) * (1.0 - LAM_INIT)).astype(BF16)
        return next_use_fixed

    lax.fori_loop(0, n_tiles, q_body, start_query_tile(0))


def _attn_b(slopes_b, lq1, lk1, lq2, lk2, gain, qdt, kd, vdt, *, batch, seq, tile=512, sw=256):
    b_w, t = qdt.shape
    heads = b_w // B_V_DIM
    tile = min(tile, seq // 2)
    assert seq % (2 * tile) == 0 and tile % sw == 0
    r = lax.broadcasted_iota(jnp.int32, (tile, sw), 0)
    c = lax.broadcasted_iota(jnp.int32, (tile, sw), 1)
    d0 = (r - c).astype(F32)
    rk = jnp.arange(seq, dtype=jnp.int32)[:, None]
    fcol = jnp.arange(B_V_DIM, dtype=jnp.int32)[None, :]
    feat = jnp.where(fcol < 3, rk % 256,
                     jnp.where(fcol < 6, rk // 256, jnp.where(fcol < 9, 1, 0))).astype(BF16)
    kern = functools.partial(_attn_b_kernel, tile=tile, sw=sw, seq=seq)
    smem = pl.BlockSpec(memory_space=pltpu.SMEM)
    vec = lambda n: pl.BlockSpec((1, n), lambda b, h: (0, 0))
    tposed = pl.BlockSpec((B_V_DIM, seq), lambda b, h: (h, b))
    natural = pl.BlockSpec((seq, B_V_DIM), lambda b, h: (b, h))
    return pl.pallas_call(
        kern,
        grid=(batch, heads),
        in_specs=[smem, vec(B_QK_DIM), vec(B_QK_DIM), vec(B_QK_DIM), vec(B_QK_DIM),
                  vec(B_V_DIM),
                  pl.BlockSpec((tile, sw), lambda b, h: (0, 0)),
                  pl.BlockSpec((seq, B_V_DIM), lambda b, h: (0, 0)),
                  tposed, natural, tposed],
        out_specs=natural,
        out_shape=jax.ShapeDtypeStruct((t, b_w), BF16),
        scratch_shapes=[pltpu.VMEM((3, 2 * B_V_DIM, 2 * tile), BF16),
                        pltpu.VMEM((tile // sw, tile, sw), F32),
                        pltpu.VMEM((tile, 2 * tile), F32),
                        pltpu.VMEM((tile, 2 * tile), F32),
                        pltpu.VMEM((1, 2 * tile), F32),
                        pltpu.VMEM((1, 2 * tile), F32),
                        pltpu.VMEM((1, 2 * tile), F32),
                        pltpu.VMEM((B_V_DIM, 2 * tile), F32),
                        pltpu.VMEM((B_V_DIM + ONES_ROWS, seq), BF16),
                        pltpu.VMEM((B_V_DIM + ONES_ROWS, 2 * tile), F32)],
        compiler_params=_cparams(("arbitrary", "arbitrary")),
        name="attn_diff",
    )(slopes_b, lq1, lk1, lq2, lk2, gain, d0, feat, qdt, kd, vdt)


def _outproj_kernel(a_ref, b_ref, w_ref, x_ref, mod_ref, g_ref, x1_ref, h2_ref, *, a_w):
    mix = (jnp.dot(a_ref[...], w_ref[0:a_w, :], preferred_element_type=F32)
           + jnp.dot(b_ref[...], w_ref[a_w:, :], preferred_element_type=F32))
    g1 = mod_ref[0, 2:3, :]
    sh2 = mod_ref[0, 3:4, :]
    sc2 = mod_ref[0, 4:5, :]
    x1 = x_ref[...] + g1 * mix
    x1_ref[...] = x1
    h2_ref[...] = (_rms(x1, g_ref[...]) * (1.0 + sc2) + sh2).astype(BF16)


def _outproj(out_a, out_b, w_bf, x2, mod3, gain, *, seq, tm=512):
    t, d = x2.shape
    a_w = out_a.shape[1]
    b_w = out_b.shape[1]
    tiles_per_batch = seq // tm
    row = lambda width: pl.BlockSpec((tm, width), lambda i: (i, 0))
    return pl.pallas_call(
        functools.partial(_outproj_kernel, a_w=a_w),
        grid=(t // tm,),
        in_specs=[row(a_w), row(b_w), _resident((a_w + b_w, d)), row(d),
                  pl.BlockSpec((1, N_MOD, d), lambda i: (i // tiles_per_batch, 0, 0)),
                  pl.BlockSpec((1, d), lambda i: (0, 0))],
        out_specs=[row(d), row(d)],
        out_shape=[jax.ShapeDtypeStruct((t, d), F32), jax.ShapeDtypeStruct((t, d), BF16)],
        compiler_params=_cparams(("arbitrary",)),
        name="outproj_norm2",
    )(out_a, out_b, w_bf, x2, mod3, gain)


def _ffn_kernel(h_ref, wg_ref, wu_ref, wd_ref, x1_ref, mod_ref, fg_ref, o_ref):
    f = pl.program_id(1)

    @pl.when(f == 0)
    def _():
        o_ref[...] = jnp.zeros_like(o_ref)

    h = h_ref[...]
    g = jnp.dot(h, wg_ref[...], preferred_element_type=F32)
    u = jnp.dot(h, wu_ref[...], preferred_element_type=F32)
    a = (g * jax.nn.sigmoid(g) * u).astype(BF16)
    o_ref[...] += jnp.dot(a, wd_ref[...], preferred_element_type=F32)

    @pl.when(f == pl.num_programs(1) - 1)
    def _():
        g2 = mod_ref[0, 5:6, :]
        o_ref[...] = _rms(x1_ref[...] + g2 * o_ref[...], fg_ref[...])


def _ffn(h2, wg, wu, wd, x1, mod3, final_gain, *, seq, tm=512, tf=512):
    t, d = h2.shape
    ff = wg.shape[1]
    tm = min(tm, seq)
    if ff % tf:
        tf = 256
    assert ff % tf == 0 and t % tm == 0 and seq % tm == 0
    tiles_per_batch = seq // tm
    return pl.pallas_call(
        _ffn_kernel,
        grid=(t // tm, ff // tf),
        in_specs=[pl.BlockSpec((tm, d), lambda i, f: (i, 0)),
                  pl.BlockSpec((d, tf), lambda i, f: (0, f)),
                  pl.BlockSpec((d, tf), lambda i, f: (0, f)),
                  pl.BlockSpec((tf, d), lambda i, f: (f, 0)),
                  pl.BlockSpec((tm, d), lambda i, f: (i, 0)),
                  pl.BlockSpec((1, N_MOD, d), lambda i, f: (i // tiles_per_batch, 0, 0)),
                  pl.BlockSpec((1, d), lambda i, f: (0, 0))],
        out_specs=pl.BlockSpec((tm, d), lambda i, f: (i, 0)),
        out_shape=jax.ShapeDtypeStruct((t, d), F32),
        compiler_params=_cparams(("arbitrary", "arbitrary")),
        name="swiglu_ffn_final_norm",
    )(h2, wg, wu, wd, x1, mod3, final_gain)


def kernel(x, c, w_ada, b_ada, norm1_gain, w_in, a_sink, a_out_gain, diff_lq1, diff_lk1,
           diff_lq2, diff_lk2, diff_subln_gain, w_o, norm2_gain, w_gate, w_up, w_down,
           final_gain):
    batch, seq, d = x.shape
    assert w_ada.shape[0] == 1, "single-layer block"
    a_w = d // 2
    b_w = d - a_w
    a_heads = a_w // HEAD_DIM
    b_heads = b_w // B_V_DIM
    a_kv = A_KV_HEADS * HEAD_DIM
    n_heads = a_heads + b_heads
    slopes = 2.0 ** (-8.0 * jnp.arange(1, n_heads + 1, dtype=F32) / n_heads)

    rows = 8
    c_pad = jnp.zeros((rows, d), F32).at[:batch].set(c)
    mod = _ada(c_pad, w_ada[0], b_ada[0][None, :])[:batch]
    mod3 = mod.reshape(batch, N_MOD, d)

    x2 = x.reshape(batch * seq, d)
    qa, ka, va, qdt, kd, vdt = _inproj(
        x2, mod3, norm1_gain[0][None, :], w_in[0].astype(BF16),
        seq=seq, a_q=a_w, a_kv=a_kv, b_w=b_w)

    out_a = _attn_a(slopes[:a_heads], a_sink[0].astype(F32), qa, ka, va,
                    a_out_gain[0][None, :], batch=batch, seq=seq)
    out_b = _attn_b(slopes[a_heads:], diff_lq1[0][None, :], diff_lk1[0][None, :],
                    diff_lq2[0][None, :], diff_lk2[0][None, :], diff_subln_gain[0][None, :],
                    qdt, kd, vdt, batch=batch, seq=seq)

    x1, h2 = _outproj(out_a, out_b, w_o[0].astype(BF16), x2, mod3, norm2_gain[0][None, :],
                      seq=seq)
    out = _ffn(h2, w_gate[0].astype(BF16), w_up[0].astype(BF16), w_down[0].astype(BF16),
               x1, mod3, final_gain[None, :], seq=seq)
    return out.reshape(batch, seq, d)
```

```python
import functools
import math

import jax
import jax.numpy as jnp
from jax import lax
from jax.experimental import pallas as pl
from jax.experimental.pallas import tpu as pltpu

HEAD_DIM = 128
A_KV_HEADS = 2
WINDOW = 128
B_QK_DIM = 64
B_V_DIM = 2 * B_QK_DIM
N_MOD = 6
EPS = 1e-6
NEG_INF = -1e30
LAM_INIT = 0.8 - 0.6 * math.exp(-0.3 * 0)
LOG2E = math.log2(math.e)
GROUP = 6
FIXED_REF_MAX_EXCESS = 64.0
FIXED_REF_MAX_VALUE = 2.0 ** 30
ONES_ROWS = 16
N_BIAS_ROWS = 16

V7X_VMEM_LIMIT_BYTES = 56 * 1024 * 1024

BF16 = jnp.bfloat16
F32 = jnp.float32


def _cparams(semantics):
    return pltpu.CompilerParams(dimension_semantics=semantics,
                                vmem_limit_bytes=V7X_VMEM_LIMIT_BYTES)


def _rms(x, gain):
    return x * lax.rsqrt(jnp.mean(x * x, axis=-1, keepdims=True) + EPS) * gain


def _resident(shape):
    return pl.BlockSpec(shape, lambda *_: (0,) * len(shape), pipeline_mode=pl.Buffered(1))


def _ada_kernel(c_ref, w_ref, b_ref, o_ref):
    c = c_ref[...]
    sc = (c * jax.nn.sigmoid(c)).astype(BF16)
    o_ref[...] = jnp.dot(sc, w_ref[...].astype(BF16), preferred_element_type=F32) + b_ref[...]


def _ada(c_pad, w, b, tn=1024):
    rows, d = c_pad.shape
    n = w.shape[1]
    return pl.pallas_call(
        _ada_kernel,
        grid=(n // tn,),
        in_specs=[pl.BlockSpec((rows, d), lambda j: (0, 0)),
                  pl.BlockSpec((d, tn), lambda j: (0, j)),
                  pl.BlockSpec((1, tn), lambda j: (0, j))],
        out_specs=pl.BlockSpec((rows, tn), lambda j: (0, j)),
        out_shape=jax.ShapeDtypeStruct((rows, n), F32),
        compiler_params=_cparams(("arbitrary",)),
        name="ada_mod",
    )(c_pad, w, b)


def _inproj_kernel(x_ref, mod_ref, g_ref, w_ref,
                   qa_ref, ka_ref, va_ref, qdt_ref, kd_ref, vdt_ref, h_ref,
                   *, a_q, a_kv, b_w, chunk):
    x = x_ref[...]
    sh1 = mod_ref[0, 0:1, :]
    sc1 = mod_ref[0, 1:2, :]
    h_ref[...] = (_rms(x, g_ref[...]) * (1.0 + sc1) + sh1).astype(BF16)

    def proj(c0, width):
        return jnp.dot(h_ref[...], w_ref[:, c0:c0 + width], preferred_element_type=F32)

    o1 = a_q
    o2 = o1 + a_kv
    o3 = o2 + a_kv
    o4 = o3 + b_w
    o5 = o4 + b_w
    for c in range(0, a_q, chunk):
        qa_ref[:, c:c + chunk] = proj(c, chunk).astype(BF16)
    ka_ref[...] = proj(o1, a_kv).astype(BF16)
    va_ref[...] = proj(o2, a_kv).astype(BF16)
    for c in range(0, b_w, chunk):
        kd_ref[:, c:c + chunk] = proj(o4 + c, chunk).astype(BF16)
    qscale = B_QK_DIM ** -0.5 * LOG2E
    for c in range(0, b_w, chunk):
        q = proj(o3 + c, chunk) * qscale
        v = proj(o5 + c, chunk)
        for hc in range(0, chunk, B_V_DIM):
            qdt_ref[c + hc:c + hc + B_V_DIM, :] = q[:, hc:hc + B_V_DIM].T.astype(BF16)
            vdt_ref[c + hc:c + hc + B_V_DIM, :] = v[:, hc:hc + B_V_DIM].T.astype(BF16)


def _inproj(x2, mod3, gain, w_bf, *, seq, a_q, a_kv, b_w, tm=512):
    t, d = x2.shape
    n = w_bf.shape[1]
    tiles_per_batch = seq // tm
    chunk = min(512, a_q, b_w)
    kern = functools.partial(_inproj_kernel, a_q=a_q, a_kv=a_kv, b_w=b_w, chunk=chunk)
    row = lambda width: pl.BlockSpec((tm, width), lambda i: (i, 0))
    col = lambda height: pl.BlockSpec((height, tm), lambda i: (0, i))
    return pl.pallas_call(
        kern,
        grid=(t // tm,),
        in_specs=[row(d),
                  pl.BlockSpec((1, N_MOD, d), lambda i: (i // tiles_per_batch, 0, 0)),
                  pl.BlockSpec((1, d), lambda i: (0, 0)),
                  _resident((d, n))],
        out_specs=[row(a_q), row(a_kv), row(a_kv), col(b_w), row(b_w), col(b_w)],
        out_shape=[jax.ShapeDtypeStruct((t, a_q), BF16),
                   jax.ShapeDtypeStruct((t, a_kv), BF16),
                   jax.ShapeDtypeStruct((t, a_kv), BF16),
                   jax.ShapeDtypeStruct((b_w, t), BF16),
                   jax.ShapeDtypeStruct((t, b_w), BF16),
                   jax.ShapeDtypeStruct((b_w, t), BF16)],
        scratch_shapes=[pltpu.VMEM((tm, d), BF16)],
        compiler_params=_cparams(("arbitrary",)),
        name="norm1_inproj",
    )(x2, mod3, gain, w_bf)


def _attn_a_kernel(slopes_ref, sink_ref, q_ref, k_ref, v_ref, g_ref, o_ref, bias_ref, acc_ref,
                   *, tq, kw, seq, heads):
    i = pl.program_id(1)
    q0 = i * tq
    kstart = pl.multiple_of(jnp.clip(q0 - WINDOW, 0, seq - kw), WINDOW)
    group = heads // A_KV_HEADS

    @pl.when((pl.program_id(0) == 0) & (i == 0))
    def _():
        r = lax.broadcasted_iota(jnp.int32, (tq, kw), 0)
        c = lax.broadcasted_iota(jnp.int32, (tq, kw), 1)
        for case in range(3):
            dist = jnp.abs((r - c) + case * WINDOW)
            for h in range(heads):
                bias_ref[case, h] = jnp.where(dist <= WINDOW,
                                              -(slopes_ref[h] * LOG2E) * dist.astype(F32),
                                              NEG_INF)

    case = (q0 - kstart) // WINDOW
    scale = HEAD_DIM ** -0.5
    kwins = [k_ref[pl.ds(kstart, kw), kvh * HEAD_DIM:(kvh + 1) * HEAD_DIM]
             for kvh in range(A_KV_HEADS)]
    vwins = [v_ref[pl.ds(kstart, kw), kvh * HEAD_DIM:(kvh + 1) * HEAD_DIM]
             for kvh in range(A_KV_HEADS)]
    scores = [lax.dot_general(q_ref[:, h * HEAD_DIM:(h + 1) * HEAD_DIM], kwins[h // group],
                              (((1,), (1,)), ((), ())), preferred_element_type=F32)
              for h in range(heads)]
    probs = []
    inv_denoms = []
    for h in range(heads):
        s = scores[h] * (scale * LOG2E) + bias_ref[case, h]
        sink = sink_ref[h] * LOG2E
        m = jnp.maximum(jnp.max(s, axis=-1, keepdims=True), sink)
        p = jnp.exp2(s - m)
        denom = jnp.sum(p, axis=-1, keepdims=True) + jnp.exp2(sink - m)
        probs.append(p.astype(BF16))
        inv_denoms.append(1.0 / denom)
    for h in range(heads):
        acc_ref[:, h * HEAD_DIM:(h + 1) * HEAD_DIM] = inv_denoms[h] * jnp.dot(
            probs[h], vwins[h // group], preferred_element_type=F32)
    o_ref[...] = _rms(acc_ref[...], g_ref[...]).astype(BF16)


def _attn_a(slopes_a, sink, qa, ka, va, gain, *, batch, seq, tq=128):
    t, a_q = qa.shape
    a_kv = ka.shape[1]
    heads = a_q // HEAD_DIM
    kw = tq + 2 * WINDOW
    nq = seq // tq
    kern = functools.partial(_attn_a_kernel, tq=tq, kw=kw, seq=seq, heads=heads)
    smem = pl.BlockSpec(memory_space=pltpu.SMEM)
    return pl.pallas_call(
        kern,
        grid=(batch, nq),
        in_specs=[smem, smem,
                  pl.BlockSpec((tq, a_q), lambda b, i: (b * nq + i, 0)),
                  pl.BlockSpec((seq, a_kv), lambda b, i: (b, 0)),
                  pl.BlockSpec((seq, a_kv), lambda b, i: (b, 0)),
                  pl.BlockSpec((1, a_q), lambda b, i: (0, 0))],
        out_specs=pl.BlockSpec((tq, a_q), lambda b, i: (b * nq + i, 0)),
        out_shape=jax.ShapeDtypeStruct((t, a_q), BF16),
        scratch_shapes=[pltpu.VMEM((3, heads, tq, kw), F32), pltpu.VMEM((tq, a_q), F32)],
        compiler_params=_cparams(("arbitrary", "arbitrary")),
        name="attn_window_gqa",
    )(slopes_a, sink, qa, ka, va, gain)


def _attn_b_kernel(slopes_ref, lq1_ref, lk1_ref, lq2_ref, lk2_ref, g_ref, d0_ref, feat_ref,
                   qt_ref, k_ref, vt_ref, o_ref,
                   w_ref, diag_ref, s0_ref, s1_ref, m_ref, l_ref, acc_ref, vt1_ref, acc1_ref,
                   nref_ref, flag_ref,
                   *, tile, sw, seq):
    h = pl.program_id(1)
    slope2 = slopes_ref[h] * LOG2E
    lam = (jnp.exp(jnp.sum(lq1_ref[...] * lk1_ref[...], keepdims=True))
           - jnp.exp(jnp.sum(lq2_ref[...] * lk2_ref[...], keepdims=True)) + LAM_INIT)
    halves = tile // sw
    n_tiles = seq // tile

    rho = lax.broadcasted_iota(jnp.int32, (N_BIAS_ROWS, sw), 0)
    lane = lax.broadcasted_iota(jnp.int32, (N_BIAS_ROWS, sw), 1).astype(F32)
    coeff = jnp.where(rho < 3, slope2,
                      jnp.where(rho < 6, 256.0 * slope2,
                                jnp.where(rho < 9, -slope2 * lane, 0.0)))
    hi = coeff.astype(BF16).astype(F32)
    mid = (coeff - hi).astype(BF16).astype(F32)
    lo = (coeff - hi - mid).astype(BF16).astype(F32)
    level = rho % 3
    rows = jnp.where(level == 0, hi, jnp.where(level == 1, mid, lo))
    w_ref[...] = jnp.zeros_like(w_ref)
    for strip in range(2 * halves):
        cols = slice(strip * sw, (strip + 1) * sw)
        w_ref[1, B_V_DIM:B_V_DIM + N_BIAS_ROWS, cols] = rows.astype(BF16)
        w_ref[2, B_V_DIM:B_V_DIM + N_BIAS_ROWS, cols] = (-rows).astype(BF16)
    for half in range(halves):
        diag_ref[half] = -slope2 * jnp.abs(d0_ref[...] - float(half * sw))

    def norm_chunk(n, carry):
        kmax2, vmax = carry
        r0 = pl.multiple_of(n * tile, tile)
        kf = k_ref[pl.ds(r0, tile), :].astype(F32)
        kn2 = jnp.max(jnp.sum(kf * kf, axis=1, keepdims=True), axis=0, keepdims=True)
        va = jnp.abs(vt_ref[:, pl.ds(r0, tile)].astype(F32))
        vm = jnp.max(jnp.max(va, axis=1, keepdims=True), axis=0, keepdims=True)
        return jnp.maximum(kmax2, kn2), jnp.maximum(vmax, vm)

    kmax2, vmax = lax.fori_loop(0, n_tiles, norm_chunk,
                                (jnp.zeros((1, 1), F32), jnp.zeros((1, 1), F32)))
    vt1_ref[0:B_V_DIM, :] = vt_ref[...]
    extra = lax.broadcasted_iota(jnp.int32, (ONES_ROWS, seq), 0)
    vt1_ref[B_V_DIM:, :] = jnp.where(extra == 0, 1.0, 0.0).astype(BF16)

    def key_tile(i, t):
        return jnp.where(t == 0, i, jnp.where(t <= i, t - 1, t))

    def score_tile(i, t):
        j = key_tile(i, t)
        k0 = pl.multiple_of(j * tile, tile)
        widx = jnp.where(j == i, 0, jnp.where(j < i, 1, 2))
        lhs = jnp.concatenate([k_ref[pl.ds(k0, tile), :], feat_ref[...]], axis=1)
        return jnp.dot(lhs, w_ref[widx], preferred_element_type=F32)

    def scores(i, t, s_ref):
        s_ref[...] = score_tile(i, t)

    def softmax_pv(i, t, s_ref, fixed, same_tile=False):
        j = key_tile(i, t)
        k0 = pl.multiple_of(j * tile, tile)
        q0 = i * tile
        for half in range(halves):
            cst = slope2 * (k0 - q0 - half * sw).astype(F32)
            tc = 0.0 if same_tile else jnp.where(j < i, cst, -cst)
            for comp in range(2):
                cols = slice((2 * half + comp) * sw, (2 * half + comp + 1) * sw)
                t_sc = s_ref[:, cols]
                if same_tile:
                    t_sc = t_sc + diag_ref[half]
                if fixed:
                    p = jnp.exp2(t_sc - (m_ref[:, cols] - tc))
                    acc1_ref[:, cols] += jnp.dot(vt1_ref[:, pl.ds(k0, tile)], p.astype(BF16),
                                                 preferred_element_type=F32)
                else:
                    m_old = m_ref[:, cols]
                    m_new = jnp.maximum(m_old, jnp.max(t_sc, axis=0, keepdims=True) + tc)
                    alpha = jnp.exp2(m_old - m_new)
                    p = jnp.exp2(t_sc - (m_new - tc))
                    l_ref[:, cols] = alpha * l_ref[:, cols] + jnp.sum(p, axis=0, keepdims=True)
                    m_ref[:, cols] = m_new
                    pv = jnp.dot(vt_ref[:, pl.ds(k0, tile)], p.astype(BF16),
                                 preferred_element_type=F32)
                    acc_ref[:, cols] = alpha * acc_ref[:, cols] + pv

    s_refs = (s0_ref, s1_ref)

    def start_query_tile(i):
        q0 = pl.multiple_of(i * tile, tile)
        for half in range(halves):
            qcols = pl.ds(q0 + half * sw, sw)
            c0 = 2 * half * sw
            for variant in range(3):
                w_ref[variant, 0:B_QK_DIM, c0:c0 + sw] = qt_ref[0:B_QK_DIM, qcols]
                w_ref[variant, B_QK_DIM:B_V_DIM, c0 + sw:c0 + 2 * sw] = qt_ref[B_QK_DIM:, qcols]
        s = score_tile(i, 0)
        s_refs[0][...] = s
        ref = jnp.max(s, axis=0, keepdims=True)
        nref_ref[...] = ref
        qf = w_ref[0, 0:B_V_DIM, :].astype(F32)
        upper = jnp.sqrt(jnp.sum(qf * qf, axis=0, keepdims=True) * kmax2) * 1.01 + 1.0
        ok = (jnp.max(upper - ref) <= FIXED_REF_MAX_EXCESS) & (
            jnp.max(vmax) <= FIXED_REF_MAX_VALUE)
        flag_ref[0] = ok.astype(jnp.int32)

    def key_tile_pipeline(i, fixed):
        scores(i, 1, s_refs[1])
        softmax_pv(i, 0, s_refs[0], fixed, same_tile=True)

        def group(n, c2):
            for u in range(GROUP):
                scores(i, GROUP * n + u + 2, s_refs[u % 2])
                softmax_pv(i, GROUP * n + u + 1, s_refs[(u + 1) % 2], fixed)
            return c2

        n_groups = (n_tiles - 2) // GROUP
        lax.fori_loop(0, n_groups, group, 0)
        for t in range(n_groups * GROUP + 1, n_tiles):
            if t + 1 < n_tiles:
                scores(i, t + 1, s_refs[(t + 1) % 2])
            else:
                assert (t - 1) % 2 == 0
                start_query_tile(jnp.minimum(i + 1, n_tiles - 1))
            softmax_pv(i, t, s_refs[t % 2], fixed)

    def q_body(i, carry):
        q0 = pl.multiple_of(i * tile, tile)
        use_fixed = flag_ref[0]
        l_ref[...] = jnp.zeros_like(l_ref)
        acc_ref[...] = jnp.zeros_like(acc_ref)

        @pl.when(use_fixed == 1)
        def _():
            m_ref[...] = nref_ref[...]
            acc1_ref[...] = jnp.zeros_like(acc1_ref)
            key_tile_pipeline(i, True)
            acc_ref[...] = acc1_ref[0:B_V_DIM, :]
            l_ref[...] = acc1_ref[B_V_DIM:B_V_DIM + 1, :]

        @pl.when(use_fixed != 1)
        def _():
            m_ref[...] = jnp.full_like(m_ref, NEG_INF)
            key_tile_pipeline(i, False)

        o = acc_ref[...] / l_ref[...]
        for half in range(halves):
            c0 = 2 * half * sw
            od = (o[:, c0:c0 + sw] - lam * o[:, c0 + sw:c0 + 2 * sw]).T
            o_ref[pl.ds(q0 + half * sw, sw), :] = (
                _rms(od, g_ref[...]) * (1.0 - LAM_INIT)).astype(BF16)
        return carry

    start_query_tile(0)
    lax.fori_loop(0, n_tiles, q_body, 0)


def _attn_b(slopes_b, lq1, lk1, lq2, lk2, gain, qdt, kd, vdt, *, batch, seq, tile=512, sw=256):
    b_w, t = qdt.shape
    heads = b_w // B_V_DIM
    tile = min(tile, seq // 2)
    assert seq % (2 * tile) == 0 and tile % sw == 0
    r = lax.broadcasted_iota(jnp.int32, (tile, sw), 0)
    c = lax.broadcasted_iota(jnp.int32, (tile, sw), 1)
    d0 = (r - c).astype(F32)
    rk = jnp.arange(tile, dtype=jnp.int32)[:, None]
    fcol = jnp.arange(B_V_DIM, dtype=jnp.int32)[None, :]
    feat = jnp.where(fcol < 3, rk % 256,
                     jnp.where(fcol < 6, rk // 256, jnp.where(fcol < 9, 1, 0))).astype(BF16)
    kern = functools.partial(_attn_b_kernel, tile=tile, sw=sw, seq=seq)
    smem = pl.BlockSpec(memory_space=pltpu.SMEM)
    vec = lambda n: pl.BlockSpec((1, n), lambda b, h: (0, 0))
    tposed = pl.BlockSpec((B_V_DIM, seq), lambda b, h: (h, b))
    natural = pl.BlockSpec((seq, B_V_DIM), lambda b, h: (b, h))
    return pl.pallas_call(
        kern,
        grid=(batch, heads),
        in_specs=[smem, vec(B_QK_DIM), vec(B_QK_DIM), vec(B_QK_DIM), vec(B_QK_DIM),
                  vec(B_V_DIM),
                  pl.BlockSpec((tile, sw), lambda b, h: (0, 0)),
                  pl.BlockSpec((tile, B_V_DIM), lambda b, h: (0, 0)),
                  tposed, natural, tposed],
        out_specs=natural,
        out_shape=jax.ShapeDtypeStruct((t, b_w), BF16),
        scratch_shapes=[pltpu.VMEM((3, 2 * B_V_DIM, 2 * tile), BF16),
                        pltpu.VMEM((tile // sw, tile, sw), F32),
                        pltpu.VMEM((tile, 2 * tile), F32),
                        pltpu.VMEM((tile, 2 * tile), F32),
                        pltpu.VMEM((1, 2 * tile), F32),
                        pltpu.VMEM((1, 2 * tile), F32),
                        pltpu.VMEM((B_V_DIM, 2 * tile), F32),
                        pltpu.VMEM((B_V_DIM + ONES_ROWS, seq), BF16),
                        pltpu.VMEM((B_V_DIM + ONES_ROWS, 2 * tile), F32),
                        pltpu.VMEM((1, 2 * tile), F32),
                        pltpu.SMEM((1,), jnp.int32)],
        compiler_params=_cparams(("arbitrary", "arbitrary")),
        name="attn_diff",
    )(slopes_b, lq1, lk1, lq2, lk2, gain, d0, feat, qdt, kd, vdt)


def _outproj_kernel(a_ref, b_ref, w_ref, x_ref, mod_ref, g_ref, x1_ref, h2_ref, *, a_w):
    mix = (jnp.dot(a_ref[...], w_ref[0:a_w, :], preferred_element_type=F32)
           + jnp.dot(b_ref[...], w_ref[a_w:, :], preferred_element_type=F32))
    g1 = mod_ref[0, 2:3, :]
    sh2 = mod_ref[0, 3:4, :]
    sc2 = mod_ref[0, 4:5, :]
    x1 = x_ref[...] + g1 * mix
    x1_ref[...] = x1
    h2_ref[...] = (_rms(x1, g_ref[...]) * (1.0 + sc2) + sh2).astype(BF16)


def _outproj(out_a, out_b, w_bf, x2, mod3, gain, *, seq, tm=512):
    t, d = x2.shape
    a_w = out_a.shape[1]
    b_w = out_b.shape[1]
    tiles_per_batch = seq // tm
    row = lambda width: pl.BlockSpec((tm, width), lambda i: (i, 0))
    return pl.pallas_call(
        functools.partial(_outproj_kernel, a_w=a_w),
        grid=(t // tm,),
        in_specs=[row(a_w), row(b_w), _resident((a_w + b_w, d)), row(d),
                  pl.BlockSpec((1, N_MOD, d), lambda i: (i // tiles_per_batch, 0, 0)),
                  pl.BlockSpec((1, d), lambda i: (0, 0))],
        out_specs=[row(d), row(d)],
        out_shape=[jax.ShapeDtypeStruct((t, d), F32), jax.ShapeDtypeStruct((t, d), BF16)],
        compiler_params=_cparams(("arbitrary",)),
        name="outproj_norm2",
    )(out_a, out_b, w_bf, x2, mod3, gain)


def _ffn_kernel(h_ref, wg_ref, wu_ref, wd_ref, x1_ref, mod_ref, fg_ref, o_ref):
    f = pl.program_id(1)

    @pl.when(f == 0)
    def _():
        o_ref[...] = jnp.zeros_like(o_ref)

    h = h_ref[...]
    g = jnp.dot(h, wg_ref[...], preferred_element_type=F32)
    u = jnp.dot(h, wu_ref[...], preferred_element_type=F32)
    a = (g * jax.nn.sigmoid(g) * u).astype(BF16)
    o_ref[...] += jnp.dot(a, wd_ref[...], preferred_element_type=F32)

    @pl.when(f == pl.num_programs(1) - 1)
    def _():
        g2 = mod_ref[0, 5:6, :]
        o_ref[...] = _rms(x1_ref[...] + g2 * o_ref[...], fg_ref[...])


def _ffn(h2, wg, wu, wd, x1, mod3, final_gain, *, seq, tm=512, tf=512):
    t, d = h2.shape
    ff = wg.shape[1]
    tm = min(tm, seq)
    if ff % tf:
        tf = 256
    assert ff % tf == 0 and t % tm == 0 and seq % tm == 0
    tiles_per_batch = seq // tm
    return pl.pallas_call(
        _ffn_kernel,
        grid=(t // tm, ff // tf),
        in_specs=[pl.BlockSpec((tm, d), lambda i, f: (i, 0)),
                  pl.BlockSpec((d, tf), lambda i, f: (0, f)),
                  pl.BlockSpec((d, tf), lambda i, f: (0, f)),
                  pl.BlockSpec((tf, d), lambda i, f: (f, 0)),
                  pl.BlockSpec((tm, d), lambda i, f: (i, 0)),
                  pl.BlockSpec((1, N_MOD, d), lambda i, f: (i // tiles_per_batch, 0, 0)),
                  pl.BlockSpec((1, d), lambda i, f: (0, 0))],
        out_specs=pl.BlockSpec((tm, d), lambda i, f: (i, 0)),
        out_shape=jax.ShapeDtypeStruct((t, d), F32),
        compiler_params=_cparams(("arbitrary", "arbitrary")),
        name="swiglu_ffn_final_norm",
    )(h2, wg, wu, wd, x1, mod3, final_gain)


def kernel(x, c, w_ada, b_ada, norm1_gain, w_in, a_sink, a_out_gain, diff_lq1, diff_lk1,
           diff_lq2, diff_lk2, diff_subln_gain, w_o, norm2_gain, w_gate, w_up, w_down,
           final_gain):
    batch, seq, d = x.shape
    assert w_ada.shape[0] == 1, "single-layer block"
    a_w = d // 2
    b_w = d - a_w
    a_heads = a_w // HEAD_DIM
    b_heads = b_w // B_V_DIM
    a_kv = A_KV_HEADS * HEAD_DIM
    n_heads = a_heads + b_heads
    slopes = 2.0 ** (-8.0 * jnp.arange(1, n_heads + 1, dtype=F32) / n_heads)

    rows = 8
    c_pad = jnp.zeros((rows, d), F32).at[:batch].set(c)
    mod = _ada(c_pad, w_ada[0], b_ada[0][None, :])[:batch]
    mod3 = mod.reshape(batch, N_MOD, d)

    x2 = x.reshape(batch * seq, d)
    qa, ka, va, qdt, kd, vdt = _inproj(
        x2, mod3, norm1_gain[0][None, :], w_in[0].astype(BF16),
        seq=seq, a_q=a_w, a_kv=a_kv, b_w=b_w)

    out_a = _attn_a(slopes[:a_heads], a_sink[0].astype(F32), qa, ka, va,
                    a_out_gain[0][None, :], batch=batch, seq=seq)
    out_b = _attn_b(slopes[a_heads:], diff_lq1[0][None, :], diff_lk1[0][None, :],
                    diff_lq2[0][None, :], diff_lk2[0][None, :], diff_subln_gain[0][None, :],
                    qdt, kd, vdt, batch=batch, seq=seq)

    x1, h2 = _outproj(out_a, out_b, w_o[0].astype(BF16), x2, mod3, norm2_gain[0][None, :],
                      seq=seq)
    out = _ffn(h2, w_gate[0].astype(BF16), w_up[0].astype(BF16), w_down[0].astype(BF16),
               x1, mod3, final_gain[None, :], seq=seq)
    return out.reshape(batch, seq, d)
```

```python
import functools
import math

import jax
import jax.numpy as jnp
from jax import lax
from jax.experimental import pallas as pl
from jax.experimental.pallas import tpu as pltpu

HEAD_DIM = 128
A_KV_HEADS = 2
WINDOW = 128
B_QK_DIM = 64
B_V_DIM = 2 * B_QK_DIM
N_MOD = 6
EPS = 1e-6
NEG_INF = -1e30
LAM_INIT = 0.8 - 0.6 * math.exp(-0.3 * 0)
LOG2E = math.log2(math.e)
GROUP = 6
FIXED_REF_MAX_EXCESS = 64.0
FIXED_REF_MAX_VALUE = 2.0 ** 30
ONES_ROWS = 16
N_BIAS_ROWS = 16

V7X_VMEM_LIMIT_BYTES = 56 * 1024 * 1024

BF16 = jnp.bfloat16
F32 = jnp.float32


def _cparams(semantics):
    return pltpu.CompilerParams(dimension_semantics=semantics,
                                vmem_limit_bytes=V7X_VMEM_LIMIT_BYTES)


def _rms(x, gain):
    return x * lax.rsqrt(jnp.mean(x * x, axis=-1, keepdims=True) + EPS) * gain


def _resident(shape):
    return pl.BlockSpec(shape, lambda *_: (0,) * len(shape), pipeline_mode=pl.Buffered(1))


def _ada_kernel(c_ref, w_ref, b_ref, o_ref):
    c = c_ref[...]
    sc = (c * jax.nn.sigmoid(c)).astype(BF16)
    o_ref[...] = jnp.dot(sc, w_ref[...].astype(BF16), preferred_element_type=F32) + b_ref[...]


def _ada(c_pad, w, b, tn=1024):
    rows, d = c_pad.shape
    n = w.shape[1]
    return pl.pallas_call(
        _ada_kernel,
        grid=(n // tn,),
        in_specs=[pl.BlockSpec((rows, d), lambda j: (0, 0)),
                  pl.BlockSpec((d, tn), lambda j: (0, j)),
                  pl.BlockSpec((1, tn), lambda j: (0, j))],
        out_specs=pl.BlockSpec((rows, tn), lambda j: (0, j)),
        out_shape=jax.ShapeDtypeStruct((rows, n), F32),
        compiler_params=_cparams(("arbitrary",)),
        name="ada_mod",
    )(c_pad, w, b)


def _inproj_kernel(x_ref, mod_ref, g_ref, w_ref,
                   qa_ref, ka_ref, va_ref, qdt_ref, kd_ref, vdt_ref, h_ref,
                   *, a_q, a_kv, b_w, chunk):
    x = x_ref[...]
    sh1 = mod_ref[0, 0:1, :]
    sc1 = mod_ref[0, 1:2, :]
    h_ref[...] = (_rms(x, g_ref[...]) * (1.0 + sc1) + sh1).astype(BF16)

    def proj(c0, width):
        return jnp.dot(h_ref[...], w_ref[:, c0:c0 + width], preferred_element_type=F32)

    o1 = a_q
    o2 = o1 + a_kv
    o3 = o2 + a_kv
    o4 = o3 + b_w
    o5 = o4 + b_w
    for c in range(0, a_q, chunk):
        qa_ref[:, c:c + chunk] = proj(c, chunk).astype(BF16)
    ka_ref[...] = proj(o1, a_kv).astype(BF16)
    va_ref[...] = proj(o2, a_kv).astype(BF16)
    for c in range(0, b_w, chunk):
        kd_ref[:, c:c + chunk] = proj(o4 + c, chunk).astype(BF16)
    qscale = B_QK_DIM ** -0.5 * LOG2E
    for c in range(0, b_w, chunk):
        q = proj(o3 + c, chunk) * qscale
        v = proj(o5 + c, chunk)
        for hc in range(0, chunk, B_V_DIM):
            qdt_ref[c + hc:c + hc + B_V_DIM, :] = q[:, hc:hc + B_V_DIM].T.astype(BF16)
            vdt_ref[c + hc:c + hc + B_V_DIM, :] = v[:, hc:hc + B_V_DIM].T.astype(BF16)


def _inproj(x2, mod3, gain, w_bf, *, seq, a_q, a_kv, b_w, tm=512):
    t, d = x2.shape
    n = w_bf.shape[1]
    tiles_per_batch = seq // tm
    chunk = min(512, a_q, b_w)
    kern = functools.partial(_inproj_kernel, a_q=a_q, a_kv=a_kv, b_w=b_w, chunk=chunk)
    row = lambda width: pl.BlockSpec((tm, width), lambda i: (i, 0))
    col = lambda height: pl.BlockSpec((height, tm), lambda i: (0, i))
    return pl.pallas_call(
        kern,
        grid=(t // tm,),
        in_specs=[row(d),
                  pl.BlockSpec((1, N_MOD, d), lambda i: (i // tiles_per_batch, 0, 0)),
                  pl.BlockSpec((1, d), lambda i: (0, 0)),
                  _resident((d, n))],
        out_specs=[row(a_q), row(a_kv), row(a_kv), col(b_w), row(b_w), col(b_w)],
        out_shape=[jax.ShapeDtypeStruct((t, a_q), BF16),
                   jax.ShapeDtypeStruct((t, a_kv), BF16),
                   jax.ShapeDtypeStruct((t, a_kv), BF16),
                   jax.ShapeDtypeStruct((b_w, t), BF16),
                   jax.ShapeDtypeStruct((t, b_w), BF16),
                   jax.ShapeDtypeStruct((b_w, t), BF16)],
        scratch_shapes=[pltpu.VMEM((tm, d), BF16)],
        compiler_params=_cparams(("arbitrary",)),
        name="norm1_inproj",
    )(x2, mod3, gain, w_bf)


def _attn_a_kernel(slopes_ref, sink_ref, q_ref, k_ref, v_ref, g_ref, o_ref, bias_ref, acc_ref,
                   *, tq, kw, seq, heads):
    i = pl.program_id(1)
    q0 = i * tq
    kstart = pl.multiple_of(jnp.clip(q0 - WINDOW, 0, seq - kw), WINDOW)
    group = heads // A_KV_HEADS

    @pl.when((pl.program_id(0) == 0) & (i == 0))
    def _():
        r = lax.broadcasted_iota(jnp.int32, (tq, kw), 0)
        c = lax.broadcasted_iota(jnp.int32, (tq, kw), 1)
        for case in range(3):
            dist = jnp.abs((r - c) + case * WINDOW)
            for h in range(heads):
                bias_ref[case, h] = jnp.where(dist <= WINDOW,
                                              -(slopes_ref[h] * LOG2E) * dist.astype(F32),
                                              NEG_INF)

    case = (q0 - kstart) // WINDOW
    scale = HEAD_DIM ** -0.5
    kwins = [k_ref[pl.ds(kstart, kw), kvh * HEAD_DIM:(kvh + 1) * HEAD_DIM]
             for kvh in range(A_KV_HEADS)]
    vwins = [v_ref[pl.ds(kstart, kw), kvh * HEAD_DIM:(kvh + 1) * HEAD_DIM]
             for kvh in range(A_KV_HEADS)]
    scores = [lax.dot_general(q_ref[:, h * HEAD_DIM:(h + 1) * HEAD_DIM], kwins[h // group],
                              (((1,), (1,)), ((), ())), preferred_element_type=F32)
              for h in range(heads)]
    probs = []
    inv_denoms = []
    for h in range(heads):
        s = scores[h] * (scale * LOG2E) + bias_ref[case, h]
        sink = sink_ref[h] * LOG2E
        m = jnp.maximum(jnp.max(s, axis=-1, keepdims=True), sink)
        p = jnp.exp2(s - m)
        denom = jnp.sum(p, axis=-1, keepdims=True) + jnp.exp2(sink - m)
        probs.append(p.astype(BF16))
        inv_denoms.append(1.0 / denom)
    for h in range(heads):
        acc_ref[:, h * HEAD_DIM:(h + 1) * HEAD_DIM] = inv_denoms[h] * jnp.dot(
            probs[h], vwins[h // group], preferred_element_type=F32)
    o_ref[...] = _rms(acc_ref[...], g_ref[...]).astype(BF16)


def _attn_a(slopes_a, sink, qa, ka, va, gain, *, batch, seq, tq=128):
    t, a_q = qa.shape
    a_kv = ka.shape[1]
    heads = a_q // HEAD_DIM
    kw = tq + 2 * WINDOW
    nq = seq // tq
    kern = functools.partial(_attn_a_kernel, tq=tq, kw=kw, seq=seq, heads=heads)
    smem = pl.BlockSpec(memory_space=pltpu.SMEM)
    return pl.pallas_call(
        kern,
        grid=(batch, nq),
        in_specs=[smem, smem,
                  pl.BlockSpec((tq, a_q), lambda b, i: (b * nq + i, 0)),
                  pl.BlockSpec((seq, a_kv), lambda b, i: (b, 0)),
                  pl.BlockSpec((seq, a_kv), lambda b, i: (b, 0)),
                  pl.BlockSpec((1, a_q), lambda b, i: (0, 0))],
        out_specs=pl.BlockSpec((tq, a_q), lambda b, i: (b * nq + i, 0)),
        out_shape=jax.ShapeDtypeStruct((t, a_q), BF16),
        scratch_shapes=[pltpu.VMEM((3, heads, tq, kw), F32), pltpu.VMEM((tq, a_q), F32)],
        compiler_params=_cparams(("arbitrary", "arbitrary")),
        name="attn_window_gqa",
    )(slopes_a, sink, qa, ka, va, gain)


def _attn_b_kernel(slopes_ref, lq1_ref, lk1_ref, lq2_ref, lk2_ref, g_ref, d0_ref, feat_ref,
                   qt_ref, k_ref, vt_ref, o_ref,
                   w_ref, diag_ref, s0_ref, s1_ref, m_ref, l_ref, acc_ref, vt1_ref, acc1_ref,
                   nref_ref, flag_ref,
                   *, tile, sw, seq):
    h = pl.program_id(1)
    slope2 = slopes_ref[h] * LOG2E
    lam = (jnp.exp(jnp.sum(lq1_ref[...] * lk1_ref[...], keepdims=True))
           - jnp.exp(jnp.sum(lq2_ref[...] * lk2_ref[...], keepdims=True)) + LAM_INIT)
    halves = tile // sw
    n_tiles = seq // tile

    rho = lax.broadcasted_iota(jnp.int32, (N_BIAS_ROWS, sw), 0)
    lane = lax.broadcasted_iota(jnp.int32, (N_BIAS_ROWS, sw), 1).astype(F32)
    coeff = jnp.where(rho < 3, slope2,
                      jnp.where(rho < 6, 256.0 * slope2,
                                jnp.where(rho < 9, -slope2 * lane, 0.0)))
    hi = coeff.astype(BF16).astype(F32)
    mid = (coeff - hi).astype(BF16).astype(F32)
    lo = (coeff - hi - mid).astype(BF16).astype(F32)
    level = rho % 3
    rows = jnp.where(level == 0, hi, jnp.where(level == 1, mid, lo))
    w_ref[...] = jnp.zeros_like(w_ref)
    for strip in range(2 * halves):
        cols = slice(strip * sw, (strip + 1) * sw)
        w_ref[1, B_V_DIM:B_V_DIM + N_BIAS_ROWS, cols] = rows.astype(BF16)
        w_ref[2, B_V_DIM:B_V_DIM + N_BIAS_ROWS, cols] = (-rows).astype(BF16)
    for half in range(halves):
        diag_ref[half] = -slope2 * jnp.abs(d0_ref[...] - float(half * sw))

    def norm_chunk(n, carry):
        kmax2, vmax = carry
        r0 = pl.multiple_of(n * tile, tile)
        kf = k_ref[pl.ds(r0, tile), :].astype(F32)
        kn2 = jnp.max(jnp.sum(kf * kf, axis=1, keepdims=True), axis=0, keepdims=True)
        va = jnp.abs(vt_ref[:, pl.ds(r0, tile)].astype(F32))
        vm = jnp.max(jnp.max(va, axis=1, keepdims=True), axis=0, keepdims=True)
        return jnp.maximum(kmax2, kn2), jnp.maximum(vmax, vm)

    kmax2, vmax = lax.fori_loop(0, n_tiles, norm_chunk,
                                (jnp.zeros((1, 1), F32), jnp.zeros((1, 1), F32)))
    vt1_ref[0:B_V_DIM, :] = vt_ref[...]
    extra = lax.broadcasted_iota(jnp.int32, (ONES_ROWS, seq), 0)
    vt1_ref[B_V_DIM:, :] = jnp.where(extra == 0, 1.0, 0.0).astype(BF16)

    def key_tile(i, t):
        return jnp.where(t == 0, i, jnp.where(t <= i, t - 1, t))

    def score_tile(i, t):
        j = key_tile(i, t)
        k0 = pl.multiple_of(j * tile, tile)
        widx = jnp.where(j == i, 0, jnp.where(j < i, 1, 2))
        lhs = jnp.concatenate([k_ref[pl.ds(k0, tile), :], feat_ref[...]], axis=1)
        return jnp.dot(lhs, w_ref[widx], preferred_element_type=F32)

    def scores(i, t, s_ref):
        s_ref[...] = score_tile(i, t)

    def softmax_pv(i, t, s_ref, fixed, same_tile=False):
        j = key_tile(i, t)
        k0 = pl.multiple_of(j * tile, tile)
        q0 = i * tile
        for half in range(halves):
            cst = slope2 * (k0 - q0 - half * sw).astype(F32)
            tc = 0.0 if same_tile else jnp.where(j < i, cst, -cst)
            for comp in range(2):
                cols = slice((2 * half + comp) * sw, (2 * half + comp + 1) * sw)
                t_sc = s_ref[:, cols]
                if same_tile:
                    t_sc = t_sc + diag_ref[half]
                if fixed:
                    p = jnp.exp2(t_sc - (m_ref[:, cols] - tc))
                    acc1_ref[:, cols] += jnp.dot(vt1_ref[:, pl.ds(k0, tile)], p.astype(BF16),
                                                 preferred_element_type=F32)
                else:
                    m_old = m_ref[:, cols]
                    m_new = jnp.maximum(m_old, jnp.max(t_sc, axis=0, keepdims=True) + tc)
                    alpha = jnp.exp2(m_old - m_new)
                    p = jnp.exp2(t_sc - (m_new - tc))
                    l_ref[:, cols] = alpha * l_ref[:, cols] + jnp.sum(p, axis=0, keepdims=True)
                    m_ref[:, cols] = m_new
                    pv = jnp.dot(vt_ref[:, pl.ds(k0, tile)], p.astype(BF16),
                                 preferred_element_type=F32)
                    acc_ref[:, cols] = alpha * acc_ref[:, cols] + pv

    s_refs = (s0_ref, s1_ref)

    def start_query_tile(i):
        q0 = pl.multiple_of(i * tile, tile)
        for half in range(halves):
            qcols = pl.ds(q0 + half * sw, sw)
            c0 = 2 * half * sw
            for variant in range(3):
                w_ref[variant, 0:B_QK_DIM, c0:c0 + sw] = qt_ref[0:B_QK_DIM, qcols]
                w_ref[variant, B_QK_DIM:B_V_DIM, c0 + sw:c0 + 2 * sw] = qt_ref[B_QK_DIM:, qcols]
        s = score_tile(i, 0)
        s_refs[0][...] = s
        ref = jnp.max(s, axis=0, keepdims=True)
        nref_ref[...] = ref
        qf = w_ref[0, 0:B_V_DIM, :].astype(F32)
        upper = jnp.sqrt(jnp.sum(qf * qf, axis=0, keepdims=True) * kmax2) * 1.01 + 1.0
        ok = (jnp.max(upper - ref) <= FIXED_REF_MAX_EXCESS) & (
            jnp.max(vmax) <= FIXED_REF_MAX_VALUE)
        flag_ref[0] = ok.astype(jnp.int32)

    def finish_query_tile(i):
        q0 = pl.multiple_of(i * tile, tile)
        o = acc_ref[...] / l_ref[...]
        for half in range(halves):
            c0 = 2 * half * sw
            od = (o[:, c0:c0 + sw] - lam * o[:, c0 + sw:c0 + 2 * sw]).T
            o_ref[pl.ds(q0 + half * sw, sw), :] = (
                _rms(od, g_ref[...]) * (1.0 - LAM_INIT)).astype(BF16)

    def key_tile_pipeline(i, fixed):
        scores(i, 1, s_refs[1])
        if fixed:
            finish_query_tile(jnp.maximum(i - 1, 0))
        softmax_pv(i, 0, s_refs[0], fixed, same_tile=True)

        def group(n, c2):
            for u in range(GROUP):
                scores(i, GROUP * n + u + 2, s_refs[u % 2])
                softmax_pv(i, GROUP * n + u + 1, s_refs[(u + 1) % 2], fixed)
            return c2

        n_groups = (n_tiles - 2) // GROUP
        lax.fori_loop(0, n_groups, group, 0)
        for t in range(n_groups * GROUP + 1, n_tiles):
            if t + 1 < n_tiles:
                scores(i, t + 1, s_refs[(t + 1) % 2])
            else:
                assert (t - 1) % 2 == 0
                start_query_tile(jnp.minimum(i + 1, n_tiles - 1))
            softmax_pv(i, t, s_refs[t % 2], fixed)

    def q_body(i, carry):
        use_fixed = flag_ref[0]

        @pl.when(use_fixed == 1)
        def _():
            m_ref[...] = nref_ref[...]
            acc1_ref[...] = jnp.zeros_like(acc1_ref)
            key_tile_pipeline(i, True)
            acc_ref[...] = acc1_ref[0:B_V_DIM, :]
            l_ref[...] = acc1_ref[B_V_DIM:B_V_DIM + 1, :]

        @pl.when(use_fixed != 1)
        def _():
            finish_query_tile(jnp.maximum(i - 1, 0))
            m_ref[...] = jnp.full_like(m_ref, NEG_INF)
            l_ref[...] = jnp.zeros_like(l_ref)
            acc_ref[...] = jnp.zeros_like(acc_ref)
            key_tile_pipeline(i, False)

        return carry

    acc_ref[...] = jnp.zeros_like(acc_ref)
    l_ref[...] = jnp.ones_like(l_ref)
    start_query_tile(0)
    lax.fori_loop(0, n_tiles, q_body, 0)
    finish_query_tile(n_tiles - 1)


def _attn_b(slopes_b, lq1, lk1, lq2, lk2, gain, qdt, kd, vdt, *, batch, seq, tile=512, sw=256):
    b_w, t = qdt.shape
    heads = b_w // B_V_DIM
    tile = min(tile, seq // 2)
    assert seq % (2 * tile) == 0 and tile % sw == 0
    r = lax.broadcasted_iota(jnp.int32, (tile, sw), 0)
    c = lax.broadcasted_iota(jnp.int32, (tile, sw), 1)
    d0 = (r - c).astype(F32)
    rk = jnp.arange(tile, dtype=jnp.int32)[:, None]
    fcol = jnp.arange(B_V_DIM, dtype=jnp.int32)[None, :]
    feat = jnp.where(fcol < 3, rk % 256,
                     jnp.where(fcol < 6, rk // 256, jnp.where(fcol < 9, 1, 0))).astype(BF16)
    kern = functools.partial(_attn_b_kernel, tile=tile, sw=sw, seq=seq)
    smem = pl.BlockSpec(memory_space=pltpu.SMEM)
    vec = lambda n: pl.BlockSpec((1, n), lambda b, h: (0, 0))
    tposed = pl.BlockSpec((B_V_DIM, seq), lambda b, h: (h, b))
    natural = pl.BlockSpec((seq, B_V_DIM), lambda b, h: (b, h))
    return pl.pallas_call(
        kern,
        grid=(batch, heads),
        in_specs=[smem, vec(B_QK_DIM), vec(B_QK_DIM), vec(B_QK_DIM), vec(B_QK_DIM),
                  vec(B_V_DIM),
                  pl.BlockSpec((tile, sw), lambda b, h: (0, 0)),
                  pl.BlockSpec((tile, B_V_DIM), lambda b, h: (0, 0)),
                  tposed, natural, tposed],
        out_specs=natural,
        out_shape=jax.ShapeDtypeStruct((t, b_w), BF16),
        scratch_shapes=[pltpu.VMEM((3, 2 * B_V_DIM, 2 * tile), BF16),
                        pltpu.VMEM((tile // sw, tile, sw), F32),
                        pltpu.VMEM((tile, 2 * tile), F32),
                        pltpu.VMEM((tile, 2 * tile), F32),
                        pltpu.VMEM((1, 2 * tile), F32),
                        pltpu.VMEM((1, 2 * tile), F32),
                        pltpu.VMEM((B_V_DIM, 2 * tile), F32),
                        pltpu.VMEM((B_V_DIM + ONES_ROWS, seq), BF16),
                        pltpu.VMEM((B_V_DIM + ONES_ROWS, 2 * tile), F32),
                        pltpu.VMEM((1, 2 * tile), F32),
                        pltpu.SMEM((1,), jnp.int32)],
        compiler_params=_cparams(("arbitrary", "arbitrary")),
        name="attn_diff",
    )(slopes_b, lq1, lk1, lq2, lk2, gain, d0, feat, qdt, kd, vdt)


def _outproj_kernel(a_ref, b_ref, w_ref, x_ref, mod_ref, g_ref, x1_ref, h2_ref, *, a_w):
    mix = (jnp.dot(a_ref[...], w_ref[0:a_w, :], preferred_element_type=F32)
           + jnp.dot(b_ref[...], w_ref[a_w:, :], preferred_element_type=F32))
    g1 = mod_ref[0, 2:3, :]
    sh2 = mod_ref[0, 3:4, :]
    sc2 = mod_ref[0, 4:5, :]
    x1 = x_ref[...] + g1 * mix
    x1_ref[...] = x1
    h2_ref[...] = (_rms(x1, g_ref[...]) * (1.0 + sc2) + sh2).astype(BF16)


def _outproj(out_a, out_b, w_bf, x2, mod3, gain, *, seq, tm=512):
    t, d = x2.shape
    a_w = out_a.shape[1]
    b_w = out_b.shape[1]
    tiles_per_batch = seq // tm
    row = lambda width: pl.BlockSpec((tm, width), lambda i: (i, 0))
    return pl.pallas_call(
        functools.partial(_outproj_kernel, a_w=a_w),
        grid=(t // tm,),
        in_specs=[row(a_w), row(b_w), _resident((a_w + b_w, d)), row(d),
                  pl.BlockSpec((1, N_MOD, d), lambda i: (i // tiles_per_batch, 0, 0)),
                  pl.BlockSpec((1, d), lambda i: (0, 0))],
        out_specs=[row(d), row(d)],
        out_shape=[jax.ShapeDtypeStruct((t, d), F32), jax.ShapeDtypeStruct((t, d), BF16)],
        compiler_params=_cparams(("arbitrary",)),
        name="outproj_norm2",
    )(out_a, out_b, w_bf, x2, mod3, gain)


def _ffn_kernel(h_ref, wg_ref, wu_ref, wd_ref, x1_ref, mod_ref, fg_ref, o_ref):
    f = pl.program_id(1)

    @pl.when(f == 0)
    def _():
        o_ref[...] = jnp.zeros_like(o_ref)

    h = h_ref[...]
    g = jnp.dot(h, wg_ref[...], preferred_element_type=F32)
    u = jnp.dot(h, wu_ref[...], preferred_element_type=F32)
    a = (g * jax.nn.sigmoid(g) * u).astype(BF16)
    o_ref[...] += jnp.dot(a, wd_ref[...], preferred_element_type=F32)

    @pl.when(f == pl.num_programs(1) - 1)
    def _():
        g2 = mod_ref[0, 5:6, :]
        o_ref[...] = _rms(x1_ref[...] + g2 * o_ref[...], fg_ref[...])


def _ffn(h2, wg, wu, wd, x1, mod3, final_gain, *, seq, tm=512, tf=512):
    t, d = h2.shape
    ff = wg.shape[1]
    tm = min(tm, seq)
    if ff % tf:
        tf = 256
    assert ff % tf == 0 and t % tm == 0 and seq % tm == 0
    tiles_per_batch = seq // tm
    return pl.pallas_call(
        _ffn_kernel,
        grid=(t // tm, ff // tf),
        in_specs=[pl.BlockSpec((tm, d), lambda i, f: (i, 0)),
                  pl.BlockSpec((d, tf), lambda i, f: (0, f)),
                  pl.BlockSpec((d, tf), lambda i, f: (0, f)),
                  pl.BlockSpec((tf, d), lambda i, f: (f, 0)),
                  pl.BlockSpec((tm, d), lambda i, f: (i, 0)),
                  pl.BlockSpec((1, N_MOD, d), lambda i, f: (i // tiles_per_batch, 0, 0)),
                  pl.BlockSpec((1, d), lambda i, f: (0, 0))],
        out_specs=pl.BlockSpec((tm, d), lambda i, f: (i, 0)),
        out_shape=jax.ShapeDtypeStruct((t, d), F32),
        compiler_params=_cparams(("arbitrary", "arbitrary")),
        name="swiglu_ffn_final_norm",
    )(h2, wg, wu, wd, x1, mod3, final_gain)


def kernel(x, c, w_ada, b_ada, norm1_gain, w_in, a_sink, a_out_gain, diff_lq1, diff_lk1,
           diff_lq2, diff_lk2, diff_subln_gain, w_o, norm2_gain, w_gate, w_up, w_down,
           final_gain):
    batch, seq, d = x.shape
    assert w_ada.shape[0] == 1, "single-layer block"
    a_w = d // 2
    b_w = d - a_w
    a_heads = a_w // HEAD_DIM
    b_heads = b_w // B_V_DIM
    a_kv = A_KV_HEADS * HEAD_DIM
    n_heads = a_heads + b_heads
    slopes = 2.0 ** (-8.0 * jnp.arange(1, n_heads + 1, dtype=F32) / n_heads)

    rows = 8
    c_pad = jnp.zeros((rows, d), F32).at[:batch].set(c)
    mod = _ada(c_pad, w_ada[0], b_ada[0][None, :])[:batch]
    mod3 = mod.reshape(batch, N_MOD, d)

    x2 = x.reshape(batch * seq, d)
    qa, ka, va, qdt, kd, vdt = _inproj(
        x2, mod3, norm1_gain[0][None, :], w_in[0].astype(BF16),
        seq=seq, a_q=a_w, a_kv=a_kv, b_w=b_w)

    out_a = _attn_a(slopes[:a_heads], a_sink[0].astype(F32), qa, ka, va,
                    a_out_gain[0][None, :], batch=batch, seq=seq)
    out_b = _attn_b(slopes[a_heads:], diff_lq1[0][None, :], diff_lk1[0][None, :],
                    diff_lq2[0][None, :], diff_lk2[0][None, :], diff_subln_gain[0][None, :],
                    qdt, kd, vdt, batch=batch, seq=seq)

    x1, h2 = _outproj(out_a, out_b, w_o[0].astype(BF16), x2, mod3, norm2_gain[0][None, :],
                      seq=seq)
    out = _ffn(h2, w_gate[0].astype(BF16), w_up[0].astype(BF16), w_down[0].astype(BF16),
               x1, mod3, final_gain[None, :], seq=seq)
    return out.reshape(batch, seq, d)
```

```python
import functools
import math

import jax
import jax.numpy as jnp
from jax import lax
from jax.experimental import pallas as pl
from jax.experimental.pallas import tpu as pltpu

HEAD_DIM = 128
A_KV_HEADS = 2
WINDOW = 128
B_QK_DIM = 64
B_V_DIM = 2 * B_QK_DIM
N_MOD = 6
EPS = 1e-6
NEG_INF = -1e30
LAM_INIT = 0.8 - 0.6 * math.exp(-0.3 * 0)
LOG2E = math.log2(math.e)
GROUP = 6
FIXED_REF_MAX_EXCESS = 64.0
FIXED_REF_MAX_VALUE = 2.0 ** 30
ONES_ROWS = 16
N_BIAS_ROWS = 16

V7X_VMEM_LIMIT_BYTES = 56 * 1024 * 1024

BF16 = jnp.bfloat16
F32 = jnp.float32


def _cparams(semantics):
    return pltpu.CompilerParams(dimension_semantics=semantics,
                                vmem_limit_bytes=V7X_VMEM_LIMIT_BYTES)


def _rms(x, gain):
    return x * lax.rsqrt(jnp.mean(x * x, axis=-1, keepdims=True) + EPS) * gain


def _resident(shape):
    return pl.BlockSpec(shape, lambda *_: (0,) * len(shape), pipeline_mode=pl.Buffered(1))


def _ada_kernel(c_ref, w_ref, b_ref, o_ref):
    c = c_ref[...]
    sc = (c * jax.nn.sigmoid(c)).astype(BF16)
    o_ref[...] = jnp.dot(sc, w_ref[...].astype(BF16), preferred_element_type=F32) + b_ref[...]


def _ada(c_pad, w, b, tn=1024):
    rows, d = c_pad.shape
    n = w.shape[1]
    return pl.pallas_call(
        _ada_kernel,
        grid=(n // tn,),
        in_specs=[pl.BlockSpec((rows, d), lambda j: (0, 0)),
                  pl.BlockSpec((d, tn), lambda j: (0, j)),
                  pl.BlockSpec((1, tn), lambda j: (0, j))],
        out_specs=pl.BlockSpec((rows, tn), lambda j: (0, j)),
        out_shape=jax.ShapeDtypeStruct((rows, n), F32),
        compiler_params=_cparams(("arbitrary",)),
        name="ada_mod",
    )(c_pad, w, b)


def _inproj_kernel(x_ref, mod_ref, g_ref, w_ref,
                   qa_ref, ka_ref, va_ref, qdt_ref, kd_ref, vdt_ref, h_ref,
                   *, a_q, a_kv, b_w, chunk):
    x = x_ref[...]
    sh1 = mod_ref[0, 0:1, :]
    sc1 = mod_ref[0, 1:2, :]
    h_ref[...] = (_rms(x, g_ref[...]) * (1.0 + sc1) + sh1).astype(BF16)

    def proj(c0, width):
        return jnp.dot(h_ref[...], w_ref[:, c0:c0 + width], preferred_element_type=F32)

    o1 = a_q
    o2 = o1 + a_kv
    o3 = o2 + a_kv
    o4 = o3 + b_w
    o5 = o4 + b_w
    for c in range(0, a_q, chunk):
        qa_ref[:, c:c + chunk] = proj(c, chunk).astype(BF16)
    ka_ref[...] = proj(o1, a_kv).astype(BF16)
    va_ref[...] = proj(o2, a_kv).astype(BF16)
    for c in range(0, b_w, chunk):
        kd_ref[:, c:c + chunk] = proj(o4 + c, chunk).astype(BF16)
    qscale = B_QK_DIM ** -0.5 * LOG2E
    for c in range(0, b_w, chunk):
        q = proj(o3 + c, chunk) * qscale
        v = proj(o5 + c, chunk)
        for hc in range(0, chunk, B_V_DIM):
            qdt_ref[c + hc:c + hc + B_V_DIM, :] = q[:, hc:hc + B_V_DIM].T.astype(BF16)
            vdt_ref[c + hc:c + hc + B_V_DIM, :] = v[:, hc:hc + B_V_DIM].T.astype(BF16)


def _inproj(x2, mod3, gain, w_bf, *, seq, a_q, a_kv, b_w, tm=512):
    t, d = x2.shape
    n = w_bf.shape[1]
    tiles_per_batch = seq // tm
    chunk = min(512, a_q, b_w)
    kern = functools.partial(_inproj_kernel, a_q=a_q, a_kv=a_kv, b_w=b_w, chunk=chunk)
    row = lambda width: pl.BlockSpec((tm, width), lambda i: (i, 0))
    col = lambda height: pl.BlockSpec((height, tm), lambda i: (0, i))
    return pl.pallas_call(
        kern,
        grid=(t // tm,),
        in_specs=[row(d),
                  pl.BlockSpec((1, N_MOD, d), lambda i: (i // tiles_per_batch, 0, 0)),
                  pl.BlockSpec((1, d), lambda i: (0, 0)),
                  _resident((d, n))],
        out_specs=[row(a_q), row(a_kv), row(a_kv), col(b_w), row(b_w), col(b_w)],
        out_shape=[jax.ShapeDtypeStruct((t, a_q), BF16),
                   jax.ShapeDtypeStruct((t, a_kv), BF16),
                   jax.ShapeDtypeStruct((t, a_kv), BF16),
                   jax.ShapeDtypeStruct((b_w, t), BF16),
                   jax.ShapeDtypeStruct((t, b_w), BF16),
                   jax.ShapeDtypeStruct((b_w, t), BF16)],
        scratch_shapes=[pltpu.VMEM((tm, d), BF16)],
        compiler_params=_cparams(("arbitrary",)),
        name="norm1_inproj",
    )(x2, mod3, gain, w_bf)


def _attn_a_kernel(slopes_ref, sink_ref, q_ref, k_ref, v_ref, g_ref, o_ref, bias_ref, acc_ref,
                   *, tq, kw, seq, heads):
    i = pl.program_id(1)
    q0 = i * tq
    kstart = pl.multiple_of(jnp.clip(q0 - WINDOW, 0, seq - kw), WINDOW)
    group = heads // A_KV_HEADS

    @pl.when((pl.program_id(0) == 0) & (i == 0))
    def _():
        r = lax.broadcasted_iota(jnp.int32, (tq, kw), 0)
        c = lax.broadcasted_iota(jnp.int32, (tq, kw), 1)
        for case in range(3):
            dist = jnp.abs((r - c) + case * WINDOW)
            for h in range(heads):
                bias_ref[case, h] = jnp.where(dist <= WINDOW,
                                              -(slopes_ref[h] * LOG2E) * dist.astype(F32),
                                              NEG_INF)

    case = (q0 - kstart) // WINDOW
    scale = HEAD_DIM ** -0.5
    kwins = [k_ref[pl.ds(kstart, kw), kvh * HEAD_DIM:(kvh + 1) * HEAD_DIM]
             for kvh in range(A_KV_HEADS)]
    vwins = [v_ref[pl.ds(kstart, kw), kvh * HEAD_DIM:(kvh + 1) * HEAD_DIM]
             for kvh in range(A_KV_HEADS)]
    scores = [lax.dot_general(q_ref[:, h * HEAD_DIM:(h + 1) * HEAD_DIM], kwins[h // group],
                              (((1,), (1,)), ((), ())), preferred_element_type=F32)
              for h in range(heads)]
    probs = []
    inv_denoms = []
    for h in range(heads):
        s = scores[h] * (scale * LOG2E) + bias_ref[case, h]
        sink = sink_ref[h] * LOG2E
        m = jnp.maximum(jnp.max(s, axis=-1, keepdims=True), sink)
        p = jnp.exp2(s - m)
        denom = jnp.sum(p, axis=-1, keepdims=True) + jnp.exp2(sink - m)
        probs.append(p.astype(BF16))
        inv_denoms.append(1.0 / denom)
    for h in range(heads):
        acc_ref[:, h * HEAD_DIM:(h + 1) * HEAD_DIM] = inv_denoms[h] * jnp.dot(
            probs[h], vwins[h // group], preferred_element_type=F32)
    o_ref[...] = _rms(acc_ref[...], g_ref[...]).astype(BF16)


def _attn_a(slopes_a, sink, qa, ka, va, gain, *, batch, seq, tq=128):
    t, a_q = qa.shape
    a_kv = ka.shape[1]
    heads = a_q // HEAD_DIM
    kw = tq + 2 * WINDOW
    nq = seq // tq
    kern = functools.partial(_attn_a_kernel, tq=tq, kw=kw, seq=seq, heads=heads)
    smem = pl.BlockSpec(memory_space=pltpu.SMEM)
    return pl.pallas_call(
        kern,
        grid=(batch, nq),
        in_specs=[smem, smem,
                  pl.BlockSpec((tq, a_q), lambda b, i: (b * nq + i, 0)),
                  pl.BlockSpec((seq, a_kv), lambda b, i: (b, 0)),
                  pl.BlockSpec((seq, a_kv), lambda b, i: (b, 0)),
                  pl.BlockSpec((1, a_q), lambda b, i: (0, 0))],
        out_specs=pl.BlockSpec((tq, a_q), lambda b, i: (b * nq + i, 0)),
        out_shape=jax.ShapeDtypeStruct((t, a_q), BF16),
        scratch_shapes=[pltpu.VMEM((3, heads, tq, kw), F32), pltpu.VMEM((tq, a_q), F32)],
        compiler_params=_cparams(("arbitrary", "arbitrary")),
        name="attn_window_gqa",
    )(slopes_a, sink, qa, ka, va, gain)


def _attn_b_kernel(slopes_ref, lq1_ref, lk1_ref, lq2_ref, lk2_ref, g_ref, d0_ref, feat_ref,
                   qt_ref, k_ref, vt_ref, o_ref,
                   w_ref, diag_ref, s0_ref, s1_ref, m_ref, l_ref, acc_ref, vt1_ref, acc1_ref,
                   nref_ref, flag_ref, p0_ref, p1_ref,
                   *, tile, sw, seq):
    h = pl.program_id(1)
    slope2 = slopes_ref[h] * LOG2E
    lam = (jnp.exp(jnp.sum(lq1_ref[...] * lk1_ref[...], keepdims=True))
           - jnp.exp(jnp.sum(lq2_ref[...] * lk2_ref[...], keepdims=True)) + LAM_INIT)
    halves = tile // sw
    n_tiles = seq // tile

    rho = lax.broadcasted_iota(jnp.int32, (N_BIAS_ROWS, sw), 0)
    lane = lax.broadcasted_iota(jnp.int32, (N_BIAS_ROWS, sw), 1).astype(F32)
    coeff = jnp.where(rho < 3, slope2,
                      jnp.where(rho < 6, 256.0 * slope2,
                                jnp.where(rho < 9, -slope2 * lane, 0.0)))
    hi = coeff.astype(BF16).astype(F32)
    mid = (coeff - hi).astype(BF16).astype(F32)
    lo = (coeff - hi - mid).astype(BF16).astype(F32)
    level = rho % 3
    rows = jnp.where(level == 0, hi, jnp.where(level == 1, mid, lo))
    w_ref[...] = jnp.zeros_like(w_ref)
    for strip in range(2 * halves):
        cols = slice(strip * sw, (strip + 1) * sw)
        w_ref[1, B_V_DIM:B_V_DIM + N_BIAS_ROWS, cols] = rows.astype(BF16)
        w_ref[2, B_V_DIM:B_V_DIM + N_BIAS_ROWS, cols] = (-rows).astype(BF16)
    for half in range(halves):
        diag_ref[half] = -slope2 * jnp.abs(d0_ref[...] - float(half * sw))

    def norm_chunk(n, carry):
        kmax2, vmax = carry
        r0 = pl.multiple_of(n * tile, tile)
        kf = k_ref[pl.ds(r0, tile), :].astype(F32)
        kn2 = jnp.max(jnp.sum(kf * kf, axis=1, keepdims=True), axis=0, keepdims=True)
        va = jnp.abs(vt_ref[:, pl.ds(r0, tile)].astype(F32))
        vm = jnp.max(jnp.max(va, axis=1, keepdims=True), axis=0, keepdims=True)
        return jnp.maximum(kmax2, kn2), jnp.maximum(vmax, vm)

    kmax2, vmax = lax.fori_loop(0, n_tiles, norm_chunk,
                                (jnp.zeros((1, 1), F32), jnp.zeros((1, 1), F32)))
    vt1_ref[0:B_V_DIM, :] = vt_ref[...]
    extra = lax.broadcasted_iota(jnp.int32, (ONES_ROWS, seq), 0)
    vt1_ref[B_V_DIM:, :] = jnp.where(extra == 0, 1.0, 0.0).astype(BF16)

    def key_tile(i, t):
        return jnp.where(t == 0, i, jnp.where(t <= i, t - 1, t))

    def score_tile(i, t):
        j = key_tile(i, t)
        k0 = pl.multiple_of(j * tile, tile)
        widx = jnp.where(j == i, 0, jnp.where(j < i, 1, 2))
        lhs = jnp.concatenate([k_ref[pl.ds(k0, tile), :], feat_ref[...]], axis=1)
        return jnp.dot(lhs, w_ref[widx], preferred_element_type=F32)

    def scores(i, t, s_ref):
        s_ref[...] = score_tile(i, t)

    def softmax_pv(i, t, s_ref, fixed, same_tile=False):
        j = key_tile(i, t)
        k0 = pl.multiple_of(j * tile, tile)
        q0 = i * tile
        for half in range(halves):
            cst = slope2 * (k0 - q0 - half * sw).astype(F32)
            tc = 0.0 if same_tile else jnp.where(j < i, cst, -cst)
            for comp in range(2):
                cols = slice((2 * half + comp) * sw, (2 * half + comp + 1) * sw)
                t_sc = s_ref[:, cols]
                if same_tile:
                    t_sc = t_sc + diag_ref[half]
                if fixed:
                    p = jnp.exp2(t_sc - (m_ref[:, cols] - tc))
                    acc1_ref[:, cols] += jnp.dot(vt1_ref[:, pl.ds(k0, tile)], p.astype(BF16),
                                                 preferred_element_type=F32)
                else:
                    m_old = m_ref[:, cols]
                    m_new = jnp.maximum(m_old, jnp.max(t_sc, axis=0, keepdims=True) + tc)
                    alpha = jnp.exp2(m_old - m_new)
                    p = jnp.exp2(t_sc - (m_new - tc))
                    l_ref[:, cols] = alpha * l_ref[:, cols] + jnp.sum(p, axis=0, keepdims=True)
                    m_ref[:, cols] = m_new
                    pv = jnp.dot(vt_ref[:, pl.ds(k0, tile)], p.astype(BF16),
                                 preferred_element_type=F32)
                    acc_ref[:, cols] = alpha * acc_ref[:, cols] + pv

    s_refs = (s0_ref, s1_ref)
    p_refs = (p0_ref, p1_ref)

    def probabilities(i, t, p_ref):
        j = key_tile(i, t)
        k0 = j * tile
        q0 = i * tile
        s = score_tile(i, t)
        for half in range(halves):
            cst = slope2 * (k0 - q0 - half * sw).astype(F32)
            tc = jnp.where(j < i, cst, -cst)
            for comp in range(2):
                cols = slice((2 * half + comp) * sw, (2 * half + comp + 1) * sw)
                p_ref[:, cols] = jnp.exp2(s[:, cols] - (m_ref[:, cols] - tc)).astype(BF16)

    def value_update(i, t, p_ref):
        k0 = pl.multiple_of(key_tile(i, t) * tile, tile)
        acc1_ref[...] += jnp.dot(vt1_ref[:, pl.ds(k0, tile)], p_ref[...],
                                 preferred_element_type=F32)

    def start_query_tile(i):
        q0 = pl.multiple_of(i * tile, tile)
        for half in range(halves):
            qcols = pl.ds(q0 + half * sw, sw)
            c0 = 2 * half * sw
            for variant in range(3):
                w_ref[variant, 0:B_QK_DIM, c0:c0 + sw] = qt_ref[0:B_QK_DIM, qcols]
                w_ref[variant, B_QK_DIM:B_V_DIM, c0 + sw:c0 + 2 * sw] = qt_ref[B_QK_DIM:, qcols]
        s = score_tile(i, 0)
        s_refs[0][...] = s
        ref = jnp.max(s, axis=0, keepdims=True)
        nref_ref[...] = ref
        qf = w_ref[0, 0:B_V_DIM, :].astype(F32)
        upper = jnp.sqrt(jnp.sum(qf * qf, axis=0, keepdims=True) * kmax2) * 1.01 + 1.0
        ok = (jnp.max(upper - ref) <= FIXED_REF_MAX_EXCESS) & (
            jnp.max(vmax) <= FIXED_REF_MAX_VALUE)
        flag_ref[0] = ok.astype(jnp.int32)

    def finish_query_tile(i):
        q0 = pl.multiple_of(i * tile, tile)
        o = acc_ref[...] / l_ref[...]
        for half in range(halves):
            c0 = 2 * half * sw
            od = (o[:, c0:c0 + sw] - lam * o[:, c0 + sw:c0 + 2 * sw]).T
            o_ref[pl.ds(q0 + half * sw, sw), :] = (
                _rms(od, g_ref[...]) * (1.0 - LAM_INIT)).astype(BF16)

    def key_tile_pipeline(i, fixed):
        def produce(t, parity):
            if fixed:
                probabilities(i, t, p_refs[parity])
            else:
                scores(i, t, s_refs[parity])

        def consume(t, parity):
            if fixed:
                value_update(i, t, p_refs[parity])
            else:
                softmax_pv(i, t, s_refs[parity], False)

        produce(1, 1)
        if fixed:
            finish_query_tile(jnp.maximum(i - 1, 0))
        softmax_pv(i, 0, s_refs[0], fixed, same_tile=True)

        def group(n, c2):
            for u in range(GROUP):
                produce(GROUP * n + u + 2, u % 2)
                consume(GROUP * n + u + 1, (u + 1) % 2)
            return c2

        n_groups = (n_tiles - 2) // GROUP
        lax.fori_loop(0, n_groups, group, 0)
        for t in range(n_groups * GROUP + 1, n_tiles):
            if t + 1 < n_tiles:
                produce(t + 1, (t + 1) % 2)
            else:
                assert (t - 1) % 2 == 0
                start_query_tile(jnp.minimum(i + 1, n_tiles - 1))
            consume(t, t % 2)

    def q_body(i, carry):
        use_fixed = flag_ref[0]

        @pl.when(use_fixed == 1)
        def _():
            m_ref[...] = nref_ref[...]
            acc1_ref[...] = jnp.zeros_like(acc1_ref)
            key_tile_pipeline(i, True)
            acc_ref[...] = acc1_ref[0:B_V_DIM, :]
            l_ref[...] = acc1_ref[B_V_DIM:B_V_DIM + 1, :]

        @pl.when(use_fixed != 1)
        def _():
            finish_query_tile(jnp.maximum(i - 1, 0))
            m_ref[...] = jnp.full_like(m_ref, NEG_INF)
            l_ref[...] = jnp.zeros_like(l_ref)
            acc_ref[...] = jnp.zeros_like(acc_ref)
            key_tile_pipeline(i, False)

        return carry

    acc_ref[...] = jnp.zeros_like(acc_ref)
    l_ref[...] = jnp.ones_like(l_ref)
    start_query_tile(0)
    lax.fori_loop(0, n_tiles, q_body, 0)
    finish_query_tile(n_tiles - 1)


def _attn_b(slopes_b, lq1, lk1, lq2, lk2, gain, qdt, kd, vdt, *, batch, seq, tile=512, sw=256):
    b_w, t = qdt.shape
    heads = b_w // B_V_DIM
    tile = min(tile, seq // 2)
    assert seq % (2 * tile) == 0 and tile % sw == 0
    r = lax.broadcasted_iota(jnp.int32, (tile, sw), 0)
    c = lax.broadcasted_iota(jnp.int32, (tile, sw), 1)
    d0 = (r - c).astype(F32)
    rk = jnp.arange(tile, dtype=jnp.int32)[:, None]
    fcol = jnp.arange(B_V_DIM, dtype=jnp.int32)[None, :]
    feat = jnp.where(fcol < 3, rk % 256,
                     jnp.where(fcol < 6, rk // 256, jnp.where(fcol < 9, 1, 0))).astype(BF16)
    kern = functools.partial(_attn_b_kernel, tile=tile, sw=sw, seq=seq)
    smem = pl.BlockSpec(memory_space=pltpu.SMEM)
    vec = lambda n: pl.BlockSpec((1, n), lambda b, h: (0, 0))
    tposed = pl.BlockSpec((B_V_DIM, seq), lambda b, h: (h, b))
    natural = pl.BlockSpec((seq, B_V_DIM), lambda b, h: (b, h))
    return pl.pallas_call(
        kern,
        grid=(batch, heads),
        in_specs=[smem, vec(B_QK_DIM), vec(B_QK_DIM), vec(B_QK_DIM), vec(B_QK_DIM),
                  vec(B_V_DIM),
                  pl.BlockSpec((tile, sw), lambda b, h: (0, 0)),
                  pl.BlockSpec((tile, B_V_DIM), lambda b, h: (0, 0)),
                  tposed, natural, tposed],
        out_specs=natural,
        out_shape=jax.ShapeDtypeStruct((t, b_w), BF16),
        scratch_shapes=[pltpu.VMEM((3, 2 * B_V_DIM, 2 * tile), BF16),
                        pltpu.VMEM((tile // sw, tile, sw), F32),
                        pltpu.VMEM((tile, 2 * tile), F32),
                        pltpu.VMEM((tile, 2 * tile), F32),
                        pltpu.VMEM((1, 2 * tile), F32),
                        pltpu.VMEM((1, 2 * tile), F32),
                        pltpu.VMEM((B_V_DIM, 2 * tile), F32),
                        pltpu.VMEM((B_V_DIM + ONES_ROWS, seq), BF16),
                        pltpu.VMEM((B_V_DIM + ONES_ROWS, 2 * tile), F32),
                        pltpu.VMEM((1, 2 * tile), F32),
                        pltpu.SMEM((1,), jnp.int32),
                        pltpu.VMEM((tile, 2 * tile), BF16),
                        pltpu.VMEM((tile, 2 * tile), BF16)],
        compiler_params=_cparams(("arbitrary", "arbitrary")),
        name="attn_diff",
    )(slopes_b, lq1, lk1, lq2, lk2, gain, d0, feat, qdt, kd, vdt)


def _outproj_kernel(a_ref, b_ref, w_ref, x_ref, mod_ref, g_ref, x1_ref, h2_ref, *, a_w):
    mix = (jnp.dot(a_ref[...], w_ref[0:a_w, :], preferred_element_type=F32)
           + jnp.dot(b_ref[...], w_ref[a_w:, :], preferred_element_type=F32))
    g1 = mod_ref[0, 2:3, :]
    sh2 = mod_ref[0, 3:4, :]
    sc2 = mod_ref[0, 4:5, :]
    x1 = x_ref[...] + g1 * mix
    x1_ref[...] = x1
    h2_ref[...] = (_rms(x1, g_ref[...]) * (1.0 + sc2) + sh2).astype(BF16)


def _outproj(out_a, out_b, w_bf, x2, mod3, gain, *, seq, tm=512):
    t, d = x2.shape
    a_w = out_a.shape[1]
    b_w = out_b.shape[1]
    tiles_per_batch = seq // tm
    row = lambda width: pl.BlockSpec((tm, width), lambda i: (i, 0))
    return pl.pallas_call(
        functools.partial(_outproj_kernel, a_w=a_w),
        grid=(t // tm,),
        in_specs=[row(a_w), row(b_w), _resident((a_w + b_w, d)), row(d),
                  pl.BlockSpec((1, N_MOD, d), lambda i: (i // tiles_per_batch, 0, 0)),
                  pl.BlockSpec((1, d), lambda i: (0, 0))],
        out_specs=[row(d), row(d)],
        out_shape=[jax.ShapeDtypeStruct((t, d), F32), jax.ShapeDtypeStruct((t, d), BF16)],
        compiler_params=_cparams(("arbitrary",)),
        name="outproj_norm2",
    )(out_a, out_b, w_bf, x2, mod3, gain)


def _ffn_kernel(h_ref, wg_ref, wu_ref, wd_ref, x1_ref, mod_ref, fg_ref, o_ref):
    f = pl.program_id(1)

    @pl.when(f == 0)
    def _():
        o_ref[...] = jnp.zeros_like(o_ref)

    h = h_ref[...]
    g = jnp.dot(h, wg_ref[...], preferred_element_type=F32)
    u = jnp.dot(h, wu_ref[...], preferred_element_type=F32)
    a = (g * jax.nn.sigmoid(g) * u).astype(BF16)
    o_ref[...] += jnp.dot(a, wd_ref[...], preferred_element_type=F32)

    @pl.when(f == pl.num_programs(1) - 1)
    def _():
        g2 = mod_ref[0, 5:6, :]
        o_ref[...] = _rms(x1_ref[...] + g2 * o_ref[...], fg_ref[...])


def _ffn(h2, wg, wu, wd, x1, mod3, final_gain, *, seq, tm=512, tf=512):
    t, d = h2.shape
    ff = wg.shape[1]
    tm = min(tm, seq)
    if ff % tf:
        tf = 256
    assert ff % tf == 0 and t % tm == 0 and seq % tm == 0
    tiles_per_batch = seq // tm
    return pl.pallas_call(
        _ffn_kernel,
        grid=(t // tm, ff // tf),
        in_specs=[pl.BlockSpec((tm, d), lambda i, f: (i, 0)),
                  pl.BlockSpec((d, tf), lambda i, f: (0, f)),
                  pl.BlockSpec((d, tf), lambda i, f: (0, f)),
                  pl.BlockSpec((tf, d), lambda i, f: (f, 0)),
                  pl.BlockSpec((tm, d), lambda i, f: (i, 0)),
                  pl.BlockSpec((1, N_MOD, d), lambda i, f: (i // tiles_per_batch, 0, 0)),
                  pl.BlockSpec((1, d), lambda i, f: (0, 0))],
        out_specs=pl.BlockSpec((tm, d), lambda i, f: (i, 0)),
        out_shape=jax.ShapeDtypeStruct((t, d), F32),
        compiler_params=_cparams(("arbitrary", "arbitrary")),
        name="swiglu_ffn_final_norm",
    )(h2, wg, wu, wd, x1, mod3, final_gain)


def kernel(x, c, w_ada, b_ada, norm1_gain, w_in, a_sink, a_out_gain, diff_lq1, diff_lk1,
           diff_lq2, diff_lk2, diff_subln_gain, w_o, norm2_gain, w_gate, w_up, w_down,
           final_gain):
    batch, seq, d = x.shape
    assert w_ada.shape[0] == 1, "single-layer block"
    a_w = d // 2
    b_w = d - a_w
    a_heads = a_w // HEAD_DIM
    b_heads = b_w // B_V_DIM
    a_kv = A_KV_HEADS * HEAD_DIM
    n_heads = a_heads + b_heads
    slopes = 2.0 ** (-8.0 * jnp.arange(1, n_heads + 1, dtype=F32) / n_heads)

    rows = 8
    c_pad = jnp.zeros((rows, d), F32).at[:batch].set(c)
    mod = _ada(c_pad, w_ada[0], b_ada[0][None, :])[:batch]
    mod3 = mod.reshape(batch, N_MOD, d)

    x2 = x.reshape(batch * seq, d)
    qa, ka, va, qdt, kd, vdt = _inproj(
        x2, mod3, norm1_gain[0][None, :], w_in[0].astype(BF16),
        seq=seq, a_q=a_w, a_kv=a_kv, b_w=b_w)

    out_a = _attn_a(slopes[:a_heads], a_sink[0].astype(F32), qa, ka, va,
                    a_out_gain[0][None, :], batch=batch, seq=seq)
    out_b = _attn_b(slopes[a_heads:], diff_lq1[0][None, :], diff_lk1[0][None, :],
                    diff_lq2[0][None, :], diff_lk2[0][None, :], diff_subln_gain[0][None, :],
                    qdt, kd, vdt, batch=batch, seq=seq)

    x1, h2 = _outproj(out_a, out_b, w_o[0].astype(BF16), x2, mod3, norm2_gain[0][None, :],
                      seq=seq)
    out = _ffn(h2, w_gate[0].astype(BF16), w_up[0].astype(BF16), w_down[0].astype(BF16),
               x1, mod3, final_gain[None, :], seq=seq)
    return out.reshape(batch, seq, d)
```

```python
import functools
import math

import jax
import jax.numpy as jnp
from jax import lax
from jax.experimental import pallas as pl
from jax.experimental.pallas import tpu as pltpu

HEAD_DIM = 128
A_KV_HEADS = 2
WINDOW = 128
B_QK_DIM = 64
B_V_DIM = 2 * B_QK_DIM
N_MOD = 6
EPS = 1e-6
NEG_INF = -1e30
LAM_INIT = 0.8 - 0.6 * math.exp(-0.3 * 0)
LOG2E = math.log2(math.e)
GROUP = 6
FIXED_REF_MAX_EXCESS = 64.0
FIXED_REF_MAX_VALUE = 2.0 ** 30
ONES_ROWS = 16
N_BIAS_ROWS = 16

V7X_VMEM_LIMIT_BYTES = 56 * 1024 * 1024

BF16 = jnp.bfloat16
F32 = jnp.float32


def _cparams(semantics):
    return pltpu.CompilerParams(dimension_semantics=semantics,
                                vmem_limit_bytes=V7X_VMEM_LIMIT_BYTES)


def _rms(x, gain):
    return x * lax.rsqrt(jnp.mean(x * x, axis=-1, keepdims=True) + EPS) * gain


def _resident(shape):
    return pl.BlockSpec(shape, lambda *_: (0,) * len(shape), pipeline_mode=pl.Buffered(1))


def _ada_kernel(c_ref, w_ref, b_ref, o_ref):
    c = c_ref[...]
    sc = (c * jax.nn.sigmoid(c)).astype(BF16)
    o_ref[...] = jnp.dot(sc, w_ref[...].astype(BF16), preferred_element_type=F32) + b_ref[...]


def _ada(c_pad, w, b, tn=1024):
    rows, d = c_pad.shape
    n = w.shape[1]
    return pl.pallas_call(
        _ada_kernel,
        grid=(n // tn,),
        in_specs=[pl.BlockSpec((rows, d), lambda j: (0, 0)),
                  pl.BlockSpec((d, tn), lambda j: (0, j)),
                  pl.BlockSpec((1, tn), lambda j: (0, j))],
        out_specs=pl.BlockSpec((rows, tn), lambda j: (0, j)),
        out_shape=jax.ShapeDtypeStruct((rows, n), F32),
        compiler_params=_cparams(("arbitrary",)),
        name="ada_mod",
    )(c_pad, w, b)


def _inproj_kernel(x_ref, mod_ref, g_ref, w_ref,
                   qa_ref, ka_ref, va_ref, qdt_ref, kd_ref, vdt_ref, h_ref,
                   *, a_q, a_kv, b_w, chunk):
    x = x_ref[...]
    sh1 = mod_ref[0, 0:1, :]
    sc1 = mod_ref[0, 1:2, :]
    h_ref[...] = (_rms(x, g_ref[...]) * (1.0 + sc1) + sh1).astype(BF16)

    def proj(c0, width):
        return jnp.dot(h_ref[...], w_ref[:, c0:c0 + width], preferred_element_type=F32)

    o1 = a_q
    o2 = o1 + a_kv
    o3 = o2 + a_kv
    o4 = o3 + b_w
    o5 = o4 + b_w
    for c in range(0, a_q, chunk):
        qa_ref[:, c:c + chunk] = proj(c, chunk).astype(BF16)
    ka_ref[...] = proj(o1, a_kv).astype(BF16)
    va_ref[...] = proj(o2, a_kv).astype(BF16)
    for c in range(0, b_w, chunk):
        kd_ref[:, c:c + chunk] = proj(o4 + c, chunk).astype(BF16)
    qscale = B_QK_DIM ** -0.5 * LOG2E
    for c in range(0, b_w, chunk):
        q = proj(o3 + c, chunk) * qscale
        v = proj(o5 + c, chunk)
        for hc in range(0, chunk, B_V_DIM):
            qdt_ref[c + hc:c + hc + B_V_DIM, :] = q[:, hc:hc + B_V_DIM].T.astype(BF16)
            vdt_ref[c + hc:c + hc + B_V_DIM, :] = v[:, hc:hc + B_V_DIM].T.astype(BF16)


def _inproj(x2, mod3, gain, w_bf, *, seq, a_q, a_kv, b_w, tm=512):
    t, d = x2.shape
    n = w_bf.shape[1]
    tiles_per_batch = seq // tm
    chunk = min(512, a_q, b_w)
    kern = functools.partial(_inproj_kernel, a_q=a_q, a_kv=a_kv, b_w=b_w, chunk=chunk)
    row = lambda width: pl.BlockSpec((tm, width), lambda i: (i, 0))
    col = lambda height: pl.BlockSpec((height, tm), lambda i: (0, i))
    return pl.pallas_call(
        kern,
        grid=(t // tm,),
        in_specs=[row(d),
                  pl.BlockSpec((1, N_MOD, d), lambda i: (i // tiles_per_batch, 0, 0)),
                  pl.BlockSpec((1, d), lambda i: (0, 0)),
                  _resident((d, n))],
        out_specs=[row(a_q), row(a_kv), row(a_kv), col(b_w), row(b_w), col(b_w)],
        out_shape=[jax.ShapeDtypeStruct((t, a_q), BF16),
                   jax.ShapeDtypeStruct((t, a_kv), BF16),
                   jax.ShapeDtypeStruct((t, a_kv), BF16),
                   jax.ShapeDtypeStruct((b_w, t), BF16),
                   jax.ShapeDtypeStruct((t, b_w), BF16),
                   jax.ShapeDtypeStruct((b_w, t), BF16)],
        scratch_shapes=[pltpu.VMEM((tm, d), BF16)],
        compiler_params=_cparams(("arbitrary",)),
        name="norm1_inproj",
    )(x2, mod3, gain, w_bf)


def _attn_a_kernel(slopes_ref, sink_ref, q_ref, k_ref, v_ref, g_ref, o_ref, bias_ref, acc_ref,
                   *, tq, kw, seq, heads):
    i = pl.program_id(1)
    q0 = i * tq
    kstart = pl.multiple_of(jnp.clip(q0 - WINDOW, 0, seq - kw), WINDOW)
    group = heads // A_KV_HEADS

    @pl.when((pl.program_id(0) == 0) & (i == 0))
    def _():
        r = lax.broadcasted_iota(jnp.int32, (tq, kw), 0)
        c = lax.broadcasted_iota(jnp.int32, (tq, kw), 1)
        for case in range(3):
            dist = jnp.abs((r - c) + case * WINDOW)
            for h in range(heads):
                bias_ref[case, h] = jnp.where(dist <= WINDOW,
                                              -(slopes_ref[h] * LOG2E) * dist.astype(F32),
                                              NEG_INF)

    case = (q0 - kstart) // WINDOW
    scale = HEAD_DIM ** -0.5
    kwins = [k_ref[pl.ds(kstart, kw), kvh * HEAD_DIM:(kvh + 1) * HEAD_DIM]
             for kvh in range(A_KV_HEADS)]
    vwins = [v_ref[pl.ds(kstart, kw), kvh * HEAD_DIM:(kvh + 1) * HEAD_DIM]
             for kvh in range(A_KV_HEADS)]
    scores = [lax.dot_general(q_ref[:, h * HEAD_DIM:(h + 1) * HEAD_DIM], kwins[h // group],
                              (((1,), (1,)), ((), ())), preferred_element_type=F32)
              for h in range(heads)]
    probs = []
    inv_denoms = []
    for h in range(heads):
        s = scores[h] * (scale * LOG2E) + bias_ref[case, h]
        sink = sink_ref[h] * LOG2E
        m = jnp.maximum(jnp.max(s, axis=-1, keepdims=True), sink)
        p = jnp.exp2(s - m)
        denom = jnp.sum(p, axis=-1, keepdims=True) + jnp.exp2(sink - m)
        probs.append(p.astype(BF16))
        inv_denoms.append(1.0 / denom)
    for h in range(heads):
        acc_ref[:, h * HEAD_DIM:(h + 1) * HEAD_DIM] = inv_denoms[h] * jnp.dot(
            probs[h], vwins[h // group], preferred_element_type=F32)
    o_ref[...] = _rms(acc_ref[...], g_ref[...]).astype(BF16)


def _attn_a(slopes_a, sink, qa, ka, va, gain, *, batch, seq, tq=128):
    t, a_q = qa.shape
    a_kv = ka.shape[1]
    heads = a_q // HEAD_DIM
    kw = tq + 2 * WINDOW
    nq = seq // tq
    kern = functools.partial(_attn_a_kernel, tq=tq, kw=kw, seq=seq, heads=heads)
    smem = pl.BlockSpec(memory_space=pltpu.SMEM)
    return pl.pallas_call(
        kern,
        grid=(batch, nq),
        in_specs=[smem, smem,
                  pl.BlockSpec((tq, a_q), lambda b, i: (b * nq + i, 0)),
                  pl.BlockSpec((seq, a_kv), lambda b, i: (b, 0)),
                  pl.BlockSpec((seq, a_kv), lambda b, i: (b, 0)),
                  pl.BlockSpec((1, a_q), lambda b, i: (0, 0))],
        out_specs=pl.BlockSpec((tq, a_q), lambda b, i: (b * nq + i, 0)),
        out_shape=jax.ShapeDtypeStruct((t, a_q), BF16),
        scratch_shapes=[pltpu.VMEM((3, heads, tq, kw), F32), pltpu.VMEM((tq, a_q), F32)],
        compiler_params=_cparams(("arbitrary", "arbitrary")),
        name="attn_window_gqa",
    )(slopes_a, sink, qa, ka, va, gain)


def _attn_b_kernel(slopes_ref, lq1_ref, lk1_ref, lq2_ref, lk2_ref, g_ref, d0_ref, feat_ref,
                   qt_ref, k_ref, vt_ref, o_ref,
                   w_ref, diag_ref, s0_ref, s1_ref, m_ref, l_ref, acc_ref, vt1_ref, acc1_ref,
                   nref_ref, flag_ref, p0_ref, p1_ref,
                   *, tile, sw, seq):
    h = pl.program_id(1)
    slope2 = slopes_ref[h] * LOG2E
    lam = (jnp.exp(jnp.sum(lq1_ref[...] * lk1_ref[...], keepdims=True))
           - jnp.exp(jnp.sum(lq2_ref[...] * lk2_ref[...], keepdims=True)) + LAM_INIT)
    halves = tile // sw
    n_tiles = seq // tile

    rho = lax.broadcasted_iota(jnp.int32, (N_BIAS_ROWS, sw), 0)
    lane = lax.broadcasted_iota(jnp.int32, (N_BIAS_ROWS, sw), 1).astype(F32)
    coeff = jnp.where(rho < 3, slope2,
                      jnp.where(rho < 6, 256.0 * slope2,
                                jnp.where(rho < 9, -slope2 * lane, 0.0)))
    hi = coeff.astype(BF16).astype(F32)
    mid = (coeff - hi).astype(BF16).astype(F32)
    lo = (coeff - hi - mid).astype(BF16).astype(F32)
    level = rho % 3
    rows = jnp.where(level == 0, hi, jnp.where(level == 1, mid, lo))
    w_ref[...] = jnp.zeros_like(w_ref)
    for strip in range(2 * halves):
        cols = slice(strip * sw, (strip + 1) * sw)
        w_ref[1, B_V_DIM:B_V_DIM + N_BIAS_ROWS, cols] = rows.astype(BF16)
        w_ref[2, B_V_DIM:B_V_DIM + N_BIAS_ROWS, cols] = (-rows).astype(BF16)
    for half in range(halves):
        diag_ref[half] = -slope2 * jnp.abs(d0_ref[...] - float(half * sw))

    def norm_chunk(n, carry):
        kmax2, vmax = carry
        r0 = pl.multiple_of(n * tile, tile)
        kf = k_ref[pl.ds(r0, tile), :].astype(F32)
        kn2 = jnp.max(jnp.sum(kf * kf, axis=1, keepdims=True), axis=0, keepdims=True)
        va = jnp.abs(vt_ref[:, pl.ds(r0, tile)].astype(F32))
        vm = jnp.max(jnp.max(va, axis=1, keepdims=True), axis=0, keepdims=True)
        return jnp.maximum(kmax2, kn2), jnp.maximum(vmax, vm)

    kmax2, vmax = lax.fori_loop(0, n_tiles, norm_chunk,
                                (jnp.zeros((1, 1), F32), jnp.zeros((1, 1), F32)))
    vt1_ref[0:B_V_DIM, :] = vt_ref[...]
    extra = lax.broadcasted_iota(jnp.int32, (ONES_ROWS, seq), 0)
    vt1_ref[B_V_DIM:, :] = jnp.where(extra == 0, 1.0, 0.0).astype(BF16)

    def key_tile(i, t):
        return jnp.where(t == 0, i, jnp.where(t <= i, t - 1, t))

    def score_tile(i, t):
        j = key_tile(i, t)
        k0 = pl.multiple_of(j * tile, tile)
        widx = jnp.where(j == i, 0, jnp.where(j < i, 1, 2))
        lhs = jnp.concatenate([k_ref[pl.ds(k0, tile), :], feat_ref[...]], axis=1)
        return jnp.dot(lhs, w_ref[widx], preferred_element_type=F32)

    def scores(i, t, s_ref):
        s_ref[...] = score_tile(i, t)

    def softmax_pv(i, t, s_ref, same_tile=False):
        j = key_tile(i, t)
        k0 = pl.multiple_of(j * tile, tile)
        q0 = i * tile
        for half in range(halves):
            cst = slope2 * (k0 - q0 - half * sw).astype(F32)
            tc = 0.0 if same_tile else jnp.where(j < i, cst, -cst)
            for comp in range(2):
                cols = slice((2 * half + comp) * sw, (2 * half + comp + 1) * sw)
                t_sc = s_ref[:, cols]
                if same_tile:
                    t_sc = t_sc + diag_ref[half]
                m_old = m_ref[:, cols]
                m_new = jnp.maximum(m_old, jnp.max(t_sc, axis=0, keepdims=True) + tc)
                alpha = jnp.exp2(m_old - m_new)
                p = jnp.exp2(t_sc - (m_new - tc))
                l_ref[:, cols] = alpha * l_ref[:, cols] + jnp.sum(p, axis=0, keepdims=True)
                m_ref[:, cols] = m_new
                pv = jnp.dot(vt_ref[:, pl.ds(k0, tile)], p.astype(BF16),
                             preferred_element_type=F32)
                acc_ref[:, cols] = alpha * acc_ref[:, cols] + pv

    s_refs = (s0_ref, s1_ref)
    p_refs = (p0_ref, p1_ref)

    def probabilities(i, t, p_ref):
        j = key_tile(i, t)
        k0 = j * tile
        q0 = i * tile
        s = score_tile(i, t)
        for half in range(halves):
            cst = slope2 * (k0 - q0 - half * sw).astype(F32)
            tc = jnp.where(j < i, cst, -cst)
            for comp in range(2):
                cols = slice((2 * half + comp) * sw, (2 * half + comp + 1) * sw)
                p_ref[:, cols] = jnp.exp2(s[:, cols] - (m_ref[:, cols] - tc)).astype(BF16)

    def value_update(i, t, p_ref):
        k0 = pl.multiple_of(key_tile(i, t) * tile, tile)
        acc1_ref[...] += jnp.dot(vt1_ref[:, pl.ds(k0, tile)], p_ref[...],
                                 preferred_element_type=F32)

    def start_query_tile(i):
        q0 = pl.multiple_of(i * tile, tile)
        for half in range(halves):
            qcols = pl.ds(q0 + half * sw, sw)
            c0 = 2 * half * sw
            for variant in range(3):
                w_ref[variant, 0:B_QK_DIM, c0:c0 + sw] = qt_ref[0:B_QK_DIM, qcols]
                w_ref[variant, B_QK_DIM:B_V_DIM, c0 + sw:c0 + 2 * sw] = qt_ref[B_QK_DIM:, qcols]
        s = score_tile(i, 0)
        s_refs[0][...] = s
        ref = jnp.max(s, axis=0, keepdims=True)
        nref_ref[...] = ref
        for half in range(halves):
            for comp in range(2):
                cols = slice((2 * half + comp) * sw, (2 * half + comp + 1) * sw)
                p_refs[0][:, cols] = jnp.exp2(
                    s[:, cols] + diag_ref[half] - ref[:, cols]).astype(BF16)
        qf = w_ref[0, 0:B_V_DIM, :].astype(F32)
        upper = jnp.sqrt(jnp.sum(qf * qf, axis=0, keepdims=True) * kmax2) * 1.01 + 1.0
        ok = (jnp.max(upper - ref) <= FIXED_REF_MAX_EXCESS) & (
            jnp.max(vmax) <= FIXED_REF_MAX_VALUE)
        flag_ref[0] = ok.astype(jnp.int32)

    def finish_query_tile(i):
        q0 = pl.multiple_of(i * tile, tile)
        o = acc_ref[...] / l_ref[...]
        for half in range(halves):
            c0 = 2 * half * sw
            od = (o[:, c0:c0 + sw] - lam * o[:, c0 + sw:c0 + 2 * sw]).T
            o_ref[pl.ds(q0 + half * sw, sw), :] = (
                _rms(od, g_ref[...]) * (1.0 - LAM_INIT)).astype(BF16)

    def key_tile_pipeline(i, fixed):
        def produce(t, parity):
            if fixed:
                probabilities(i, t, p_refs[parity])
            else:
                scores(i, t, s_refs[parity])

        def consume(t, parity):
            if fixed:
                value_update(i, t, p_refs[parity])
            else:
                softmax_pv(i, t, s_refs[parity])

        produce(1, 1)
        if fixed:
            finish_query_tile(jnp.maximum(i - 1, 0))
            value_update(i, 0, p_refs[0])
        else:
            softmax_pv(i, 0, s_refs[0], same_tile=True)

        def group(n, c2):
            for u in range(GROUP):
                produce(GROUP * n + u + 2, u % 2)
                consume(GROUP * n + u + 1, (u + 1) % 2)
            return c2

        n_groups = (n_tiles - 2) // GROUP
        lax.fori_loop(0, n_groups, group, 0)
        for t in range(n_groups * GROUP + 1, n_tiles):
            if t + 1 < n_tiles:
                produce(t + 1, (t + 1) % 2)
            else:
                assert (t - 1) % 2 == 0
                start_query_tile(jnp.minimum(i + 1, n_tiles - 1))
            consume(t, t % 2)

    def q_body(i, carry):
        use_fixed = flag_ref[0]

        @pl.when(use_fixed == 1)
        def _():
            m_ref[...] = nref_ref[...]
            acc1_ref[...] = jnp.zeros_like(acc1_ref)
            key_tile_pipeline(i, True)
            acc_ref[...] = acc1_ref[0:B_V_DIM, :]
            l_ref[...] = acc1_ref[B_V_DIM:B_V_DIM + 1, :]

        @pl.when(use_fixed != 1)
        def _():
            finish_query_tile(jnp.maximum(i - 1, 0))
            m_ref[...] = jnp.full_like(m_ref, NEG_INF)
            l_ref[...] = jnp.zeros_like(l_ref)
            acc_ref[...] = jnp.zeros_like(acc_ref)
            key_tile_pipeline(i, False)

        return carry

    acc_ref[...] = jnp.zeros_like(acc_ref)
    l_ref[...] = jnp.ones_like(l_ref)
    start_query_tile(0)
    lax.fori_loop(0, n_tiles, q_body, 0)
    finish_query_tile(n_tiles - 1)


def _attn_b(slopes_b, lq1, lk1, lq2, lk2, gain, qdt, kd, vdt, *, batch, seq, tile=512, sw=256):
    b_w, t = qdt.shape
    heads = b_w // B_V_DIM
    tile = min(tile, seq // 2)
    assert seq % (2 * tile) == 0 and tile % sw == 0
    r = lax.broadcasted_iota(jnp.int32, (tile, sw), 0)
    c = lax.broadcasted_iota(jnp.int32, (tile, sw), 1)
    d0 = (r - c).astype(F32)
    rk = jnp.arange(tile, dtype=jnp.int32)[:, None]
    fcol = jnp.arange(B_V_DIM, dtype=jnp.int32)[None, :]
    feat = jnp.where(fcol < 3, rk % 256,
                     jnp.where(fcol < 6, rk // 256, jnp.where(fcol < 9, 1, 0))).astype(BF16)
    kern = functools.partial(_attn_b_kernel, tile=tile, sw=sw, seq=seq)
    smem = pl.BlockSpec(memory_space=pltpu.SMEM)
    vec = lambda n: pl.BlockSpec((1, n), lambda b, h: (0, 0))
    tposed = pl.BlockSpec((B_V_DIM, seq), lambda b, h: (h, b))
    natural = pl.BlockSpec((seq, B_V_DIM), lambda b, h: (b, h))
    return pl.pallas_call(
        kern,
        grid=(batch, heads),
        in_specs=[smem, vec(B_QK_DIM), vec(B_QK_DIM), vec(B_QK_DIM), vec(B_QK_DIM),
                  vec(B_V_DIM),
                  pl.BlockSpec((tile, sw), lambda b, h: (0, 0)),
                  pl.BlockSpec((tile, B_V_DIM), lambda b, h: (0, 0)),
                  tposed, natural, tposed],
        out_specs=natural,
        out_shape=jax.ShapeDtypeStruct((t, b_w), BF16),
        scratch_shapes=[pltpu.VMEM((3, 2 * B_V_DIM, 2 * tile), BF16),
                        pltpu.VMEM((tile // sw, tile, sw), F32),
                        pltpu.VMEM((tile, 2 * tile), F32),
                        pltpu.VMEM((tile, 2 * tile), F32),
                        pltpu.VMEM((1, 2 * tile), F32),
                        pltpu.VMEM((1, 2 * tile), F32),
                        pltpu.VMEM((B_V_DIM, 2 * tile), F32),
                        pltpu.VMEM((B_V_DIM + ONES_ROWS, seq), BF16),
                        pltpu.VMEM((B_V_DIM + ONES_ROWS, 2 * tile), F32),
                        pltpu.VMEM((1, 2 * tile), F32),
                        pltpu.SMEM((1,), jnp.int32),
                        pltpu.VMEM((tile, 2 * tile), BF16),
                        pltpu.VMEM((tile, 2 * tile), BF16)],
        compiler_params=_cparams(("arbitrary", "arbitrary")),
        name="attn_diff",
    )(slopes_b, lq1, lk1, lq2, lk2, gain, d0, feat, qdt, kd, vdt)


def _outproj_kernel(a_ref, b_ref, w_ref, x_ref, mod_ref, g_ref, x1_ref, h2_ref, *, a_w):
    mix = (jnp.dot(a_ref[...], w_ref[0:a_w, :], preferred_element_type=F32)
           + jnp.dot(b_ref[...], w_ref[a_w:, :], preferred_element_type=F32))
    g1 = mod_ref[0, 2:3, :]
    sh2 = mod_ref[0, 3:4, :]
    sc2 = mod_ref[0, 4:5, :]
    x1 = x_ref[...] + g1 * mix
    x1_ref[...] = x1
    h2_ref[...] = (_rms(x1, g_ref[...]) * (1.0 + sc2) + sh2).astype(BF16)


def _outproj(out_a, out_b, w_bf, x2, mod3, gain, *, seq, tm=512):
    t, d = x2.shape
    a_w = out_a.shape[1]
    b_w = out_b.shape[1]
    tiles_per_batch = seq // tm
    row = lambda width: pl.BlockSpec((tm, width), lambda i: (i, 0))
    return pl.pallas_call(
        functools.partial(_outproj_kernel, a_w=a_w),
        grid=(t // tm,),
        in_specs=[row(a_w), row(b_w), _resident((a_w + b_w, d)), row(d),
                  pl.BlockSpec((1, N_MOD, d), lambda i: (i // tiles_per_batch, 0, 0)),
                  pl.BlockSpec((1, d), lambda i: (0, 0))],
        out_specs=[row(d), row(d)],
        out_shape=[jax.ShapeDtypeStruct((t, d), F32), jax.ShapeDtypeStruct((t, d), BF16)],
        compiler_params=_cparams(("arbitrary",)),
        name="outproj_norm2",
    )(out_a, out_b, w_bf, x2, mod3, gain)


def _ffn_kernel(h_ref, wg_ref, wu_ref, wd_ref, x1_ref, mod_ref, fg_ref, o_ref):
    f = pl.program_id(1)

    @pl.when(f == 0)
    def _():
        o_ref[...] = jnp.zeros_like(o_ref)

    h = h_ref[...]
    g = jnp.dot(h, wg_ref[...], preferred_element_type=F32)
    u = jnp.dot(h, wu_ref[...], preferred_element_type=F32)
    a = (g * jax.nn.sigmoid(g) * u).astype(BF16)
    o_ref[...] += jnp.dot(a, wd_ref[...], preferred_element_type=F32)

    @pl.when(f == pl.num_programs(1) - 1)
    def _():
        g2 = mod_ref[0, 5:6, :]
        o_ref[...] = _rms(x1_ref[...] + g2 * o_ref[...], fg_ref[...])


def _ffn(h2, wg, wu, wd, x1, mod3, final_gain, *, seq, tm=512, tf=512):
    t, d = h2.shape
    ff = wg.shape[1]
    tm = min(tm, seq)
    if ff % tf:
        tf = 256
    assert ff % tf == 0 and t % tm == 0 and seq % tm == 0
    tiles_per_batch = seq // tm
    return pl.pallas_call(
        _ffn_kernel,
        grid=(t // tm, ff // tf),
        in_specs=[pl.BlockSpec((tm, d), lambda i, f: (i, 0)),
                  pl.BlockSpec((d, tf), lambda i, f: (0, f)),
                  pl.BlockSpec((d, tf), lambda i, f: (0, f)),
                  pl.BlockSpec((tf, d), lambda i, f: (f, 0)),
                  pl.BlockSpec((tm, d), lambda i, f: (i, 0)),
                  pl.BlockSpec((1, N_MOD, d), lambda i, f: (i // tiles_per_batch, 0, 0)),
                  pl.BlockSpec((1, d), lambda i, f: (0, 0))],
        out_specs=pl.BlockSpec((tm, d), lambda i, f: (i, 0)),
        out_shape=jax.ShapeDtypeStruct((t, d), F32),
        compiler_params=_cparams(("arbitrary", "arbitrary")),
        name="swiglu_ffn_final_norm",
    )(h2, wg, wu, wd, x1, mod3, final_gain)


def kernel(x, c, w_ada, b_ada, norm1_gain, w_in, a_sink, a_out_gain, diff_lq1, diff_lk1,
           diff_lq2, diff_lk2, diff_subln_gain, w_o, norm2_gain, w_gate, w_up, w_down,
           final_gain):
    batch, seq, d = x.shape
    assert w_ada.shape[0] == 1, "single-layer block"
    a_w = d // 2
    b_w = d - a_w
    a_heads = a_w // HEAD_DIM
    b_heads = b_w // B_V_DIM
    a_kv = A_KV_HEADS * HEAD_DIM
    n_heads = a_heads + b_heads
    slopes = 2.0 ** (-8.0 * jnp.arange(1, n_heads + 1, dtype=F32) / n_heads)

    rows = 8
    c_pad = jnp.zeros((rows, d), F32).at[:batch].set(c)
    mod = _ada(c_pad, w_ada[0], b_ada[0][None, :])[:batch]
    mod3 = mod.reshape(batch, N_MOD, d)

    x2 = x.reshape(batch * seq, d)
    qa, ka, va, qdt, kd, vdt = _inproj(
        x2, mod3, norm1_gain[0][None, :], w_in[0].astype(BF16),
        seq=seq, a_q=a_w, a_kv=a_kv, b_w=b_w)

    out_a = _attn_a(slopes[:a_heads], a_sink[0].astype(F32), qa, ka, va,
                    a_out_gain[0][None, :], batch=batch, seq=seq)
    out_b = _attn_b(slopes[a_heads:], diff_lq1[0][None, :], diff_lk1[0][None, :],
                    diff_lq2[0][None, :], diff_lk2[0][None, :], diff_subln_gain[0][None, :],
                    qdt, kd, vdt, batch=batch, seq=seq)

    x1, h2 = _outproj(out_a, out_b, w_o[0].astype(BF16), x2, mod3, norm2_gain[0][None, :],
                      seq=seq)
    out = _ffn(h2, w_gate[0].astype(BF16), w_up[0].astype(BF16), w_down[0].astype(BF16),
               x1, mod3, final_gain[None, :], seq=seq)
    return out.reshape(batch, seq, d)
```

```python
import functools
import math

import jax
import jax.numpy as jnp
from jax import lax
from jax.experimental import pallas as pl
from jax.experimental.pallas import tpu as pltpu

HEAD_DIM = 128
A_KV_HEADS = 2
WINDOW = 128
B_QK_DIM = 64
B_V_DIM = 2 * B_QK_DIM
N_MOD = 6
EPS = 1e-6
NEG_INF = -1e30
LAM_INIT = 0.8 - 0.6 * math.exp(-0.3 * 0)
LOG2E = math.log2(math.e)
GROUP = 14
FIXED_REF_MAX_EXCESS = 64.0
FIXED_REF_MAX_VALUE = 2.0 ** 30
ONES_ROWS = 16
N_BIAS_ROWS = 16

V7X_VMEM_LIMIT_BYTES = 56 * 1024 * 1024

BF16 = jnp.bfloat16
F32 = jnp.float32


def _cparams(semantics):
    return pltpu.CompilerParams(dimension_semantics=semantics,
                                vmem_limit_bytes=V7X_VMEM_LIMIT_BYTES)


def _rms(x, gain):
    return x * lax.rsqrt(jnp.mean(x * x, axis=-1, keepdims=True) + EPS) * gain


def _resident(shape):
    return pl.BlockSpec(shape, lambda *_: (0,) * len(shape), pipeline_mode=pl.Buffered(1))


def _ada_kernel(c_ref, w_ref, b_ref, o_ref):
    c = c_ref[...]
    sc = (c * jax.nn.sigmoid(c)).astype(BF16)
    o_ref[...] = jnp.dot(sc, w_ref[...].astype(BF16), preferred_element_type=F32) + b_ref[...]


def _ada(c_pad, w, b, tn=1024):
    rows, d = c_pad.shape
    n = w.shape[1]
    return pl.pallas_call(
        _ada_kernel,
        grid=(n // tn,),
        in_specs=[pl.BlockSpec((rows, d), lambda j: (0, 0)),
                  pl.BlockSpec((d, tn), lambda j: (0, j)),
                  pl.BlockSpec((1, tn), lambda j: (0, j))],
        out_specs=pl.BlockSpec((rows, tn), lambda j: (0, j)),
        out_shape=jax.ShapeDtypeStruct((rows, n), F32),
        compiler_params=_cparams(("arbitrary",)),
        name="ada_mod",
    )(c_pad, w, b)


def _inproj_kernel(x_ref, mod_ref, g_ref, w_ref,
                   qa_ref, ka_ref, va_ref, qdt_ref, kd_ref, vdt_ref, h_ref,
                   *, a_q, a_kv, b_w, chunk):
    x = x_ref[...]
    sh1 = mod_ref[0, 0:1, :]
    sc1 = mod_ref[0, 1:2, :]
    h_ref[...] = (_rms(x, g_ref[...]) * (1.0 + sc1) + sh1).astype(BF16)

    def proj(c0, width):
        return jnp.dot(h_ref[...], w_ref[:, c0:c0 + width], preferred_element_type=F32)

    o1 = a_q
    o2 = o1 + a_kv
    o3 = o2 + a_kv
    o4 = o3 + b_w
    o5 = o4 + b_w
    for c in range(0, a_q, chunk):
        qa_ref[:, c:c + chunk] = proj(c, chunk).astype(BF16)
    ka_ref[...] = proj(o1, a_kv).astype(BF16)
    va_ref[...] = proj(o2, a_kv).astype(BF16)
    for c in range(0, b_w, chunk):
        kd_ref[:, c:c + chunk] = proj(o4 + c, chunk).astype(BF16)
    qscale = B_QK_DIM ** -0.5 * LOG2E
    for c in range(0, b_w, chunk):
        q = proj(o3 + c, chunk) * qscale
        v = proj(o5 + c, chunk)
        for hc in range(0, chunk, B_V_DIM):
            qdt_ref[c + hc:c + hc + B_V_DIM, :] = q[:, hc:hc + B_V_DIM].T.astype(BF16)
            vdt_ref[c + hc:c + hc + B_V_DIM, :] = v[:, hc:hc + B_V_DIM].T.astype(BF16)


def _inproj(x2, mod3, gain, w_bf, *, seq, a_q, a_kv, b_w, tm=512):
    t, d = x2.shape
    n = w_bf.shape[1]
    tiles_per_batch = seq // tm
    chunk = min(512, a_q, b_w)
    kern = functools.partial(_inproj_kernel, a_q=a_q, a_kv=a_kv, b_w=b_w, chunk=chunk)
    row = lambda width: pl.BlockSpec((tm, width), lambda i: (i, 0))
    col = lambda height: pl.BlockSpec((height, tm), lambda i: (0, i))
    return pl.pallas_call(
        kern,
        grid=(t // tm,),
        in_specs=[row(d),
                  pl.BlockSpec((1, N_MOD, d), lambda i: (i // tiles_per_batch, 0, 0)),
                  pl.BlockSpec((1, d), lambda i: (0, 0)),
                  _resident((d, n))],
        out_specs=[row(a_q), row(a_kv), row(a_kv), col(b_w), row(b_w), col(b_w)],
        out_shape=[jax.ShapeDtypeStruct((t, a_q), BF16),
                   jax.ShapeDtypeStruct((t, a_kv), BF16),
                   jax.ShapeDtypeStruct((t, a_kv), BF16),
                   jax.ShapeDtypeStruct((b_w, t), BF16),
                   jax.ShapeDtypeStruct((t, b_w), BF16),
                   jax.ShapeDtypeStruct((b_w, t), BF16)],
        scratch_shapes=[pltpu.VMEM((tm, d), BF16)],
        compiler_params=_cparams(("arbitrary",)),
        name="norm1_inproj",
    )(x2, mod3, gain, w_bf)


def _attn_a_kernel(slopes_ref, sink_ref, q_ref, k_ref, v_ref, g_ref, o_ref, bias_ref, acc_ref,
                   *, tq, kw, seq, heads):
    i = pl.program_id(1)
    q0 = i * tq
    kstart = pl.multiple_of(jnp.clip(q0 - WINDOW, 0, seq - kw), WINDOW)
    group = heads // A_KV_HEADS

    @pl.when((pl.program_id(0) == 0) & (i == 0))
    def _():
        r = lax.broadcasted_iota(jnp.int32, (tq, kw), 0)
        c = lax.broadcasted_iota(jnp.int32, (tq, kw), 1)
        for case in range(3):
            dist = jnp.abs((r - c) + case * WINDOW)
            for h in range(heads):
                bias_ref[case, h] = jnp.where(dist <= WINDOW,
                                              -(slopes_ref[h] * LOG2E) * dist.astype(F32),
                                              NEG_INF)

    case = (q0 - kstart) // WINDOW
    scale = HEAD_DIM ** -0.5
    kwins = [k_ref[pl.ds(kstart, kw), kvh * HEAD_DIM:(kvh + 1) * HEAD_DIM]
             for kvh in range(A_KV_HEADS)]
    vwins = [v_ref[pl.ds(kstart, kw), kvh * HEAD_DIM:(kvh + 1) * HEAD_DIM]
             for kvh in range(A_KV_HEADS)]
    scores = [lax.dot_general(q_ref[:, h * HEAD_DIM:(h + 1) * HEAD_DIM], kwins[h // group],
                              (((1,), (1,)), ((), ())), preferred_element_type=F32)
              for h in range(heads)]
    probs = []
    inv_denoms = []
    for h in range(heads):
        s = scores[h] * (scale * LOG2E) + bias_ref[case, h]
        sink = sink_ref[h] * LOG2E
        m = jnp.maximum(jnp.max(s, axis=-1, keepdims=True), sink)
        p = jnp.exp2(s - m)
        denom = jnp.sum(p, axis=-1, keepdims=True) + jnp.exp2(sink - m)
        probs.append(p.astype(BF16))
        inv_denoms.append(1.0 / denom)
    for h in range(heads):
        acc_ref[:, h * HEAD_DIM:(h + 1) * HEAD_DIM] = inv_denoms[h] * jnp.dot(
            probs[h], vwins[h // group], preferred_element_type=F32)
    o_ref[...] = _rms(acc_ref[...], g_ref[...]).astype(BF16)


def _attn_a(slopes_a, sink, qa, ka, va, gain, *, batch, seq, tq=128):
    t, a_q = qa.shape
    a_kv = ka.shape[1]
    heads = a_q // HEAD_DIM
    kw = tq + 2 * WINDOW
    nq = seq // tq
    kern = functools.partial(_attn_a_kernel, tq=tq, kw=kw, seq=seq, heads=heads)
    smem = pl.BlockSpec(memory_space=pltpu.SMEM)
    return pl.pallas_call(
        kern,
        grid=(batch, nq),
        in_specs=[smem, smem,
                  pl.BlockSpec((tq, a_q), lambda b, i: (b * nq + i, 0)),
                  pl.BlockSpec((seq, a_kv), lambda b, i: (b, 0)),
                  pl.BlockSpec((seq, a_kv), lambda b, i: (b, 0)),
                  pl.BlockSpec((1, a_q), lambda b, i: (0, 0))],
        out_specs=pl.BlockSpec((tq, a_q), lambda b, i: (b * nq + i, 0)),
        out_shape=jax.ShapeDtypeStruct((t, a_q), BF16),
        scratch_shapes=[pltpu.VMEM((3, heads, tq, kw), F32), pltpu.VMEM((tq, a_q), F32)],
        compiler_params=_cparams(("arbitrary", "arbitrary")),
        name="attn_window_gqa",
    )(slopes_a, sink, qa, ka, va, gain)


def _attn_b_kernel(slopes_ref, lq1_ref, lk1_ref, lq2_ref, lk2_ref, g_ref, d0_ref, feat_ref,
                   qt_ref, k_ref, vt_ref, o_ref,
                   w_ref, diag_ref, s0_ref, s1_ref, m_ref, l_ref, acc_ref, vt1_ref, acc1_ref,
                   nref_ref, flag_ref, p0_ref, p1_ref,
                   *, tile, sw, seq):
    h = pl.program_id(1)
    slope2 = slopes_ref[h] * LOG2E
    lam = (jnp.exp(jnp.sum(lq1_ref[...] * lk1_ref[...], keepdims=True))
           - jnp.exp(jnp.sum(lq2_ref[...] * lk2_ref[...], keepdims=True)) + LAM_INIT)
    halves = tile // sw
    n_tiles = seq // tile

    rho = lax.broadcasted_iota(jnp.int32, (N_BIAS_ROWS, sw), 0)
    lane = lax.broadcasted_iota(jnp.int32, (N_BIAS_ROWS, sw), 1).astype(F32)
    coeff = jnp.where(rho < 3, slope2,
                      jnp.where(rho < 6, 256.0 * slope2,
                                jnp.where(rho < 9, -slope2 * lane, 0.0)))
    hi = coeff.astype(BF16).astype(F32)
    mid = (coeff - hi).astype(BF16).astype(F32)
    lo = (coeff - hi - mid).astype(BF16).astype(F32)
    level = rho % 3
    rows = jnp.where(level == 0, hi, jnp.where(level == 1, mid, lo))
    w_ref[...] = jnp.zeros_like(w_ref)
    for strip in range(2 * halves):
        cols = slice(strip * sw, (strip + 1) * sw)
        w_ref[1, B_V_DIM:B_V_DIM + N_BIAS_ROWS, cols] = rows.astype(BF16)
        w_ref[2, B_V_DIM:B_V_DIM + N_BIAS_ROWS, cols] = (-rows).astype(BF16)
    for half in range(halves):
        diag_ref[half] = -slope2 * jnp.abs(d0_ref[...] - float(half * sw))

    def norm_chunk(n, carry):
        kmax2, vmax = carry
        r0 = pl.multiple_of(n * tile, tile)
        kf = k_ref[pl.ds(r0, tile), :].astype(F32)
        kn2 = jnp.max(jnp.sum(kf * kf, axis=1, keepdims=True), axis=0, keepdims=True)
        va = jnp.abs(vt_ref[:, pl.ds(r0, tile)].astype(F32))
        vm = jnp.max(jnp.max(va, axis=1, keepdims=True), axis=0, keepdims=True)
        return jnp.maximum(kmax2, kn2), jnp.maximum(vmax, vm)

    kmax2, vmax = lax.fori_loop(0, n_tiles, norm_chunk,
                                (jnp.zeros((1, 1), F32), jnp.zeros((1, 1), F32)))
    vt1_ref[0:B_V_DIM, :] = vt_ref[...]
    extra = lax.broadcasted_iota(jnp.int32, (ONES_ROWS, seq), 0)
    vt1_ref[B_V_DIM:, :] = jnp.where(extra == 0, 1.0, 0.0).astype(BF16)

    def key_tile(i, t):
        return jnp.where(t == 0, i, jnp.where(t <= i, t - 1, t))

    def score_tile(i, t):
        j = key_tile(i, t)
        k0 = pl.multiple_of(j * tile, tile)
        widx = jnp.where(j == i, 0, jnp.where(j < i, 1, 2))
        lhs = jnp.concatenate([k_ref[pl.ds(k0, tile), :], feat_ref[...]], axis=1)
        return jnp.dot(lhs, w_ref[widx], preferred_element_type=F32)

    def scores(i, t, s_ref):
        s_ref[...] = score_tile(i, t)

    def softmax_pv(i, t, s_ref, fixed, same_tile=False):
        j = key_tile(i, t)
        k0 = pl.multiple_of(j * tile, tile)
        q0 = i * tile
        for half in range(halves):
            cst = slope2 * (k0 - q0 - half * sw).astype(F32)
            tc = 0.0 if same_tile else jnp.where(j < i, cst, -cst)
            for comp in range(2):
                cols = slice((2 * half + comp) * sw, (2 * half + comp + 1) * sw)
                t_sc = s_ref[:, cols]
                if same_tile:
                    t_sc = t_sc + diag_ref[half]
                if fixed:
                    p = jnp.exp2(t_sc - (m_ref[:, cols] - tc))
                    acc1_ref[:, cols] += jnp.dot(vt1_ref[:, pl.ds(k0, tile)], p.astype(BF16),
                                                 preferred_element_type=F32)
                else:
                    m_old = m_ref[:, cols]
                    m_new = jnp.maximum(m_old, jnp.max(t_sc, axis=0, keepdims=True) + tc)
                    alpha = jnp.exp2(m_old - m_new)
                    p = jnp.exp2(t_sc - (m_new - tc))
                    l_ref[:, cols] = alpha * l_ref[:, cols] + jnp.sum(p, axis=0, keepdims=True)
                    m_ref[:, cols] = m_new
                    pv = jnp.dot(vt_ref[:, pl.ds(k0, tile)], p.astype(BF16),
                                 preferred_element_type=F32)
                    acc_ref[:, cols] = alpha * acc_ref[:, cols] + pv

    s_refs = (s0_ref, s1_ref)
    p_refs = (p0_ref, p1_ref)

    def probabilities(i, t, p_ref):
        j = key_tile(i, t)
        k0 = j * tile
        q0 = i * tile
        s = score_tile(i, t)
        for half in range(halves):
            cst = slope2 * (k0 - q0 - half * sw).astype(F32)
            tc = jnp.where(j < i, cst, -cst)
            for comp in range(2):
                cols = slice((2 * half + comp) * sw, (2 * half + comp + 1) * sw)
                p_ref[:, cols] = jnp.exp2(s[:, cols] - (m_ref[:, cols] - tc)).astype(BF16)

    def value_update(i, t, p_ref):
        k0 = pl.multiple_of(key_tile(i, t) * tile, tile)
        acc1_ref[...] += jnp.dot(vt1_ref[:, pl.ds(k0, tile)], p_ref[...],
                                 preferred_element_type=F32)

    def start_query_tile(i):
        q0 = pl.multiple_of(i * tile, tile)
        for half in range(halves):
            qcols = pl.ds(q0 + half * sw, sw)
            c0 = 2 * half * sw
            for variant in range(3):
                w_ref[variant, 0:B_QK_DIM, c0:c0 + sw] = qt_ref[0:B_QK_DIM, qcols]
                w_ref[variant, B_QK_DIM:B_V_DIM, c0 + sw:c0 + 2 * sw] = qt_ref[B_QK_DIM:, qcols]
        s = score_tile(i, 0)
        s_refs[0][...] = s
        ref = jnp.max(s, axis=0, keepdims=True)
        nref_ref[...] = ref
        qf = w_ref[0, 0:B_V_DIM, :].astype(F32)
        upper = jnp.sqrt(jnp.sum(qf * qf, axis=0, keepdims=True) * kmax2) * 1.01 + 1.0
        ok = (jnp.max(upper - ref) <= FIXED_REF_MAX_EXCESS) & (
            jnp.max(vmax) <= FIXED_REF_MAX_VALUE)
        flag_ref[0] = ok.astype(jnp.int32)

    def finish_query_tile(i):
        q0 = pl.multiple_of(i * tile, tile)
        o = acc_ref[...] / l_ref[...]
        for half in range(halves):
            c0 = 2 * half * sw
            od = (o[:, c0:c0 + sw] - lam * o[:, c0 + sw:c0 + 2 * sw]).T
            o_ref[pl.ds(q0 + half * sw, sw), :] = (
                _rms(od, g_ref[...]) * (1.0 - LAM_INIT)).astype(BF16)

    def key_tile_pipeline(i, fixed):
        def produce(t, parity):
            if fixed:
                probabilities(i, t, p_refs[parity])
            else:
                scores(i, t, s_refs[parity])

        def consume(t, parity):
            if fixed:
                value_update(i, t, p_refs[parity])
            else:
                softmax_pv(i, t, s_refs[parity], False)

        produce(1, 1)
        if fixed:
            finish_query_tile(jnp.maximum(i - 1, 0))
        softmax_pv(i, 0, s_refs[0], fixed, same_tile=True)

        def group(n, c2):
            for u in range(GROUP):
                produce(GROUP * n + u + 2, u % 2)
                consume(GROUP * n + u + 1, (u + 1) % 2)
            return c2

        n_groups = (n_tiles - 2) // GROUP
        lax.fori_loop(0, n_groups, group, 0)
        for t in range(n_groups * GROUP + 1, n_tiles):
            if t + 1 < n_tiles:
                produce(t + 1, (t + 1) % 2)
            else:
                assert (t - 1) % 2 == 0
                start_query_tile(jnp.minimum(i + 1, n_tiles - 1))
            consume(t, t % 2)

    def q_body(i, carry):
        use_fixed = flag_ref[0]

        @pl.when(use_fixed == 1)
        def _():
            m_ref[...] = nref_ref[...]
            acc1_ref[...] = jnp.zeros_like(acc1_ref)
            key_tile_pipeline(i, True)
            acc_ref[...] = acc1_ref[0:B_V_DIM, :]
            l_ref[...] = acc1_ref[B_V_DIM:B_V_DIM + 1, :]

        @pl.when(use_fixed != 1)
        def _():
            finish_query_tile(jnp.maximum(i - 1, 0))
            m_ref[...] = jnp.full_like(m_ref, NEG_INF)
            l_ref[...] = jnp.zeros_like(l_ref)
            acc_ref[...] = jnp.zeros_like(acc_ref)
            key_tile_pipeline(i, False)

        return carry

    acc_ref[...] = jnp.zeros_like(acc_ref)
    l_ref[...] = jnp.ones_like(l_ref)
    start_query_tile(0)
    lax.fori_loop(0, n_tiles, q_body, 0)
    finish_query_tile(n_tiles - 1)


def _attn_b(slopes_b, lq1, lk1, lq2, lk2, gain, qdt, kd, vdt, *, batch, seq, tile=512, sw=256):
    b_w, t = qdt.shape
    heads = b_w // B_V_DIM
    tile = min(tile, seq // 2)
    assert seq % (2 * tile) == 0 and tile % sw == 0
    r = lax.broadcasted_iota(jnp.int32, (tile, sw), 0)
    c = lax.broadcasted_iota(jnp.int32, (tile, sw), 1)
    d0 = (r - c).astype(F32)
    rk = jnp.arange(tile, dtype=jnp.int32)[:, None]
    fcol = jnp.arange(B_V_DIM, dtype=jnp.int32)[None, :]
    feat = jnp.where(fcol < 3, rk % 256,
                     jnp.where(fcol < 6, rk // 256, jnp.where(fcol < 9, 1, 0))).astype(BF16)
    kern = functools.partial(_attn_b_kernel, tile=tile, sw=sw, seq=seq)
    smem = pl.BlockSpec(memory_space=pltpu.SMEM)
    vec = lambda n: pl.BlockSpec((1, n), lambda b, h: (0, 0))
    tposed = pl.BlockSpec((B_V_DIM, seq), lambda b, h: (h, b))
    natural = pl.BlockSpec((seq, B_V_DIM), lambda b, h: (b, h))
    return pl.pallas_call(
        kern,
        grid=(batch, heads),
        in_specs=[smem, vec(B_QK_DIM), vec(B_QK_DIM), vec(B_QK_DIM), vec(B_QK_DIM),
                  vec(B_V_DIM),
                  pl.BlockSpec((tile, sw), lambda b, h: (0, 0)),
                  pl.BlockSpec((tile, B_V_DIM), lambda b, h: (0, 0)),
                  tposed, natural, tposed],
        out_specs=natural,
        out_shape=jax.ShapeDtypeStruct((t, b_w), BF16),
        scratch_shapes=[pltpu.VMEM((3, 2 * B_V_DIM, 2 * tile), BF16),
                        pltpu.VMEM((tile // sw, tile, sw), F32),
                        pltpu.VMEM((tile, 2 * tile), F32),
                        pltpu.VMEM((tile, 2 * tile), F32),
                        pltpu.VMEM((1, 2 * tile), F32),
                        pltpu.VMEM((1, 2 * tile), F32),
                        pltpu.VMEM((B_V_DIM, 2 * tile), F32),
                        pltpu.VMEM((B_V_DIM + ONES_ROWS, seq), BF16),
                        pltpu.VMEM((B_V_DIM + ONES_ROWS, 2 * tile), F32),
                        pltpu.VMEM((1, 2 * tile), F32),
                        pltpu.SMEM((1,), jnp.int32),
                        pltpu.VMEM((tile, 2 * tile), BF16),
                        pltpu.VMEM((tile, 2 * tile), BF16)],
        compiler_params=_cparams(("arbitrary", "arbitrary")),
        name="attn_diff",
    )(slopes_b, lq1, lk1, lq2, lk2, gain, d0, feat, qdt, kd, vdt)


def _outproj_kernel(a_ref, b_ref, w_ref, x_ref, mod_ref, g_ref, x1_ref, h2_ref, *, a_w):
    mix = (jnp.dot(a_ref[...], w_ref[0:a_w, :], preferred_element_type=F32)
           + jnp.dot(b_ref[...], w_ref[a_w:, :], preferred_element_type=F32))
    g1 = mod_ref[0, 2:3, :]
    sh2 = mod_ref[0, 3:4, :]
    sc2 = mod_ref[0, 4:5, :]
    x1 = x_ref[...] + g1 * mix
    x1_ref[...] = x1
    h2_ref[...] = (_rms(x1, g_ref[...]) * (1.0 + sc2) + sh2).astype(BF16)


def _outproj(out_a, out_b, w_bf, x2, mod3, gain, *, seq, tm=512):
    t, d = x2.shape
    a_w = out_a.shape[1]
    b_w = out_b.shape[1]
    tiles_per_batch = seq // tm
    row = lambda width: pl.BlockSpec((tm, width), lambda i: (i, 0))
    return pl.pallas_call(
        functools.partial(_outproj_kernel, a_w=a_w),
        grid=(t // tm,),
        in_specs=[row(a_w), row(b_w), _resident((a_w + b_w, d)), row(d),
                  pl.BlockSpec((1, N_MOD, d), lambda i: (i // tiles_per_batch, 0, 0)),
                  pl.BlockSpec((1, d), lambda i: (0, 0))],
        out_specs=[row(d), row(d)],
        out_shape=[jax.ShapeDtypeStruct((t, d), F32), jax.ShapeDtypeStruct((t, d), BF16)],
        compiler_params=_cparams(("arbitrary",)),
        name="outproj_norm2",
    )(out_a, out_b, w_bf, x2, mod3, gain)


def _ffn_kernel(h_ref, wg_ref, wu_ref, wd_ref, x1_ref, mod_ref, fg_ref, o_ref):
    f = pl.program_id(1)

    @pl.when(f == 0)
    def _():
        o_ref[...] = jnp.zeros_like(o_ref)

    h = h_ref[...]
    g = jnp.dot(h, wg_ref[...], preferred_element_type=F32)
    u = jnp.dot(h, wu_ref[...], preferred_element_type=F32)
    a = (g * jax.nn.sigmoid(g) * u).astype(BF16)
    o_ref[...] += jnp.dot(a, wd_ref[...], preferred_element_type=F32)

    @pl.when(f == pl.num_programs(1) - 1)
    def _():
        g2 = mod_ref[0, 5:6, :]
        o_ref[...] = _rms(x1_ref[...] + g2 * o_ref[...], fg_ref[...])


def _ffn(h2, wg, wu, wd, x1, mod3, final_gain, *, seq, tm=512, tf=512):
    t, d = h2.shape
    ff = wg.shape[1]
    tm = min(tm, seq)
    if ff % tf:
        tf = 256
    assert ff % tf == 0 and t % tm == 0 and seq % tm == 0
    tiles_per_batch = seq // tm
    return pl.pallas_call(
        _ffn_kernel,
        grid=(t // tm, ff // tf),
        in_specs=[pl.BlockSpec((tm, d), lambda i, f: (i, 0)),
                  pl.BlockSpec((d, tf), lambda i, f: (0, f)),
                  pl.BlockSpec((d, tf), lambda i, f: (0, f)),
                  pl.BlockSpec((tf, d), lambda i, f: (f, 0)),
                  pl.BlockSpec((tm, d), lambda i, f: (i, 0)),
                  pl.BlockSpec((1, N_MOD, d), lambda i, f: (i // tiles_per_batch, 0, 0)),
                  pl.BlockSpec((1, d), lambda i, f: (0, 0))],
        out_specs=pl.BlockSpec((tm, d), lambda i, f: (i, 0)),
        out_shape=jax.ShapeDtypeStruct((t, d), F32),
        compiler_params=_cparams(("arbitrary", "arbitrary")),
        name="swiglu_ffn_final_norm",
    )(h2, wg, wu, wd, x1, mod3, final_gain)


def kernel(x, c, w_ada, b_ada, norm1_gain, w_in, a_sink, a_out_gain, diff_lq1, diff_lk1,
           diff_lq2, diff_lk2, diff_subln_gain, w_o, norm2_gain, w_gate, w_up, w_down,
           final_gain):
    batch, seq, d = x.shape
    assert w_ada.shape[0] == 1, "single-layer block"
    a_w = d // 2
    b_w = d - a_w
    a_heads = a_w // HEAD_DIM
    b_heads = b_w // B_V_DIM
    a_kv = A_KV_HEADS * HEAD_DIM
    n_heads = a_heads + b_heads
    slopes = 2.0 ** (-8.0 * jnp.arange(1, n_heads + 1, dtype=F32) / n_heads)

    rows = 8
    c_pad = jnp.zeros((rows, d), F32).at[:batch].set(c)
    mod = _ada(c_pad, w_ada[0], b_ada[0][None, :])[:batch]
    mod3 = mod.reshape(batch, N_MOD, d)

    x2 = x.reshape(batch * seq, d)
    qa, ka, va, qdt, kd, vdt = _inproj(
        x2, mod3, norm1_gain[0][None, :], w_in[0].astype(BF16),
        seq=seq, a_q=a_w, a_kv=a_kv, b_w=b_w)

    out_a = _attn_a(slopes[:a_heads], a_sink[0].astype(F32), qa, ka, va,
                    a_out_gain[0][None, :], batch=batch, seq=seq)
    out_b = _attn_b(slopes[a_heads:], diff_lq1[0][None, :], diff_lk1[0][None, :],
                    diff_lq2[0][None, :], diff_lk2[0][None, :], diff_subln_gain[0][None, :],
                    qdt, kd, vdt, batch=batch, seq=seq)

    x1, h2 = _outproj(out_a, out_b, w_o[0].astype(BF16), x2, mod3, norm2_gain[0][None, :],
                      seq=seq)
    out = _ffn(h2, w_gate[0].astype(BF16), w_up[0].astype(BF16), w_down[0].astype(BF16),
               x1, mod3, final_gain[None, :], seq=seq)
    return out.reshape(batch, seq, d)
```

```python
import functools
import math

import jax
import jax.numpy as jnp
from jax import lax
from jax.experimental import pallas as pl
from jax.experimental.pallas import tpu as pltpu

HEAD_DIM = 128
A_KV_HEADS = 2
WINDOW = 128
B_QK_DIM = 64
B_V_DIM = 2 * B_QK_DIM
N_MOD = 6
EPS = 1e-6
NEG_INF = -1e30
LAM_INIT = 0.8 - 0.6 * math.exp(-0.3 * 0)
LOG2E = math.log2(math.e)
GROUP = 6
FIXED_REF_MAX_EXCESS = 64.0
FIXED_REF_MAX_VALUE = 2.0 ** 30
ONES_ROWS = 16
N_BIAS_ROWS = 16

V7X_VMEM_LIMIT_BYTES = 56 * 1024 * 1024

BF16 = jnp.bfloat16
F32 = jnp.float32


def _cparams(semantics):
    return pltpu.CompilerParams(dimension_semantics=semantics,
                                vmem_limit_bytes=V7X_VMEM_LIMIT_BYTES)


def _rms(x, gain):
    return x * lax.rsqrt(jnp.mean(x * x, axis=-1, keepdims=True) + EPS) * gain


def _resident(shape):
    return pl.BlockSpec(shape, lambda *_: (0,) * len(shape), pipeline_mode=pl.Buffered(1))


def _ada_kernel(c_ref, w_ref, b_ref, o_ref):
    c = c_ref[...]
    sc = (c * jax.nn.sigmoid(c)).astype(BF16)
    o_ref[...] = jnp.dot(sc, w_ref[...].astype(BF16), preferred_element_type=F32) + b_ref[...]


def _ada(c_pad, w, b, tn=1024):
    rows, d = c_pad.shape
    n = w.shape[1]
    return pl.pallas_call(
        _ada_kernel,
        grid=(n // tn,),
        in_specs=[pl.BlockSpec((rows, d), lambda j: (0, 0)),
                  pl.BlockSpec((d, tn), lambda j: (0, j)),
                  pl.BlockSpec((1, tn), lambda j: (0, j))],
        out_specs=pl.BlockSpec((rows, tn), lambda j: (0, j)),
        out_shape=jax.ShapeDtypeStruct((rows, n), F32),
        compiler_params=_cparams(("arbitrary",)),
        name="ada_mod",
    )(c_pad, w, b)


def _inproj_kernel(x_ref, mod_ref, g_ref, w_ref,
                   qa_ref, ka_ref, va_ref, qdt_ref, kd_ref, vdt_ref, h_ref,
                   *, a_q, a_kv, b_w, chunk):
    x = x_ref[...]
    sh1 = mod_ref[0, 0:1, :]
    sc1 = mod_ref[0, 1:2, :]
    h_ref[...] = (_rms(x, g_ref[...]) * (1.0 + sc1) + sh1).astype(BF16)

    def proj(c0, width):
        return jnp.dot(h_ref[...], w_ref[:, c0:c0 + width], preferred_element_type=F32)

    o1 = a_q
    o2 = o1 + a_kv
    o3 = o2 + a_kv
    o4 = o3 + b_w
    o5 = o4 + b_w
    for c in range(0, a_q, chunk):
        qa_ref[:, c:c + chunk] = proj(c, chunk).astype(BF16)
    ka_ref[...] = proj(o1, a_kv).astype(BF16)
    va_ref[...] = proj(o2, a_kv).astype(BF16)
    for c in range(0, b_w, chunk):
        kd_ref[:, c:c + chunk] = proj(o4 + c, chunk).astype(BF16)
    qscale = B_QK_DIM ** -0.5 * LOG2E
    for c in range(0, b_w, chunk):
        q = proj(o3 + c, chunk) * qscale
        v = proj(o5 + c, chunk)
        for hc in range(0, chunk, B_V_DIM):
            qdt_ref[c + hc:c + hc + B_V_DIM, :] = q[:, hc:hc + B_V_DIM].T.astype(BF16)
            vdt_ref[c + hc:c + hc + B_V_DIM, :] = v[:, hc:hc + B_V_DIM].T.astype(BF16)


def _inproj(x2, mod3, gain, w_bf, *, seq, a_q, a_kv, b_w, tm=512):
    t, d = x2.shape
    n = w_bf.shape[1]
    tiles_per_batch = seq // tm
    chunk = min(512, a_q, b_w)
    kern = functools.partial(_inproj_kernel, a_q=a_q, a_kv=a_kv, b_w=b_w, chunk=chunk)
    row = lambda width: pl.BlockSpec((tm, width), lambda i: (i, 0))
    col = lambda height: pl.BlockSpec((height, tm), lambda i: (0, i))
    return pl.pallas_call(
        kern,
        grid=(t // tm,),
        in_specs=[row(d),
                  pl.BlockSpec((1, N_MOD, d), lambda i: (i // tiles_per_batch, 0, 0)),
                  pl.BlockSpec((1, d), lambda i: (0, 0)),
                  _resident((d, n))],
        out_specs=[row(a_q), row(a_kv), row(a_kv), col(b_w), row(b_w), col(b_w)],
        out_shape=[jax.ShapeDtypeStruct((t, a_q), BF16),
                   jax.ShapeDtypeStruct((t, a_kv), BF16),
                   jax.ShapeDtypeStruct((t, a_kv), BF16),
                   jax.ShapeDtypeStruct((b_w, t), BF16),
                   jax.ShapeDtypeStruct((t, b_w), BF16),
                   jax.ShapeDtypeStruct((b_w, t), BF16)],
        scratch_shapes=[pltpu.VMEM((tm, d), BF16)],
        compiler_params=_cparams(("arbitrary",)),
        name="norm1_inproj",
    )(x2, mod3, gain, w_bf)


def _attn_a_kernel(slopes_ref, sink_ref, q_ref, k_ref, v_ref, g_ref, o_ref, bias_ref, acc_ref,
                   *, tq, kw, seq, heads):
    i = pl.program_id(1)
    q0 = i * tq
    kstart = pl.multiple_of(jnp.clip(q0 - WINDOW, 0, seq - kw), WINDOW)
    group = heads // A_KV_HEADS

    @pl.when((pl.program_id(0) == 0) & (i == 0))
    def _():
        r = lax.broadcasted_iota(jnp.int32, (tq, kw), 0)
        c = lax.broadcasted_iota(jnp.int32, (tq, kw), 1)
        for case in range(3):
            dist = jnp.abs((r - c) + case * WINDOW)
            for h in range(heads):
                bias_ref[case, h] = jnp.where(dist <= WINDOW,
                                              -(slopes_ref[h] * LOG2E) * dist.astype(F32),
                                              NEG_INF)

    case = (q0 - kstart) // WINDOW
    scale = HEAD_DIM ** -0.5
    kwins = [k_ref[pl.ds(kstart, kw), kvh * HEAD_DIM:(kvh + 1) * HEAD_DIM]
             for kvh in range(A_KV_HEADS)]
    vwins = [v_ref[pl.ds(kstart, kw), kvh * HEAD_DIM:(kvh + 1) * HEAD_DIM]
             for kvh in range(A_KV_HEADS)]
    scores = [lax.dot_general(q_ref[:, h * HEAD_DIM:(h + 1) * HEAD_DIM], kwins[h // group],
                              (((1,), (1,)), ((), ())), preferred_element_type=F32)
              for h in range(heads)]
    probs = []
    inv_denoms = []
    for h in range(heads):
        s = scores[h] * (scale * LOG2E) + bias_ref[case, h]
        sink = sink_ref[h] * LOG2E
        m = jnp.maximum(jnp.max(s, axis=-1, keepdims=True), sink)
        p = jnp.exp2(s - m)
        denom = jnp.sum(p, axis=-1, keepdims=True) + jnp.exp2(sink - m)
        probs.append(p.astype(BF16))
        inv_denoms.append(1.0 / denom)
    for h in range(heads):
        acc_ref[:, h * HEAD_DIM:(h + 1) * HEAD_DIM] = inv_denoms[h] * jnp.dot(
            probs[h], vwins[h // group], preferred_element_type=F32)
    o_ref[...] = _rms(acc_ref[...], g_ref[...]).astype(BF16)


def _attn_a(slopes_a, sink, qa, ka, va, gain, *, batch, seq, tq=128):
    t, a_q = qa.shape
    a_kv = ka.shape[1]
    heads = a_q // HEAD_DIM
    kw = tq + 2 * WINDOW
    nq = seq // tq
    kern = functools.partial(_attn_a_kernel, tq=tq, kw=kw, seq=seq, heads=heads)
    smem = pl.BlockSpec(memory_space=pltpu.SMEM)
    return pl.pallas_call(
        kern,
        grid=(batch, nq),
        in_specs=[smem, smem,
                  pl.BlockSpec((tq, a_q), lambda b, i: (b * nq + i, 0)),
                  pl.BlockSpec((seq, a_kv), lambda b, i: (b, 0)),
                  pl.BlockSpec((seq, a_kv), lambda b, i: (b, 0)),
                  pl.BlockSpec((1, a_q), lambda b, i: (0, 0))],
        out_specs=pl.BlockSpec((tq, a_q), lambda b, i: (b * nq + i, 0)),
        out_shape=jax.ShapeDtypeStruct((t, a_q), BF16),
        scratch_shapes=[pltpu.VMEM((3, heads, tq, kw), F32), pltpu.VMEM((tq, a_q), F32)],
        compiler_params=_cparams(("arbitrary", "arbitrary")),
        name="attn_window_gqa",
    )(slopes_a, sink, qa, ka, va, gain)


def _attn_b_kernel(slopes_ref, lq1_ref, lk1_ref, lq2_ref, lk2_ref, g_ref, d0_ref, feat_ref,
                   qt_ref, k_ref, vt_ref, o_ref,
                   w_ref, diag_ref, s0_ref, s1_ref, m_ref, l_ref, acc_ref, vt1_ref, acc1_ref,
                   nref_ref, flag_ref, p0_ref, p1_ref,
                   *, tile, sw, seq):
    h = pl.program_id(1)
    slope2 = slopes_ref[h] * LOG2E
    lam = (jnp.exp(jnp.sum(lq1_ref[...] * lk1_ref[...], keepdims=True))
           - jnp.exp(jnp.sum(lq2_ref[...] * lk2_ref[...], keepdims=True)) + LAM_INIT)
    halves = tile // sw
    n_tiles = seq // tile

    rho = lax.broadcasted_iota(jnp.int32, (N_BIAS_ROWS, sw), 0)
    lane = lax.broadcasted_iota(jnp.int32, (N_BIAS_ROWS, sw), 1).astype(F32)
    coeff = jnp.where(rho < 3, slope2,
                      jnp.where(rho < 6, 256.0 * slope2,
                                jnp.where(rho < 9, -slope2 * lane, 0.0)))
    hi = coeff.astype(BF16).astype(F32)
    mid = (coeff - hi).astype(BF16).astype(F32)
    lo = (coeff - hi - mid).astype(BF16).astype(F32)
    level = rho % 3
    rows = jnp.where(level == 0, hi, jnp.where(level == 1, mid, lo))
    w_ref[...] = jnp.zeros_like(w_ref)
    for strip in range(2 * halves):
        cols = slice(strip * sw, (strip + 1) * sw)
        w_ref[1, B_V_DIM:B_V_DIM + N_BIAS_ROWS, cols] = rows.astype(BF16)
        w_ref[2, B_V_DIM:B_V_DIM + N_BIAS_ROWS, cols] = (-rows).astype(BF16)
    for half in range(halves):
        diag_ref[half] = -slope2 * jnp.abs(d0_ref[...] - float(half * sw))

    def norm_chunk(n, carry):
        kmax2, vmax = carry
        r0 = pl.multiple_of(n * tile, tile)
        kf = k_ref[pl.ds(r0, tile), :].astype(F32)
        kn2 = jnp.max(jnp.sum(kf * kf, axis=1, keepdims=True), axis=0, keepdims=True)
        va = jnp.abs(vt_ref[:, pl.ds(r0, tile)].astype(F32))
        vm = jnp.max(jnp.max(va, axis=1, keepdims=True), axis=0, keepdims=True)
        return jnp.maximum(kmax2, kn2), jnp.maximum(vmax, vm)

    kmax2, vmax = lax.fori_loop(0, n_tiles, norm_chunk,
                                (jnp.zeros((1, 1), F32), jnp.zeros((1, 1), F32)))
    vt1_ref[0:B_V_DIM, :] = vt_ref[...]
    extra = lax.broadcasted_iota(jnp.int32, (ONES_ROWS, seq), 0)
    vt1_ref[B_V_DIM:, :] = jnp.where(extra == 0, 1.0, 0.0).astype(BF16)

    def key_tile(i, t):
        return jnp.where(t == 0, i, jnp.where(t <= i, t - 1, t))

    def score_tile(i, t):
        j = key_tile(i, t)
        k0 = pl.multiple_of(j * tile, tile)
        widx = jnp.where(j == i, 0, jnp.where(j < i, 1, 2))
        lhs = jnp.concatenate([k_ref[pl.ds(k0, tile), :], feat_ref[...]], axis=1)
        return jnp.dot(lhs, w_ref[widx], preferred_element_type=F32)

    def scores(i, t, s_ref):
        s_ref[...] = score_tile(i, t)

    def softmax_pv(i, t, s_ref, fixed, same_tile=False):
        j = key_tile(i, t)
        k0 = pl.multiple_of(j * tile, tile)
        q0 = i * tile
        for half in range(halves):
            cst = slope2 * (k0 - q0 - half * sw).astype(F32)
            tc = 0.0 if same_tile else jnp.where(j < i, cst, -cst)
            for comp in range(2):
                cols = slice((2 * half + comp) * sw, (2 * half + comp + 1) * sw)
                t_sc = s_ref[:, cols]
                if same_tile:
                    t_sc = t_sc + diag_ref[half]
                if fixed:
                    p = jnp.exp2(t_sc - (m_ref[:, cols] - tc))
                    acc1_ref[:, cols] += jnp.dot(vt1_ref[:, pl.ds(k0, tile)], p.astype(BF16),
                                                 preferred_element_type=F32)
                else:
                    m_old = m_ref[:, cols]
                    m_new = jnp.maximum(m_old, jnp.max(t_sc, axis=0, keepdims=True) + tc)
                    alpha = jnp.exp2(m_old - m_new)
                    p = jnp.exp2(t_sc - (m_new - tc))
                    l_ref[:, cols] = alpha * l_ref[:, cols] + jnp.sum(p, axis=0, keepdims=True)
                    m_ref[:, cols] = m_new
                    pv = jnp.dot(vt_ref[:, pl.ds(k0, tile)], p.astype(BF16),
                                 preferred_element_type=F32)
                    acc_ref[:, cols] = alpha * acc_ref[:, cols] + pv

    s_refs = (s0_ref, s1_ref)
    p_refs = (p0_ref, p1_ref)

    def probabilities(i, t, p_ref):
        j = key_tile(i, t)
        k0 = j * tile
        q0 = i * tile
        s = score_tile(i, t)
        for half in range(halves):
            cst = slope2 * (k0 - q0 - half * sw).astype(F32)
            tc = jnp.where(j < i, cst, -cst)
            for comp in range(2):
                cols = slice((2 * half + comp) * sw, (2 * half + comp + 1) * sw)
                p_ref[:, cols] = jnp.exp2(s[:, cols] - (m_ref[:, cols] - tc)).astype(BF16)

    def value_update(i, t, p_ref):
        k0 = pl.multiple_of(key_tile(i, t) * tile, tile)
        acc1_ref[...] += jnp.dot(vt1_ref[:, pl.ds(k0, tile)], p_ref[...],
                                 preferred_element_type=F32)

    def start_query_tile(i):
        q0 = pl.multiple_of(i * tile, tile)
        for half in range(halves):
            qcols = pl.ds(q0 + half * sw, sw)
            c0 = 2 * half * sw
            for variant in range(3):
                w_ref[variant, 0:B_QK_DIM, c0:c0 + sw] = qt_ref[0:B_QK_DIM, qcols]
                w_ref[variant, B_QK_DIM:B_V_DIM, c0 + sw:c0 + 2 * sw] = qt_ref[B_QK_DIM:, qcols]
        s = score_tile(i, 0)
        s_refs[0][...] = s
        ref = jnp.max(s, axis=0, keepdims=True)
        nref_ref[...] = ref
        qf = w_ref[0, 0:B_V_DIM, :].astype(F32)
        upper = jnp.sqrt(jnp.sum(qf * qf, axis=0, keepdims=True) * kmax2) * 1.01 + 1.0
        ok = (jnp.max(upper - ref) <= FIXED_REF_MAX_EXCESS) & (
            jnp.max(vmax) <= FIXED_REF_MAX_VALUE)
        flag_ref[0] = ok.astype(jnp.int32)

    def finish_query_tile(i):
        q0 = pl.multiple_of(i * tile, tile)
        o = acc_ref[...] / l_ref[...]
        for half in range(halves):
            c0 = 2 * half * sw
            od = (o[:, c0:c0 + sw] - lam * o[:, c0 + sw:c0 + 2 * sw]).T
            o_ref[pl.ds(q0 + half * sw, sw), :] = (
                _rms(od, g_ref[...]) * (1.0 - LAM_INIT)).astype(BF16)

    def key_tile_pipeline(i, fixed):
        def produce(t, parity):
            if fixed:
                probabilities(i, t, p_refs[parity])
            else:
                scores(i, t, s_refs[parity])

        def consume(t, parity):
            if fixed:
                value_update(i, t, p_refs[parity])
            else:
                softmax_pv(i, t, s_refs[parity], False)

        produce(1, 1)
        if fixed:
            finish_query_tile(jnp.maximum(i - 1, 0))
        softmax_pv(i, 0, s_refs[0], fixed, same_tile=True)

        def group(n, c2):
            for u in range(GROUP):
                produce(GROUP * n + u + 2, u % 2)
                consume(GROUP * n + u + 1, (u + 1) % 2)
            return c2

        n_groups = (n_tiles - 2) // GROUP
        lax.fori_loop(0, n_groups, group, 0)
        for t in range(n_groups * GROUP + 1, n_tiles):
            if t + 1 < n_tiles:
                produce(t + 1, (t + 1) % 2)
            else:
                assert (t - 1) % 2 == 0
                start_query_tile(jnp.minimum(i + 1, n_tiles - 1))
            consume(t, t % 2)

    def q_body(i, carry):
        use_fixed = flag_ref[0]

        @pl.when(use_fixed == 1)
        def _():
            m_ref[...] = nref_ref[...]
            acc1_ref[...] = jnp.zeros_like(acc1_ref)
            key_tile_pipeline(i, True)
            acc_ref[...] = acc1_ref[0:B_V_DIM, :]
            l_ref[...] = acc1_ref[B_V_DIM:B_V_DIM + 1, :]

        @pl.when(use_fixed != 1)
        def _():
            finish_query_tile(jnp.maximum(i - 1, 0))
            m_ref[...] = jnp.full_like(m_ref, NEG_INF)
            l_ref[...] = jnp.zeros_like(l_ref)
            acc_ref[...] = jnp.zeros_like(acc_ref)
            key_tile_pipeline(i, False)

        return carry

    acc_ref[...] = jnp.zeros_like(acc_ref)
    l_ref[...] = jnp.ones_like(l_ref)
    start_query_tile(0)
    lax.fori_loop(0, n_tiles, q_body, 0)
    finish_query_tile(n_tiles - 1)


def _attn_b(slopes_b, lq1, lk1, lq2, lk2, gain, qdt, kd, vdt, *, batch, seq, tile=512, sw=256):
    b_w, t = qdt.shape
    heads = b_w // B_V_DIM
    tile = min(tile, seq // 2)
    assert seq % (2 * tile) == 0 and tile % sw == 0
    r = lax.broadcasted_iota(jnp.int32, (tile, sw), 0)
    c = lax.broadcasted_iota(jnp.int32, (tile, sw), 1)
    d0 = (r - c).astype(F32)
    rk = jnp.arange(tile, dtype=jnp.int32)[:, None]
    fcol = jnp.arange(B_V_DIM, dtype=jnp.int32)[None, :]
    feat = jnp.where(fcol < 3, rk % 256,
                     jnp.where(fcol < 6, rk // 256, jnp.where(fcol < 9, 1, 0))).astype(BF16)
    kern = functools.partial(_attn_b_kernel, tile=tile, sw=sw, seq=seq)
    smem = pl.BlockSpec(memory_space=pltpu.SMEM)
    vec = lambda n: pl.BlockSpec((1, n), lambda b, h: (0, 0))
    tposed = pl.BlockSpec((B_V_DIM, seq), lambda b, h: (h, b))
    natural = pl.BlockSpec((seq, B_V_DIM), lambda b, h: (b, h))
    return pl.pallas_call(
        kern,
        grid=(batch, heads),
        in_specs=[smem, vec(B_QK_DIM), vec(B_QK_DIM), vec(B_QK_DIM), vec(B_QK_DIM),
                  vec(B_V_DIM),
                  pl.BlockSpec((tile, sw), lambda b, h: (0, 0)),
                  pl.BlockSpec((tile, B_V_DIM), lambda b, h: (0, 0)),
                  tposed, natural, tposed],
        out_specs=natural,
        out_shape=jax.ShapeDtypeStruct((t, b_w), BF16),
        scratch_shapes=[pltpu.VMEM((3, 2 * B_V_DIM, 2 * tile), BF16),
                        pltpu.VMEM((tile // sw, tile, sw), F32),
                        pltpu.VMEM((tile, 2 * tile), F32),
                        pltpu.VMEM((tile, 2 * tile), F32),
                        pltpu.VMEM((1, 2 * tile), F32),
                        pltpu.VMEM((1, 2 * tile), F32),
                        pltpu.VMEM((B_V_DIM, 2 * tile), F32),
                        pltpu.VMEM((B_V_DIM + ONES_ROWS, seq), BF16),
                        pltpu.VMEM((B_V_DIM + ONES_ROWS, 2 * tile), F32),
                        pltpu.VMEM((1, 2 * tile), F32),
                        pltpu.SMEM((1,), jnp.int32),
                        pltpu.VMEM((tile, 2 * tile), BF16),
                        pltpu.VMEM((tile, 2 * tile), BF16)],
        compiler_params=_cparams(("arbitrary", "arbitrary")),
        name="attn_diff",
    )(slopes_b, lq1, lk1, lq2, lk2, gain, d0, feat, qdt, kd, vdt)


def _outproj_kernel(a_ref, b_ref, w_ref, x_ref, mod_ref, g_ref, x1_ref, h2_ref, *, a_w):
    mix = (jnp.dot(a_ref[...], w_ref[0:a_w, :], preferred_element_type=F32)
           + jnp.dot(b_ref[...], w_ref[a_w:, :], preferred_element_type=F32))
    g1 = mod_ref[0, 2:3, :]
    sh2 = mod_ref[0, 3:4, :]
    sc2 = mod_ref[0, 4:5, :]
    x1 = x_ref[...] + g1 * mix
    x1_ref[...] = x1
    h2_ref[...] = (_rms(x1, g_ref[...]) * (1.0 + sc2) + sh2).astype(BF16)


def _outproj(out_a, out_b, w_bf, x2, mod3, gain, *, seq, tm=512):
    t, d = x2.shape
    a_w = out_a.shape[1]
    b_w = out_b.shape[1]
    tiles_per_batch = seq // tm
    row = lambda width: pl.BlockSpec((tm, width), lambda i: (i, 0))
    return pl.pallas_call(
        functools.partial(_outproj_kernel, a_w=a_w),
        grid=(t // tm,),
        in_specs=[row(a_w), row(b_w), _resident((a_w + b_w, d)), row(d),
                  pl.BlockSpec((1, N_MOD, d), lambda i: (i // tiles_per_batch, 0, 0)),
                  pl.BlockSpec((1, d), lambda i: (0, 0))],
        out_specs=[row(d), row(d)],
        out_shape=[jax.ShapeDtypeStruct((t, d), F32), jax.ShapeDtypeStruct((t, d), BF16)],
        compiler_params=_cparams(("arbitrary",)),
        name="outproj_norm2",
    )(out_a, out_b, w_bf, x2, mod3, gain)


def _ffn_kernel(h_ref, wgu_ref, wd_ref, x1_ref, mod_ref, fg_ref, o_ref):
    f = pl.program_id(1)

    @pl.when(f == 0)
    def _():
        o_ref[...] = jnp.zeros_like(o_ref)

    tf = wd_ref.shape[0]
    gu = jnp.dot(h_ref[...], wgu_ref[...], preferred_element_type=F32)
    g = gu[:, :tf]
    u = gu[:, tf:]
    a = (g * jax.nn.sigmoid(g) * u).astype(BF16)
    o_ref[...] += jnp.dot(a, wd_ref[...], preferred_element_type=F32)

    @pl.when(f == pl.num_programs(1) - 1)
    def _():
        g2 = mod_ref[0, 5:6, :]
        o_ref[...] = _rms(x1_ref[...] + g2 * o_ref[...], fg_ref[...])


def _ffn(h2, wg, wu, wd, x1, mod3, final_gain, *, seq, tm=512, tf=512):
    t, d = h2.shape
    ff = wg.shape[1]
    tm = min(tm, seq)
    if ff % tf:
        tf = 256
    assert ff % tf == 0 and t % tm == 0 and seq % tm == 0
    tiles_per_batch = seq // tm
    wgu = jnp.concatenate([wg.reshape(d, ff // tf, tf), wu.reshape(d, ff // tf, tf)],
                          axis=2).reshape(d, 2 * ff)
    return pl.pallas_call(
        _ffn_kernel,
        grid=(t // tm, ff // tf),
        in_specs=[pl.BlockSpec((tm, d), lambda i, f: (i, 0)),
                  pl.BlockSpec((d, 2 * tf), lambda i, f: (0, f)),
                  pl.BlockSpec((tf, d), lambda i, f: (f, 0)),
                  pl.BlockSpec((tm, d), lambda i, f: (i, 0)),
                  pl.BlockSpec((1, N_MOD, d), lambda i, f: (i // tiles_per_batch, 0, 0)),
                  pl.BlockSpec((1, d), lambda i, f: (0, 0))],
        out_specs=pl.BlockSpec((tm, d), lambda i, f: (i, 0)),
        out_shape=jax.ShapeDtypeStruct((t, d), F32),
        compiler_params=_cparams(("arbitrary", "arbitrary")),
        name="swiglu_ffn_final_norm",
    )(h2, wgu, wd, x1, mod3, final_gain)


def kernel(x, c, w_ada, b_ada, norm1_gain, w_in, a_sink, a_out_gain, diff_lq1, diff_lk1,
           diff_lq2, diff_lk2, diff_subln_gain, w_o, norm2_gain, w_gate, w_up, w_down,
           final_gain):
    batch, seq, d = x.shape
    assert w_ada.shape[0] == 1, "single-layer block"
    a_w = d // 2
    b_w = d - a_w
    a_heads = a_w // HEAD_DIM
    b_heads = b_w // B_V_DIM
    a_kv = A_KV_HEADS * HEAD_DIM
    n_heads = a_heads + b_heads
    slopes = 2.0 ** (-8.0 * jnp.arange(1, n_heads + 1, dtype=F32) / n_heads)

    rows = 8
    c_pad = jnp.zeros((rows, d), F32).at[:batch].set(c)
    mod = _ada(c_pad, w_ada[0], b_ada[0][None, :])[:batch]
    mod3 = mod.reshape(batch, N_MOD, d)

    x2 = x.reshape(batch * seq, d)
    qa, ka, va, qdt, kd, vdt = _inproj(
        x2, mod3, norm1_gain[0][None, :], w_in[0].astype(BF16),
        seq=seq, a_q=a_w, a_kv=a_kv, b_w=b_w)

    out_a = _attn_a(slopes[:a_heads], a_sink[0].astype(F32), qa, ka, va,
                    a_out_gain[0][None, :], batch=batch, seq=seq)
    out_b = _attn_b(slopes[a_heads:], diff_lq1[0][None, :], diff_lk1[0][None, :],
                    diff_lq2[0][None, :], diff_lk2[0][None, :], diff_subln_gain[0][None, :],
                    qdt, kd, vdt, batch=batch, seq=seq)

    x1, h2 = _outproj(out_a, out_b, w_o[0].astype(BF16), x2, mod3, norm2_gain[0][None, :],
                      seq=seq)
    out = _ffn(h2, w_gate[0].astype(BF16), w_up[0].astype(BF16), w_down[0].astype(BF16),
               x1, mod3, final_gain[None, :], seq=seq)
    return out.reshape(batch, seq, d)
```
